```python
import math
import jax
import jax.numpy as jnp
from jax import lax
import numpy as np

D_MODEL = 1024
BATCH = 8
SEQ = 4096
DEPTH = 1

SSD_D_INNER = 1024
SSD_HEADDIM = 64
SSD_HEADS = SSD_D_INNER // SSD_HEADDIM
SSD_GROUPS = 2
SSD_HPG = SSD_HEADS // SSD_GROUPS
SSD_STATE = 128
CONV_WIDTH = 3
CHUNK = 128
SSD_CONV_DIM = SSD_D_INNER + 2 * SSD_GROUPS * SSD_STATE

MLA_HEADS = 16
QK_NOPE = 64
QK_ROPE = 32
V_HEAD = 64
Q_LORA = 384
KV_LORA = 256
Q_BLOCK = 128
ROPE_THETA = 10000.0

PEER_HEADS = 8
N_KEYS = 128
N_EXPERTS = N_KEYS * N_KEYS
KEY_HALF = 128
PEER_TOPK = 16
TOK_BLOCK = 128

EPS = 1e-6

IN_SIZES = (SSD_D_INNER, SSD_CONV_DIM, 2 * SSD_HEADS, Q_LORA, KV_LORA, QK_ROPE, 2 * D_MODEL)
IN_COLS = sum(IN_SIZES)

kernel_name = "hybrid_ssd_mla_peer_block"


def _split_points(sizes):
    pts, acc = [], 0
    for s in sizes[:-1]:
        acc += s
        pts.append(acc)
    return pts


def rms_norm(x, w):
    xf = x.astype(jnp.float32)
    y = xf * lax.rsqrt(jnp.mean(xf * xf, axis=-1, keepdims=True) + EPS)
    return (y * w.astype(jnp.float32)).astype(x.dtype)


def depthwise_conv_centred(x, w, b):
    c = x.shape[-1]
    k = w.shape[0]
    y = lax.conv_general_dilated(x, w[:, None, :], window_strides=(1,),
                                 padding=[(k // 2, k // 2)],
                                 dimension_numbers=('NWC', 'WIO', 'NWC'),
                                 feature_group_count=c)
    return y + b


def segsum(a):
    t = a.shape[-1]
    cs = jnp.cumsum(a, axis=-1)
    diff = cs[..., :, None] - cs[..., None, :]
    mask = jnp.tril(jnp.ones((t, t), dtype=bool))
    return jnp.where(mask, diff, -jnp.inf)


def ssd_scan(xh, dt, A, Bm, Cm):
    b, l, g, r, p = xh.shape
    n = Bm.shape[-1]
    c = l // CHUNK
    f32 = jnp.float32
    xd = (xh.astype(f32) * dt.astype(f32)[..., None]).reshape(b, c, CHUNK, g, r, p)
    a = (dt.astype(f32) * A.astype(f32)).reshape(b, c, CHUNK, g, r)
    a = jnp.moveaxis(a, 2, -1)
    Bc = Bm.astype(f32).reshape(b, c, CHUNK, g, n)
    Cc = Cm.astype(f32).reshape(b, c, CHUNK, g, n)
    a_cs = jnp.cumsum(a, axis=-1)
    L = jnp.exp(segsum(a))
    CB = jnp.einsum('bclgn,bcsgn->bcgls', Cc, Bc)
    y_diag = jnp.einsum('bcgls,bcgrls,bcsgrp->bclgrp', CB, L, xd)
    decay_states = jnp.exp(a_cs[..., -1:] - a_cs)
    states = jnp.einsum('bcsgn,bcgrs,bcsgrp->bcgrpn', Bc, decay_states, xd)
    chunk_tot = jnp.moveaxis(a_cs[..., -1], 1, -1)
    decay_chunk = jnp.exp(segsum(jnp.pad(chunk_tot, ((0, 0), (0, 0), (0, 0), (1, 0)))))
    states0 = jnp.pad(states, ((0, 0), (1, 0), (0, 0), (0, 0), (0, 0), (0, 0)))
    new_states = jnp.einsum('bgrzc,bcgrpn->bzgrpn', decay_chunk, states0)
    prev = new_states[:, :-1]
    state_decay_out = jnp.exp(a_cs)
    y_off = jnp.einsum('bclgn,bcgrpn,bcgrl->bclgrp', Cc, prev, state_decay_out)
    return (y_diag + y_off).reshape(b, l, g, r, p)


def rope_tables(positions, dim, dtype):
    inv = 1.0 / (ROPE_THETA ** (jnp.arange(0, dim, 2, dtype=jnp.float32) / dim))
    ang = positions.astype(jnp.float32)[..., None] * inv
    return jnp.cos(ang).astype(dtype), jnp.sin(ang).astype(dtype)


def apply_rope(x, cos, sin):
    x1, x2 = jnp.split(x, 2, axis=-1)
    return jnp.concatenate([x1 * cos - x2 * sin, x1 * sin + x2 * cos], axis=-1)


def mla_attention(q_nope, q_rope, k_nope, k_rope, v):
    b, s, h, _ = q_nope.shape
    nb = s // Q_BLOCK
    scale = (QK_NOPE + QK_ROPE) ** -0.5

    def block(args):
        qn, qr = args
        sc = (jnp.einsum('bqhd,bkhd->bhqk', qn, k_nope)
              + jnp.einsum('bqhd,bkd->bhqk', qr, k_rope))
        p = jax.nn.softmax(sc.astype(jnp.float32) * scale, axis=-1).astype(v.dtype)
        return jnp.einsum('bhqk,bkhd->bqhd', p, v)

    qn_b = q_nope.reshape(b, nb, Q_BLOCK, h, QK_NOPE).swapaxes(0, 1)
    qr_b = q_rope.reshape(b, nb, Q_BLOCK, h, QK_ROPE).swapaxes(0, 1)
    out = lax.map(block, (qn_b, qr_b))
    return out.swapaxes(0, 1).reshape(b, s, h * V_HEAD)


def peer_ffn(u, w_q, sub_keys, peer_u, peer_v):
    b, s, d = u.shape
    ub = u.reshape((b * s) // TOK_BLOCK, TOK_BLOCK, d)

    def block(ut):
        q = (ut @ w_q).reshape(TOK_BLOCK, PEER_HEADS, 2, KEY_HALF)
        sc = jnp.einsum('thid,ind->thin', q, sub_keys).astype(jnp.float32)
        v_top, i_top = lax.top_k(sc, PEER_TOPK)
        cand = v_top[:, :, 0, :, None] + v_top[:, :, 1, None, :]
        cand_idx = i_top[:, :, 0, :, None] * N_KEYS + i_top[:, :, 1, None, :]
        cand = cand.reshape(TOK_BLOCK, PEER_HEADS, PEER_TOPK * PEER_TOPK)
        cand_idx = cand_idx.reshape(TOK_BLOCK, PEER_HEADS, PEER_TOPK * PEER_TOPK)
        best, pos = lax.top_k(cand, PEER_TOPK)
        idx = jnp.take_along_axis(cand_idx, pos, axis=-1)
        gate = jax.nn.softmax(best, axis=-1).astype(ut.dtype)
        act = jax.nn.gelu(jnp.einsum('td,thkd->thk', ut, peer_u[idx]), approximate=False)
        return jnp.einsum('thk,thkd->td', gate * act, peer_v[idx])

    return lax.map(block, ub).reshape(b, s, d)


def setup_inputs(seed: int = 0) -> dict:
    key = jax.random.key(seed)
    ks = jax.random.split(key, 32)
    f32 = jnp.float32
    nrm = lambda k, shape, scale: jax.random.normal(k, shape, f32) * scale
    gain = lambda k, n: 1.0 + 0.1 * jax.random.normal(k, (n,), f32)

    x = jax.random.normal(ks[0], (BATCH, SEQ, D_MODEL), f32)
    offsets = jax.random.randint(ks[1], (BATCH, 1), 0, 4096, dtype=jnp.int32)
    positions = offsets + jnp.arange(SEQ, dtype=jnp.int32)[None, :]

    dt_init_f = jnp.exp(jax.random.uniform(ks[6], (SSD_HEADS,), f32, math.log(1e-3), math.log(1e-1)))
    dt_init_b = jnp.exp(jax.random.uniform(ks[7], (SSD_HEADS,), f32, math.log(1e-3), math.log(1e-1)))
    inv_softplus = lambda t: t + jnp.log(-jnp.expm1(-t))

    return {
        "x": x,
        "positions": positions,
        "norm_mix_w": gain(ks[2], D_MODEL),
        "w_in": nrm(ks[3], (D_MODEL, IN_COLS), D_MODEL ** -0.5),
        "conv_w": nrm(ks[4], (CONV_WIDTH, SSD_CONV_DIM), CONV_WIDTH ** -0.5),
        "conv_b": nrm(ks[5], (SSD_CONV_DIM,), 0.02),
        "dt_bias_fwd": inv_softplus(dt_init_f),
        "dt_bias_bwd": inv_softplus(dt_init_b),
        "a_log_fwd": jnp.log(jax.random.uniform(ks[8], (SSD_HEADS,), f32, 1.0, 16.0)),
        "a_log_bwd": jnp.log(jax.random.uniform(ks[9], (SSD_HEADS,), f32, 1.0, 16.0)),
        "d_skip": gain(ks[10], SSD_HEADS),
        "ssd_norm_w": gain(ks[11], SSD_D_INNER),
        "q_a_norm_w": gain(ks[12], Q_LORA),
        "w_q_b": nrm(ks[13], (Q_LORA, MLA_HEADS * (QK_NOPE + QK_ROPE)), Q_LORA ** -0.5),
        "kv_a_norm_w": gain(ks[14], KV_LORA),
        "w_kv_b": nrm(ks[15], (KV_LORA, MLA_HEADS * (QK_NOPE + V_HEAD)), KV_LORA ** -0.5),
        "w_proj_ssd": nrm(ks[16], (SSD_D_INNER, D_MODEL), SSD_D_INNER ** -0.5),
        "w_proj_mla": nrm(ks[17], (MLA_HEADS * V_HEAD, D_MODEL), (MLA_HEADS * V_HEAD) ** -0.5),
        "w_out": nrm(ks[18], (D_MODEL, D_MODEL), D_MODEL ** -0.5),
        "norm_ffn_w": gain(ks[19], D_MODEL),
        "peer_w_q": nrm(ks[20], (D_MODEL, PEER_HEADS * 2 * KEY_HALF), D_MODEL ** -0.5),
        "peer_sub_keys": nrm(ks[21], (2, N_KEYS, KEY_HALF), KEY_HALF ** -0.5),
        "peer_u": nrm(ks[22], (N_EXPERTS, D_MODEL), D_MODEL ** -0.5),
        "peer_v": nrm(ks[23], (N_EXPERTS, D_MODEL), 0.5),
        "norm_final_w": gain(ks[24], D_MODEL),
    }


def reference(x, positions, norm_mix_w, w_in, conv_w, conv_b, dt_bias_fwd, dt_bias_bwd,
              a_log_fwd, a_log_bwd, d_skip, ssd_norm_w, q_a_norm_w, w_q_b, kv_a_norm_w,
              w_kv_b, w_proj_ssd, w_proj_mla, w_out, norm_ffn_w, peer_w_q, peer_sub_keys,
              peer_u, peer_v, norm_final_w):
    b, s, _ = x.shape
    h = x
    for _layer in range(DEPTH):
        u = rms_norm(h, norm_mix_w)
        proj = u @ w_in
        z, xbc, dt_raw, q_a, kv_a, k_rope, gate_logits = jnp.split(proj, _split_points(IN_SIZES), axis=-1)

        xbc = jax.nn.silu(depthwise_conv_centred(xbc, conv_w, conv_b))
        xs, Bm, Cm = jnp.split(xbc, [SSD_D_INNER, SSD_D_INNER + SSD_GROUPS * SSD_STATE], axis=-1)
        xh = xs.reshape(b, s, SSD_GROUPS, SSD_HPG, SSD_HEADDIM)
        Bm = Bm.reshape(b, s, SSD_GROUPS, SSD_STATE)
        Cm = Cm.reshape(b, s, SSD_GROUPS, SSD_STATE)
        dt_f_raw, dt_b_raw = jnp.split(dt_raw, 2, axis=-1)
        dt_f = jax.nn.softplus(dt_f_raw + dt_bias_fwd).reshape(b, s, SSD_GROUPS, SSD_HPG)
        dt_b = jax.nn.softplus(dt_b_raw + dt_bias_bwd).reshape(b, s, SSD_GROUPS, SSD_HPG)
        A_f = -jnp.exp(a_log_fwd).reshape(SSD_GROUPS, SSD_HPG)
        A_b = -jnp.exp(a_log_bwd).reshape(SSD_GROUPS, SSD_HPG)
        flip = lambda t: jnp.flip(t, axis=1)
        y_f = ssd_scan(xh, dt_f, A_f, Bm, Cm)
        y_b = flip(ssd_scan(flip(xh), flip(dt_b), A_b, flip(Bm), flip(Cm)))
        y = (y_f + y_b).astype(h.dtype) + xh * d_skip.reshape(SSD_GROUPS, SSD_HPG, 1)
        yg = (y.reshape(b, s, SSD_D_INNER) * jax.nn.silu(z)).reshape(b, s, SSD_GROUPS, SSD_D_INNER // SSD_GROUPS)
        y_ssd = rms_norm(yg, ssd_norm_w.reshape(SSD_GROUPS, SSD_D_INNER // SSD_GROUPS)).reshape(b, s, SSD_D_INNER)

        q = (rms_norm(q_a, q_a_norm_w) @ w_q_b).reshape(b, s, MLA_HEADS, QK_NOPE + QK_ROPE)
        q_nope, q_rope = jnp.split(q, [QK_NOPE], axis=-1)
        kv = (rms_norm(kv_a, kv_a_norm_w) @ w_kv_b).reshape(b, s, MLA_HEADS, QK_NOPE + V_HEAD)
        k_nope, v = jnp.split(kv, [QK_NOPE], axis=-1)
        cos, sin = rope_tables(positions, QK_ROPE, h.dtype)
        q_rope = apply_rope(q_rope, cos[:, :, None, :], sin[:, :, None, :])
        k_rope = apply_rope(k_rope, cos, sin)
        y_mla = mla_attention(q_nope, q_rope, k_nope, k_rope, v)

        g_ssd, g_mla = jnp.split(jax.nn.sigmoid(gate_logits), 2, axis=-1)
        m = g_ssd * (y_ssd @ w_proj_ssd) + g_mla * (y_mla @ w_proj_mla)
        h = h + m @ w_out

        h = h + peer_ffn(rms_norm(h, norm_ffn_w), peer_w_q, peer_sub_keys, peer_u, peer_v)
    return rms_norm(h, norm_final_w)
```

```python
import functools
import math

import jax
import jax.numpy as jnp
from jax import lax
from jax.experimental import pallas as pl
from jax.experimental.pallas import tpu as pltpu

D_MODEL = 1024
SSD_D_INNER = 1024
SSD_HEADDIM = 64
SSD_HEADS = 16
SSD_GROUPS = 2
SSD_HPG = 8
SSD_STATE = 128
CONV_WIDTH = 3
CHUNK = 128
SSD_CONV_DIM = SSD_D_INNER + 2 * SSD_GROUPS * SSD_STATE
MLA_HEADS = 16
QK_NOPE = 64
QK_ROPE = 32
V_HEAD = 64
Q_LORA = 384
KV_LORA = 256
Q_BLOCK = 128
ROPE_THETA = 10000.0
PEER_HEADS = 8
N_KEYS = 128
KEY_HALF = 128
PEER_TOPK = 16
TOK_BLOCK = 128
EPS = 1e-6
IN_SIZES = (SSD_D_INNER, SSD_CONV_DIM, 2 * SSD_HEADS, Q_LORA, KV_LORA, QK_ROPE, 2 * D_MODEL)


def _split_points(sizes):
    pts, acc = [], 0
    for s in sizes[:-1]:
        acc += s
        pts.append(acc)
    return pts


def _rms(x, w):
    y = x * lax.rsqrt(jnp.mean(x * x, axis=-1, keepdims=True) + EPS)
    return y * w


def _segsum(a):
    t = a.shape[-1]
    cs = jnp.cumsum(a, axis=-1)
    diff = cs[..., :, None] - cs[..., None, :]
    mask = jnp.tril(jnp.ones((t, t), dtype=bool))
    return jnp.where(mask, diff, -jnp.inf)


def _ssd_scan(xh, dt, A, Bm, Cm):
    b, l, g, r, p = xh.shape
    n = Bm.shape[-1]
    c = l // CHUNK
    xd = (xh * dt[..., None]).reshape(b, c, CHUNK, g, r, p)
    a = (dt * A).reshape(b, c, CHUNK, g, r)
    a = jnp.moveaxis(a, 2, -1)
    Bc = Bm.reshape(b, c, CHUNK, g, n)
    Cc = Cm.reshape(b, c, CHUNK, g, n)
    a_cs = jnp.cumsum(a, axis=-1)
    L = jnp.exp(_segsum(a))
    CB = jnp.einsum('bclgn,bcsgn->bcgls', Cc, Bc)
    y_diag = jnp.einsum('bcgls,bcgrls,bcsgrp->bclgrp', CB, L, xd)
    decay_states = jnp.exp(a_cs[..., -1:] - a_cs)
    states = jnp.einsum('bcsgn,bcgrs,bcsgrp->bcgrpn', Bc, decay_states, xd)
    chunk_tot = jnp.moveaxis(a_cs[..., -1], 1, -1)
    decay_chunk = jnp.exp(_segsum(jnp.pad(chunk_tot, ((0, 0), (0, 0), (0, 0), (1, 0)))))
    states0 = jnp.pad(states, ((0, 0), (1, 0), (0, 0), (0, 0), (0, 0), (0, 0)))
    new_states = jnp.einsum('bgrzc,bcgrpn->bzgrpn', decay_chunk, states0)
    prev = new_states[:, :-1]
    y_off = jnp.einsum('bclgn,bcgrpn,bcgrl->bclgrp', Cc, prev, jnp.exp(a_cs))
    return (y_diag + y_off).reshape(b, l, g, r, p)


def _rope_tables(positions, dim):
    inv = 1.0 / (ROPE_THETA ** (jnp.arange(0, dim, 2, dtype=jnp.float32) / dim))
    ang = positions.astype(jnp.float32)[..., None] * inv
    return jnp.cos(ang), jnp.sin(ang)


def _apply_rope(x, cos, sin):
    x1, x2 = jnp.split(x, 2, axis=-1)
    return jnp.concatenate([x1 * cos - x2 * sin, x1 * sin + x2 * cos], axis=-1)


def _mla_attention(q_nope, q_rope, k_nope, k_rope, v):
    b, s, h, _ = q_nope.shape
    nb = s // Q_BLOCK
    scale = (QK_NOPE + QK_ROPE) ** -0.5

    def block(args):
        qn, qr = args
        sc = (jnp.einsum('bqhd,bkhd->bhqk', qn, k_nope)
              + jnp.einsum('bqhd,bkd->bhqk', qr, k_rope))
        p = jax.nn.softmax(sc * scale, axis=-1)
        return jnp.einsum('bhqk,bkhd->bqhd', p, v)

    qn_b = q_nope.reshape(b, nb, Q_BLOCK, h, QK_NOPE).swapaxes(0, 1)
    qr_b = q_rope.reshape(b, nb, Q_BLOCK, h, QK_ROPE).swapaxes(0, 1)
    out = lax.map(block, (qn_b, qr_b))
    return out.swapaxes(0, 1).reshape(b, s, h * V_HEAD)


def _peer_ffn(u, w_q, sub_keys, peer_u, peer_v):
    b, s, d = u.shape
    n_exp = peer_u.shape[0]
    ub = u.reshape((b * s) // TOK_BLOCK, TOK_BLOCK, d)

    def block(ut):
        q = (ut @ w_q).reshape(TOK_BLOCK, PEER_HEADS, 2, KEY_HALF)
        sc = jnp.einsum('thid,ind->thin', q, sub_keys)
        v_top, i_top = lax.top_k(sc, PEER_TOPK)
        cand = v_top[:, :, 0, :, None] + v_top[:, :, 1, None, :]
        cand_idx = i_top[:, :, 0, :, None] * N_KEYS + i_top[:, :, 1, None, :]
        cand = cand.reshape(TOK_BLOCK, PEER_HEADS, PEER_TOPK * PEER_TOPK)
        cand_idx = cand_idx.reshape(TOK_BLOCK, PEER_HEADS, PEER_TOPK * PEER_TOPK)
        best, pos = lax.top_k(cand, PEER_TOPK)
        idx = jnp.take_along_axis(cand_idx, pos, axis=-1)
        gate = jax.nn.softmax(best, axis=-1)
        act = jax.nn.gelu(jnp.einsum('td,thkd->thk', ut, peer_u[idx]), approximate=False)
        return jnp.einsum('thk,thkd->td', gate * act, peer_v[idx])

    return lax.map(block, ub).reshape(b, s, d)


def _final_norm_kernel(h_ref, p_ref, w_ref, o_ref):
    hh = h_ref[...] + p_ref[...]
    ms = jnp.mean(hh * hh, axis=-1, keepdims=True)
    o_ref[...] = hh * lax.rsqrt(ms + EPS) * w_ref[...]


def _final_norm(h, p, w):
    t, d = h.shape
    tb = 512
    return pl.pallas_call(
        _final_norm_kernel,
        grid=(t // tb,),
        in_specs=[pl.BlockSpec((tb, d), lambda i: (i, 0)),
                  pl.BlockSpec((tb, d), lambda i: (i, 0)),
                  pl.BlockSpec((1, d), lambda i: (0, 0))],
        out_specs=pl.BlockSpec((tb, d), lambda i: (i, 0)),
        out_shape=jax.ShapeDtypeStruct((t, d), jnp.float32),
    )(h, p, w.reshape(1, d))


def kernel(x, positions, norm_mix_w, w_in, conv_w, conv_b, dt_bias_fwd, dt_bias_bwd, a_log_fwd, a_log_bwd, d_skip, ssd_norm_w, q_a_norm_w, w_q_b, kv_a_norm_w, w_kv_b, w_proj_ssd, w_proj_mla, w_out, norm_ffn_w, peer_w_q, peer_sub_keys, peer_u, peer_v, norm_final_w):
    b, s, _ = x.shape
    h = x
    u = _rms(h, norm_mix_w)
    proj = u @ w_in
    z, xbc, dt_raw, q_a, kv_a, k_rope, gate_logits = jnp.split(proj, _split_points(IN_SIZES), axis=-1)

    k = conv_w.shape[0]
    xbc = lax.conv_general_dilated(xbc, conv_w[:, None, :], window_strides=(1,),
                                   padding=[(k // 2, k // 2)],
                                   dimension_numbers=('NWC', 'WIO', 'NWC'),
                                   feature_group_count=xbc.shape[-1]) + conv_b
    xbc = jax.nn.silu(xbc)
    xs, Bm, Cm = jnp.split(xbc, [SSD_D_INNER, SSD_D_INNER + SSD_GROUPS * SSD_STATE], axis=-1)
    xh = xs.reshape(b, s, SSD_GROUPS, SSD_HPG, SSD_HEADDIM)
    Bm = Bm.reshape(b, s, SSD_GROUPS, SSD_STATE)
    Cm = Cm.reshape(b, s, SSD_GROUPS, SSD_STATE)
    dt_f_raw, dt_b_raw = jnp.split(dt_raw, 2, axis=-1)
    dt_f = jax.nn.softplus(dt_f_raw + dt_bias_fwd).reshape(b, s, SSD_GROUPS, SSD_HPG)
    dt_b = jax.nn.softplus(dt_b_raw + dt_bias_bwd).reshape(b, s, SSD_GROUPS, SSD_HPG)
    A_f = -jnp.exp(a_log_fwd).reshape(SSD_GROUPS, SSD_HPG)
    A_b = -jnp.exp(a_log_bwd).reshape(SSD_GROUPS, SSD_HPG)
    flip = lambda t: jnp.flip(t, axis=1)
    y_f = _ssd_scan(xh, dt_f, A_f, Bm, Cm)
    y_b = flip(_ssd_scan(flip(xh), flip(dt_b), A_b, flip(Bm), flip(Cm)))
    y = (y_f + y_b) + xh * d_skip.reshape(SSD_GROUPS, SSD_HPG, 1)
    yg = (y.reshape(b, s, SSD_D_INNER) * jax.nn.silu(z)).reshape(b, s, SSD_GROUPS, SSD_D_INNER // SSD_GROUPS)
    y_ssd = _rms(yg, ssd_norm_w.reshape(SSD_GROUPS, SSD_D_INNER // SSD_GROUPS)).reshape(b, s, SSD_D_INNER)

    q = (_rms(q_a, q_a_norm_w) @ w_q_b).reshape(b, s, MLA_HEADS, QK_NOPE + QK_ROPE)
    q_nope, q_rope = jnp.split(q, [QK_NOPE], axis=-1)
    kv = (_rms(kv_a, kv_a_norm_w) @ w_kv_b).reshape(b, s, MLA_HEADS, QK_NOPE + V_HEAD)
    k_nope, v = jnp.split(kv, [QK_NOPE], axis=-1)
    cos, sin = _rope_tables(positions, QK_ROPE)
    q_rope = _apply_rope(q_rope, cos[:, :, None, :], sin[:, :, None, :])
    k_rope = _apply_rope(k_rope, cos, sin)
    y_mla = _mla_attention(q_nope, q_rope, k_nope, k_rope, v)

    g_ssd, g_mla = jnp.split(jax.nn.sigmoid(gate_logits), 2, axis=-1)
    m = g_ssd * (y_ssd @ w_proj_ssd) + g_mla * (y_mla @ w_proj_mla)
    h = h + m @ w_out

    pf = _peer_ffn(_rms(h, norm_ffn_w), peer_w_q, peer_sub_keys, peer_u, peer_v)
    out = _final_norm(h.reshape(b * s, D_MODEL), pf.reshape(b * s, D_MODEL), norm_final_w)
    return out.reshape(b, s, D_MODEL)
```

```python
import functools
import math

import jax
import jax.numpy as jnp
from jax import lax
from jax.experimental import pallas as pl
from jax.experimental.pallas import tpu as pltpu

D_MODEL = 1024
SSD_D_INNER = 1024
SSD_HEADDIM = 64
SSD_HEADS = 16
SSD_GROUPS = 2
SSD_HPG = 8
SSD_STATE = 128
CONV_WIDTH = 3
CHUNK = 128
SSD_CONV_DIM = SSD_D_INNER + 2 * SSD_GROUPS * SSD_STATE
MLA_HEADS = 16
QK_NOPE = 64
QK_ROPE = 32
V_HEAD = 64
Q_LORA = 384
KV_LORA = 256
Q_BLOCK = 128
ROPE_THETA = 10000.0
PEER_HEADS = 8
N_KEYS = 128
KEY_HALF = 128
PEER_TOPK = 16
TOK_BLOCK = 128
EPS = 1e-6
IN_SIZES = (SSD_D_INNER, SSD_CONV_DIM, 2 * SSD_HEADS, Q_LORA, KV_LORA, QK_ROPE, 2 * D_MODEL)


def _split_points(sizes):
    pts, acc = [], 0
    for s in sizes[:-1]:
        acc += s
        pts.append(acc)
    return pts


def _rms(x, w):
    y = x * lax.rsqrt(jnp.mean(x * x, axis=-1, keepdims=True) + EPS)
    return y * w


def _segsum(a):
    t = a.shape[-1]
    cs = jnp.cumsum(a, axis=-1)
    diff = cs[..., :, None] - cs[..., None, :]
    mask = jnp.tril(jnp.ones((t, t), dtype=bool))
    return jnp.where(mask, diff, -jnp.inf)


def _ssd_scan(xh, dt, A, Bm, Cm):
    b, l, g, r, p = xh.shape
    n = Bm.shape[-1]
    c = l // CHUNK
    xd = (xh * dt[..., None]).reshape(b, c, CHUNK, g, r, p)
    a = (dt * A).reshape(b, c, CHUNK, g, r)
    a = jnp.moveaxis(a, 2, -1)
    Bc = Bm.reshape(b, c, CHUNK, g, n)
    Cc = Cm.reshape(b, c, CHUNK, g, n)
    a_cs = jnp.cumsum(a, axis=-1)
    L = jnp.exp(_segsum(a))
    CB = jnp.einsum('bclgn,bcsgn->bcgls', Cc, Bc)
    y_diag = jnp.einsum('bcgls,bcgrls,bcsgrp->bclgrp', CB, L, xd)
    decay_states = jnp.exp(a_cs[..., -1:] - a_cs)
    states = jnp.einsum('bcsgn,bcgrs,bcsgrp->bcgrpn', Bc, decay_states, xd)
    chunk_tot = jnp.moveaxis(a_cs[..., -1], 1, -1)
    decay_chunk = jnp.exp(_segsum(jnp.pad(chunk_tot, ((0, 0), (0, 0), (0, 0), (1, 0)))))
    states0 = jnp.pad(states, ((0, 0), (1, 0), (0, 0), (0, 0), (0, 0), (0, 0)))
    new_states = jnp.einsum('bgrzc,bcgrpn->bzgrpn', decay_chunk, states0)
    prev = new_states[:, :-1]
    y_off = jnp.einsum('bclgn,bcgrpn,bcgrl->bclgrp', Cc, prev, jnp.exp(a_cs))
    return (y_diag + y_off).reshape(b, l, g, r, p)


def _rope_tables(positions, dim):
    inv = 1.0 / (ROPE_THETA ** (jnp.arange(0, dim, 2, dtype=jnp.float32) / dim))
    ang = positions.astype(jnp.float32)[..., None] * inv
    return jnp.cos(ang), jnp.sin(ang)


def _apply_rope(x, cos, sin):
    x1, x2 = jnp.split(x, 2, axis=-1)
    return jnp.concatenate([x1 * cos - x2 * sin, x1 * sin + x2 * cos], axis=-1)


def _mla_attention(q_nope, q_rope, k_nope, k_rope, v):
    b, s, h, _ = q_nope.shape
    nb = s // Q_BLOCK
    scale = (QK_NOPE + QK_ROPE) ** -0.5

    def block(args):
        qn, qr = args
        sc = (jnp.einsum('bqhd,bkhd->bhqk', qn, k_nope)
              + jnp.einsum('bqhd,bkd->bhqk', qr, k_rope))
        p = jax.nn.softmax(sc * scale, axis=-1)
        return jnp.einsum('bhqk,bkhd->bqhd', p, v)

    qn_b = q_nope.reshape(b, nb, Q_BLOCK, h, QK_NOPE).swapaxes(0, 1)
    qr_b = q_rope.reshape(b, nb, Q_BLOCK, h, QK_ROPE).swapaxes(0, 1)
    out = lax.map(block, (qn_b, qr_b))
    return out.swapaxes(0, 1).reshape(b, s, h * V_HEAD)


def _peer_ffn(u, w_q, sub_keys, peer_u, peer_v):
    b, s, d = u.shape
    n_exp = peer_u.shape[0]
    ub = u.reshape((b * s) // TOK_BLOCK, TOK_BLOCK, d)

    def block(ut):
        q = (ut @ w_q).reshape(TOK_BLOCK, PEER_HEADS, 2, KEY_HALF)
        sc = jnp.einsum('thid,ind->thin', q, sub_keys)
        v_top, i_top = lax.top_k(sc, PEER_TOPK)
        cand = v_top[:, :, 0, :, None] + v_top[:, :, 1, None, :]
        cand_idx = i_top[:, :, 0, :, None] * N_KEYS + i_top[:, :, 1, None, :]
        cand = cand.reshape(TOK_BLOCK, PEER_HEADS, PEER_TOPK * PEER_TOPK)
        cand_idx = cand_idx.reshape(TOK_BLOCK, PEER_HEADS, PEER_TOPK * PEER_TOPK)
        best, pos = lax.top_k(cand, PEER_TOPK)
        idx = jnp.take_along_axis(cand_idx, pos, axis=-1)
        gate = jax.nn.softmax(best, axis=-1)
        act = jax.nn.gelu(jnp.einsum('td,thkd->thk', ut, peer_u[idx]), approximate=False)
        return jnp.einsum('thk,thkd->td', gate * act, peer_v[idx])

    return lax.map(block, ub).reshape(b, s, d)


PEER_TB = 128
PEER_K = PEER_HEADS * PEER_TOPK
HALF_D = D_MODEL // 2
NEG_INF = float("-inf")


def _top16_rows(x, extra=None):
    rows = x.shape[0]
    iota = lax.broadcasted_iota(jnp.int32, x.shape, 0)
    vals, poss, exts = [], [], []
    for _ in range(PEER_TOPK):
        m = jnp.max(x, axis=0, keepdims=True)
        p = jnp.min(jnp.where(x == m, iota, rows), axis=0, keepdims=True)
        hit = iota == p
        if extra is not None:
            exts.append(jnp.sum(jnp.where(hit, extra, 0), axis=0, keepdims=True))
        x = jnp.where(hit, NEG_INF, x)
        vals.append(m)
        poss.append(p)
    vals = jnp.concatenate(vals, axis=0)
    poss = jnp.concatenate(poss, axis=0)
    if extra is None:
        return vals, poss
    return vals, poss, jnp.concatenate(exts, axis=0)


def _peer_topk_kernel(q_ref, keys_ref, idx_ref, gate_ref):
    for h in range(PEER_HEADS):
        v, ids = [], []
        for i in range(2):
            c0 = (h * 2 + i) * KEY_HALF
            qhi = q_ref[:, c0:c0 + KEY_HALF]
            sc = lax.dot_general(keys_ref[i], qhi, (((1,), (1,)), ((), ())),
                                 preferred_element_type=jnp.float32)
            vi, pi = _top16_rows(sc)
            v.append(vi)
            ids.append(pi)
        cand = jnp.concatenate([v[0][a:a + 1, :] + v[1] for a in range(PEER_TOPK)], axis=0)
        cidx = jnp.concatenate([ids[0][a:a + 1, :] * N_KEYS + ids[1] for a in range(PEER_TOPK)], axis=0)
        best, _, eidx = _top16_rows(cand, cidx)
        e = jnp.exp(best - best[0:1, :])
        gate = e / jnp.sum(e, axis=0, keepdims=True)
        idx_ref[h * PEER_TOPK:(h + 1) * PEER_TOPK, :] = eidx
        gate_ref[h * PEER_TOPK:(h + 1) * PEER_TOPK, :] = gate


def _peer_topk(q, sub_keys):
    t = q.shape[0]
    return pl.pallas_call(
        _peer_topk_kernel,
        grid=(t // PEER_TB,),
        in_specs=[pl.BlockSpec((PEER_TB, q.shape[1]), lambda i: (i, 0)),
                  pl.BlockSpec(sub_keys.shape, lambda i: (0, 0, 0))],
        out_specs=[pl.BlockSpec((PEER_K, PEER_TB), lambda i: (0, i)),
                   pl.BlockSpec((PEER_K, PEER_TB), lambda i: (0, i))],
        out_shape=[jax.ShapeDtypeStruct((PEER_K, t), jnp.int32),
                   jax.ShapeDtypeStruct((PEER_K, t), jnp.float32)],
        name="peer_topk",
    )(q, sub_keys)


def _pack_table(tab):
    e = tab.shape[0]
    tb = tab.astype(jnp.bfloat16)
    lo = lax.bitcast_convert_type(tb[:, :HALF_D], jnp.uint16).astype(jnp.uint32)
    hi = lax.bitcast_convert_type(tb[:, HALF_D:], jnp.uint16).astype(jnp.uint32)
    return (lo | (hi << 16)).reshape(e, HALF_D // 128, 128)


def _gather_rows(idx_ref, tok, tab_ref, scr):
    nchunk = HALF_D // 128
    for k in range(PEER_K):
        e = idx_ref[tok * PEER_K + k]
        scr[k // 8, pl.ds(k % 8, nchunk, stride=8), :] = tab_ref[e]
    gm = jnp.concatenate(
        [scr[:, j * 8:(j + 1) * 8, :].reshape(PEER_K, 128) for j in range(nchunk)], axis=1)
    return pltpu.bitcast(gm, jnp.bfloat16)


def _peer_u_kernel(idx_ref, ut_ref, tab_ref, out_ref, scr0, scr1):
    shp = (8, 2 * PEER_K)
    row = lax.broadcasted_iota(jnp.int32, shp, 0)
    even = (lax.broadcasted_iota(jnp.int32, shp, 1) & 1) == 0

    def group(g, carry):
        base = pl.multiple_of(g * 8, 8)
        ut8 = ut_ref[pl.ds(base, 8), :]
        lhs = jnp.concatenate([ut8[:, :HALF_D], ut8[:, HALF_D:]], axis=0).astype(jnp.bfloat16)
        acc = jnp.zeros(shp, jnp.float32)
        for i in range(8):
            b = _gather_rows(idx_ref, base + i, tab_ref, (scr0, scr1)[i % 2])
            r = lax.dot_general(lhs, b, (((1,), (1,)), ((), ())),
                                preferred_element_type=jnp.float32)
            acc = jnp.where(row == i, jnp.where(even, r[0:8], r[8:16]), acc)
        out_ref[pl.ds(base, 8), :] = acc
        return carry

    lax.fori_loop(0, PEER_TB // 8, group, 0)


def _peer_v_kernel(idx_ref, a_ref, gate_ref, tab_ref, out_ref, scr0, scr1):
    shp = (8, 2 * PEER_K)
    row = lax.broadcasted_iota(jnp.int32, (8, HALF_D), 0)
    even = (lax.broadcasted_iota(jnp.int32, shp, 1) & 1) == 0

    def group(g, carry):
        base = pl.multiple_of(g * 8, 8)
        a8 = a_ref[pl.ds(base, 8), :]
        s = a8 + pltpu.roll(a8, 2 * PEER_K - 1, axis=1)
        act = jnp.where(even, s, pltpu.roll(s, 1, axis=1))
        gelu = 0.5 * act * (1.0 + lax.erf(act * (2.0 ** -0.5)))
        w = gelu * gate_ref[pl.ds(base, 8), :]
        lhs = jnp.concatenate([jnp.where(even, w, 0.0), jnp.where(even, 0.0, w)],
                              axis=0).astype(jnp.bfloat16)
        acc_lo = jnp.zeros((8, HALF_D), jnp.float32)
        acc_hi = jnp.zeros((8, HALF_D), jnp.float32)
        for i in range(8):
            b = _gather_rows(idx_ref, base + i, tab_ref, (scr0, scr1)[i % 2])
            r = jnp.dot(lhs, b, preferred_element_type=jnp.float32)
            acc_lo = jnp.where(row == i, r[0:8], acc_lo)
            acc_hi = jnp.where(row == i, r[8:16], acc_hi)
        out_ref[pl.ds(base, 8), 0:HALF_D] = acc_lo
        out_ref[pl.ds(base, 8), HALF_D:] = acc_hi
        return carry

    lax.fori_loop(0, PEER_TB // 8, group, 0)


def _peer_scratch():
    return [pltpu.VMEM((PEER_K // 8, 8 * (HALF_D // 128), 128), jnp.uint32) for _ in range(2)]


def _peer_params(tab):
    table_bytes = tab.size * 4
    return pltpu.CompilerParams(dimension_semantics=("arbitrary",),
                                vmem_limit_bytes=table_bytes + (16 << 20))


def _peer_u_pass(idx_flat, un, tab):
    t = un.shape[0]
    return pl.pallas_call(
        _peer_u_kernel,
        grid=(t // PEER_TB,),
        in_specs=[pl.BlockSpec((PEER_TB * PEER_K,), lambda i: (i,), memory_space=pltpu.SMEM),
                  pl.BlockSpec((PEER_TB, D_MODEL), lambda i: (i, 0)),
                  pl.BlockSpec(memory_space=pltpu.VMEM)],
        out_specs=pl.BlockSpec((PEER_TB, 2 * PEER_K), lambda i: (i, 0)),
        out_shape=jax.ShapeDtypeStruct((t, 2 * PEER_K), jnp.float32),
        scratch_shapes=_peer_scratch(),
        compiler_params=_peer_params(tab),
        name="peer_u_pass",
    )(idx_flat, un, tab)


def _peer_v_pass(idx_flat, a, gate2, tab):
    t = a.shape[0]
    return pl.pallas_call(
        _peer_v_kernel,
        grid=(t // PEER_TB,),
        in_specs=[pl.BlockSpec((PEER_TB * PEER_K,), lambda i: (i,), memory_space=pltpu.SMEM),
                  pl.BlockSpec((PEER_TB, 2 * PEER_K), lambda i: (i, 0)),
                  pl.BlockSpec((PEER_TB, 2 * PEER_K), lambda i: (i, 0)),
                  pl.BlockSpec(memory_space=pltpu.VMEM)],
        out_specs=pl.BlockSpec((PEER_TB, D_MODEL), lambda i: (i, 0)),
        out_shape=jax.ShapeDtypeStruct((t, D_MODEL), jnp.float32),
        scratch_shapes=_peer_scratch(),
        compiler_params=_peer_params(tab),
        name="peer_v_pass",
    )(idx_flat, a, gate2, tab)


def _peer_pallas(un, w_q, sub_keys, peer_u, peer_v):
    q = un @ w_q
    idx_t, gate_t = _peer_topk(q, sub_keys)
    idx_flat = idx_t.T.reshape(-1)
    gate2 = jnp.repeat(gate_t.T, 2, axis=1)
    a = _peer_u_pass(idx_flat, un, _pack_table(peer_u))
    return _peer_v_pass(idx_flat, a, gate2, _pack_table(peer_v))


def dev_stages():
    s = jax.ShapeDtypeStruct
    f32, i32, u32 = jnp.float32, jnp.int32, jnp.uint32
    t = 32768
    return {
        "topk": (_peer_topk, (s((t, 2048), f32), s((2, 128, 128), f32))),
        "upass": (_peer_u_pass, (s((t * 128,), i32), s((t, 1024), f32), s((16384, 4, 128), u32))),
        "vpass": (_peer_v_pass, (s((t * 128,), i32), s((t, 256), f32), s((t, 256), f32), s((16384, 4, 128), u32))),
    }


def _final_norm_kernel(h_ref, p_ref, w_ref, o_ref):
    hh = h_ref[...] + p_ref[...]
    ms = jnp.mean(hh * hh, axis=-1, keepdims=True)
    o_ref[...] = hh * lax.rsqrt(ms + EPS) * w_ref[...]


def _final_norm(h, p, w):
    t, d = h.shape
    tb = 512
    return pl.pallas_call(
        _final_norm_kernel,
        grid=(t // tb,),
        in_specs=[pl.BlockSpec((tb, d), lambda i: (i, 0)),
                  pl.BlockSpec((tb, d), lambda i: (i, 0)),
                  pl.BlockSpec((1, d), lambda i: (0, 0))],
        out_specs=pl.BlockSpec((tb, d), lambda i: (i, 0)),
        out_shape=jax.ShapeDtypeStruct((t, d), jnp.float32),
    )(h, p, w.reshape(1, d))


def kernel(x, positions, norm_mix_w, w_in, conv_w, conv_b, dt_bias_fwd, dt_bias_bwd, a_log_fwd, a_log_bwd, d_skip, ssd_norm_w, q_a_norm_w, w_q_b, kv_a_norm_w, w_kv_b, w_proj_ssd, w_proj_mla, w_out, norm_ffn_w, peer_w_q, peer_sub_keys, peer_u, peer_v, norm_final_w):
    b, s, _ = x.shape
    h = x
    u = _rms(h, norm_mix_w)
    proj = u @ w_in
    z, xbc, dt_raw, q_a, kv_a, k_rope, gate_logits = jnp.split(proj, _split_points(IN_SIZES), axis=-1)

    k = conv_w.shape[0]
    xbc = lax.conv_general_dilated(xbc, conv_w[:, None, :], window_strides=(1,),
                                   padding=[(k // 2, k // 2)],
                                   dimension_numbers=('NWC', 'WIO', 'NWC'),
                                   feature_group_count=xbc.shape[-1]) + conv_b
    xbc = jax.nn.silu(xbc)
    xs, Bm, Cm = jnp.split(xbc, [SSD_D_INNER, SSD_D_INNER + SSD_GROUPS * SSD_STATE], axis=-1)
    xh = xs.reshape(b, s, SSD_GROUPS, SSD_HPG, SSD_HEADDIM)
    Bm = Bm.reshape(b, s, SSD_GROUPS, SSD_STATE)
    Cm = Cm.reshape(b, s, SSD_GROUPS, SSD_STATE)
    dt_f_raw, dt_b_raw = jnp.split(dt_raw, 2, axis=-1)
    dt_f = jax.nn.softplus(dt_f_raw + dt_bias_fwd).reshape(b, s, SSD_GROUPS, SSD_HPG)
    dt_b = jax.nn.softplus(dt_b_raw + dt_bias_bwd).reshape(b, s, SSD_GROUPS, SSD_HPG)
    A_f = -jnp.exp(a_log_fwd).reshape(SSD_GROUPS, SSD_HPG)
    A_b = -jnp.exp(a_log_bwd).reshape(SSD_GROUPS, SSD_HPG)
    flip = lambda t: jnp.flip(t, axis=1)
    y_f = _ssd_scan(xh, dt_f, A_f, Bm, Cm)
    y_b = flip(_ssd_scan(flip(xh), flip(dt_b), A_b, flip(Bm), flip(Cm)))
    y = (y_f + y_b) + xh * d_skip.reshape(SSD_GROUPS, SSD_HPG, 1)
    yg = (y.reshape(b, s, SSD_D_INNER) * jax.nn.silu(z)).reshape(b, s, SSD_GROUPS, SSD_D_INNER // SSD_GROUPS)
    y_ssd = _rms(yg, ssd_norm_w.reshape(SSD_GROUPS, SSD_D_INNER // SSD_GROUPS)).reshape(b, s, SSD_D_INNER)

    q = (_rms(q_a, q_a_norm_w) @ w_q_b).reshape(b, s, MLA_HEADS, QK_NOPE + QK_ROPE)
    q_nope, q_rope = jnp.split(q, [QK_NOPE], axis=-1)
    kv = (_rms(kv_a, kv_a_norm_w) @ w_kv_b).reshape(b, s, MLA_HEADS, QK_NOPE + V_HEAD)
    k_nope, v = jnp.split(kv, [QK_NOPE], axis=-1)
    cos, sin = _rope_tables(positions, QK_ROPE)
    q_rope = _apply_rope(q_rope, cos[:, :, None, :], sin[:, :, None, :])
    k_rope = _apply_rope(k_rope, cos, sin)
    y_mla = _mla_attention(q_nope, q_rope, k_nope, k_rope, v)

    g_ssd, g_mla = jnp.split(jax.nn.sigmoid(gate_logits), 2, axis=-1)
    m = g_ssd * (y_ssd @ w_proj_ssd) + g_mla * (y_mla @ w_proj_mla)
    h = h + m @ w_out

    h2 = h.reshape(b * s, D_MODEL)
    pf = _peer_pallas(_rms(h2, norm_ffn_w), peer_w_q, peer_sub_keys, peer_u, peer_v)
    out = _final_norm(h2, pf, norm_final_w)
    return out.reshape(b, s, D_MODEL)
```

```python
import functools
import math

import jax
import jax.numpy as jnp
from jax import lax
from jax.experimental import pallas as pl
from jax.experimental.pallas import tpu as pltpu

D_MODEL = 1024
SSD_D_INNER = 1024
SSD_HEADDIM = 64
SSD_HEADS = 16
SSD_GROUPS = 2
SSD_HPG = 8
SSD_STATE = 128
CONV_WIDTH = 3
CHUNK = 128
SSD_CONV_DIM = SSD_D_INNER + 2 * SSD_GROUPS * SSD_STATE
MLA_HEADS = 16
QK_NOPE = 64
QK_ROPE = 32
V_HEAD = 64
Q_LORA = 384
KV_LORA = 256
Q_BLOCK = 128
ROPE_THETA = 10000.0
PEER_HEADS = 8
N_KEYS = 128
KEY_HALF = 128
PEER_TOPK = 16
TOK_BLOCK = 128
EPS = 1e-6
HEAD_PAD = 128
QK_DIM = QK_NOPE + QK_ROPE
LOG2E = 1.4426950408889634
IN_SIZES = (SSD_D_INNER, SSD_CONV_DIM, 2 * SSD_HEADS, Q_LORA, KV_LORA, QK_ROPE, 2 * D_MODEL)


def _split_points(sizes):
    pts, acc = [], 0
    for s in sizes[:-1]:
        acc += s
        pts.append(acc)
    return pts


def _rms(x, w):
    y = x * lax.rsqrt(jnp.mean(x * x, axis=-1, keepdims=True) + EPS)
    return y * w


def _segsum(a):
    t = a.shape[-1]
    cs = jnp.cumsum(a, axis=-1)
    diff = cs[..., :, None] - cs[..., None, :]
    mask = jnp.tril(jnp.ones((t, t), dtype=bool))
    return jnp.where(mask, diff, -jnp.inf)


def _ssd_scan(xh, dt, A, Bm, Cm):
    b, l, g, r, p = xh.shape
    n = Bm.shape[-1]
    c = l // CHUNK
    xd = (xh * dt[..., None]).reshape(b, c, CHUNK, g, r, p)
    a = (dt * A).reshape(b, c, CHUNK, g, r)
    a = jnp.moveaxis(a, 2, -1)
    Bc = Bm.reshape(b, c, CHUNK, g, n)
    Cc = Cm.reshape(b, c, CHUNK, g, n)
    a_cs = jnp.cumsum(a, axis=-1)
    L = jnp.exp(_segsum(a))
    CB = jnp.einsum('bclgn,bcsgn->bcgls', Cc, Bc)
    y_diag = jnp.einsum('bcgls,bcgrls,bcsgrp->bclgrp', CB, L, xd)
    decay_states = jnp.exp(a_cs[..., -1:] - a_cs)
    states = jnp.einsum('bcsgn,bcgrs,bcsgrp->bcgrpn', Bc, decay_states, xd)
    chunk_tot = jnp.moveaxis(a_cs[..., -1], 1, -1)
    decay_chunk = jnp.exp(_segsum(jnp.pad(chunk_tot, ((0, 0), (0, 0), (0, 0), (1, 0)))))
    states0 = jnp.pad(states, ((0, 0), (1, 0), (0, 0), (0, 0), (0, 0), (0, 0)))
    new_states = jnp.einsum('bgrzc,bcgrpn->bzgrpn', decay_chunk, states0)
    prev = new_states[:, :-1]
    y_off = jnp.einsum('bclgn,bcgrpn,bcgrl->bclgrp', Cc, prev, jnp.exp(a_cs))
    return (y_diag + y_off).reshape(b, l, g, r, p)


def _rope_tables(positions, dim):
    inv = 1.0 / (ROPE_THETA ** (jnp.arange(0, dim, 2, dtype=jnp.float32) / dim))
    ang = positions.astype(jnp.float32)[..., None] * inv
    return jnp.cos(ang), jnp.sin(ang)


def _apply_rope(x, cos, sin):
    x1, x2 = jnp.split(x, 2, axis=-1)
    return jnp.concatenate([x1 * cos - x2 * sin, x1 * sin + x2 * cos], axis=-1)


def _mla_attention(q_nope, q_rope, k_nope, k_rope, v):
    b, s, h, _ = q_nope.shape
    nb = s // Q_BLOCK
    scale = (QK_NOPE + QK_ROPE) ** -0.5

    def block(args):
        qn, qr = args
        sc = (jnp.einsum('bqhd,bkhd->bhqk', qn, k_nope)
              + jnp.einsum('bqhd,bkd->bhqk', qr, k_rope))
        p = jax.nn.softmax(sc * scale, axis=-1)
        return jnp.einsum('bhqk,bkhd->bqhd', p, v)

    qn_b = q_nope.reshape(b, nb, Q_BLOCK, h, QK_NOPE).swapaxes(0, 1)
    qr_b = q_rope.reshape(b, nb, Q_BLOCK, h, QK_ROPE).swapaxes(0, 1)
    out = lax.map(block, (qn_b, qr_b))
    return out.swapaxes(0, 1).reshape(b, s, h * V_HEAD)


def _peer_ffn(u, w_q, sub_keys, peer_u, peer_v):
    b, s, d = u.shape
    n_exp = peer_u.shape[0]
    ub = u.reshape((b * s) // TOK_BLOCK, TOK_BLOCK, d)

    def block(ut):
        q = (ut @ w_q).reshape(TOK_BLOCK, PEER_HEADS, 2, KEY_HALF)
        sc = jnp.einsum('thid,ind->thin', q, sub_keys)
        v_top, i_top = lax.top_k(sc, PEER_TOPK)
        cand = v_top[:, :, 0, :, None] + v_top[:, :, 1, None, :]
        cand_idx = i_top[:, :, 0, :, None] * N_KEYS + i_top[:, :, 1, None, :]
        cand = cand.reshape(TOK_BLOCK, PEER_HEADS, PEER_TOPK * PEER_TOPK)
        cand_idx = cand_idx.reshape(TOK_BLOCK, PEER_HEADS, PEER_TOPK * PEER_TOPK)
        best, pos = lax.top_k(cand, PEER_TOPK)
        idx = jnp.take_along_axis(cand_idx, pos, axis=-1)
        gate = jax.nn.softmax(best, axis=-1)
        act = jax.nn.gelu(jnp.einsum('td,thkd->thk', ut, peer_u[idx]), approximate=False)
        return jnp.einsum('thk,thkd->td', gate * act, peer_v[idx])

    return lax.map(block, ub).reshape(b, s, d)


IN_TB = 256
DT_PAD = 128
GATE_W = 2 * D_MODEL
IN_OUT_WIDTHS = (SSD_D_INNER, SSD_CONV_DIM, DT_PAD, Q_LORA, KV_LORA, HEAD_PAD, HEAD_PAD, GATE_W)


def _in_proj_kernel(x_ref, nw_ref, w_ref, *out_refs):
    u = _rms_rows(x_ref[...], nw_ref[...]).astype(jnp.bfloat16)
    col = 0
    for o_ref, width in zip(out_refs, IN_OUT_WIDTHS):
        o_ref[...] = jnp.dot(u, w_ref[:, col:col + width], preferred_element_type=jnp.float32)
        col += width


def _arrange_w_in(w_in):
    z, xbc, dt, qa, kva, kr, gates = jnp.split(w_in, _split_points(IN_SIZES), axis=1)
    d = w_in.shape[0]
    half = QK_ROPE // 2
    dt_p = jnp.pad(dt, ((0, 0), (0, DT_PAD - dt.shape[1])))
    kr3 = kr.reshape(d, 1, QK_ROPE)
    kr_p = _pad_heads(None, kr3)
    krs_p = _pad_heads(None, _rope_partner(kr3[..., :half], kr3[..., half:]))
    return jnp.concatenate([z, xbc, dt_p, qa, kva, kr_p, krs_p, gates], axis=1).astype(jnp.bfloat16)


def _in_proj(x2, norm_w, w_in):
    t, d = x2.shape
    w = _arrange_w_in(w_in)
    tb = min(IN_TB, t)
    return pl.pallas_call(
        _in_proj_kernel,
        grid=(t // tb,),
        in_specs=[pl.BlockSpec((tb, d), lambda i: (i, 0)),
                  pl.BlockSpec((1, d), lambda i: (0, 0)),
                  pl.BlockSpec(w.shape, lambda i: (0, 0))],
        out_specs=[pl.BlockSpec((tb, n), lambda i: (i, 0)) for n in IN_OUT_WIDTHS],
        out_shape=[jax.ShapeDtypeStruct((t, n), jnp.float32) for n in IN_OUT_WIDTHS],
        compiler_params=pltpu.CompilerParams(dimension_semantics=("arbitrary",),
                                             vmem_limit_bytes=56 << 20),
        name="in_proj",
    )(x2, norm_w.reshape(1, d), w)


MERGE_TB = 256


def _merge_kernel(x_ref, ys_ref, ym_ref, g_ref, nw_ref, pa_ref, pb_ref, wo_ref, wq_ref,
                  h_ref, un_ref, q_ref):
    bf16 = jnp.bfloat16
    f32 = jnp.float32
    g = jax.nn.sigmoid(g_ref[...])
    ma = jnp.dot(ys_ref[...].astype(bf16), pa_ref[...], preferred_element_type=f32)
    mb = jnp.dot(ym_ref[...].astype(bf16), pb_ref[...], preferred_element_type=f32)
    m = g[:, :D_MODEL] * ma + g[:, D_MODEL:] * mb
    h = x_ref[...] + jnp.dot(m.astype(bf16), wo_ref[...], preferred_element_type=f32)
    h_ref[...] = h
    un = _rms_rows(h, nw_ref[...])
    un_ref[...] = un
    q_ref[...] = jnp.dot(un.astype(bf16), wq_ref[...], preferred_element_type=f32)


def _merge(x2, y_ssd, y_mla, gates, norm_ffn_w, w_proj_ssd, w_proj_mla, w_out, peer_w_q):
    t, d = x2.shape
    bf16 = jnp.bfloat16
    tb = min(MERGE_TB, t)
    nq = peer_w_q.shape[1]
    row = lambda n: pl.BlockSpec((tb, n), lambda i: (i, 0))
    full = lambda a: pl.BlockSpec(a.shape, lambda i: (0, 0))
    ws = [w.astype(bf16) for w in (w_proj_ssd, w_proj_mla, w_out, peer_w_q)]
    nw = norm_ffn_w.reshape(1, d)
    return pl.pallas_call(
        _merge_kernel,
        grid=(t // tb,),
        in_specs=[row(d), row(d), row(d), row(GATE_W), full(nw)] + [full(w) for w in ws],
        out_specs=[row(d), row(d), row(nq)],
        out_shape=[jax.ShapeDtypeStruct((t, d), jnp.float32), jax.ShapeDtypeStruct((t, d), jnp.float32),
                   jax.ShapeDtypeStruct((t, nq), jnp.float32)],
        compiler_params=pltpu.CompilerParams(dimension_semantics=("arbitrary",),
                                             vmem_limit_bytes=56 << 20),
        name="merge",
    )(x2, y_ssd, y_mla, gates, nw, *ws)


MLA_TB = 512
ATT_TQ = 512


def _rms_rows(x, w):
    return x * lax.rsqrt(jnp.mean(x * x, axis=-1, keepdims=True) + EPS) * w


def _mla_proj_kernel(pos_ref, qa_ref, kva_ref, kr_ref, krs_ref, qnw_ref, kvnw_ref, inv_ref,
                     wq_ref, wqs_ref, wk_ref, wv_ref, q_out, k_out, v_out):
    bf16 = jnp.bfloat16
    ang = pos_ref[...].astype(jnp.float32) * inv_ref[...]
    cos, sin = jnp.cos(ang), jnp.sin(ang)
    qa = _rms_rows(qa_ref[...], qnw_ref[...]).astype(bf16)
    q = jnp.dot(qa, wq_ref[...], preferred_element_type=jnp.float32)
    qs = jnp.dot(qa, wqs_ref[...], preferred_element_type=jnp.float32)
    kva = _rms_rows(kva_ref[...], kvnw_ref[...]).astype(bf16)
    k = jnp.dot(kva, wk_ref[...], preferred_element_type=jnp.float32)
    v_out[...] = jnp.dot(kva, wv_ref[...], preferred_element_type=jnp.float32).astype(bf16)
    kro = kr_ref[...] * cos + krs_ref[...] * sin
    c = (QK_DIM ** -0.5) * LOG2E
    for h in range(MLA_HEADS):
        sl = slice(h * HEAD_PAD, (h + 1) * HEAD_PAD)
        q_out[:, sl] = ((q[:, sl] * cos + qs[:, sl] * sin) * c).astype(bf16)
        k_out[:, sl] = (k[:, sl] + kro).astype(bf16)


def _rope_partner(w1, w2):
    return jnp.concatenate([-w2, w1], axis=-1)


def _pad_heads(nope, rope):
    ref = nope if nope is not None else rope
    lead = ref.shape[:-1]
    z = lambda n: jnp.zeros(lead + (n,), ref.dtype)
    parts = [nope if nope is not None else z(QK_NOPE), rope if rope is not None else z(QK_ROPE),
             z(HEAD_PAD - QK_DIM)]
    out = jnp.concatenate(parts, axis=-1)
    return out.reshape(out.shape[:-2] + (out.shape[-2] * HEAD_PAD,))


def _mla_proj(positions, q_a, kv_a, kr, krs, q_a_norm_w, kv_a_norm_w, w_q_b, w_kv_b):
    t = q_a.shape[0]
    bf16 = jnp.bfloat16
    half = QK_ROPE // 2
    wq = w_q_b.reshape(Q_LORA, MLA_HEADS, QK_DIM)
    wq_n, wq_r = wq[..., :QK_NOPE], wq[..., QK_NOPE:]
    wq_main = _pad_heads(wq_n, wq_r).astype(bf16)
    wq_part = _pad_heads(None, _rope_partner(wq_r[..., :half], wq_r[..., half:])).astype(bf16)
    wkv = w_kv_b.reshape(KV_LORA, MLA_HEADS, QK_NOPE + V_HEAD)
    wk = _pad_heads(wkv[..., :QK_NOPE], None).astype(bf16)
    wv = wkv[..., QK_NOPE:].reshape(KV_LORA, MLA_HEADS * V_HEAD).astype(bf16)
    inv = 1.0 / (ROPE_THETA ** (jnp.arange(0, QK_ROPE, 2, dtype=jnp.float32) / QK_ROPE))
    inv_lane = jnp.concatenate([jnp.zeros((QK_NOPE,), jnp.float32), inv, inv,
                                jnp.zeros((HEAD_PAD - QK_DIM,), jnp.float32)]).reshape(1, HEAD_PAD)
    tb = min(MLA_TB, t)
    row = lambda n: pl.BlockSpec((tb, n), lambda i: (i, 0))
    full = lambda a: pl.BlockSpec(a.shape, lambda i: (0,) * a.ndim)
    qn = q_a_norm_w.reshape(1, Q_LORA)
    kvn = kv_a_norm_w.reshape(1, KV_LORA)
    args = (positions.reshape(t, 1), q_a, kv_a, kr, krs, qn, kvn, inv_lane, wq_main, wq_part, wk, wv)
    in_specs = [row(1), row(Q_LORA), row(KV_LORA), row(HEAD_PAD), row(HEAD_PAD)] + [full(a) for a in args[5:]]
    nq = MLA_HEADS * HEAD_PAD
    return pl.pallas_call(
        _mla_proj_kernel,
        grid=(t // tb,),
        in_specs=in_specs,
        out_specs=[row(nq), row(nq), row(MLA_HEADS * V_HEAD)],
        out_shape=[jax.ShapeDtypeStruct((t, nq), bf16), jax.ShapeDtypeStruct((t, nq), bf16),
                   jax.ShapeDtypeStruct((t, MLA_HEADS * V_HEAD), bf16)],
        compiler_params=pltpu.CompilerParams(dimension_semantics=("arbitrary",),
                                             vmem_limit_bytes=48 << 20),
        name="mla_proj",
    )(*args)


def _attn_kernel(q_ref, k_ref, v_ref, o_ref):
    tq = q_ref.shape[1]
    v = v_ref[0]
    outs = []
    for hh in range(2):
        sl = slice(hh * HEAD_PAD, (hh + 1) * HEAD_PAD)
        s = lax.dot_general(q_ref[0, :, sl], k_ref[0, :, sl], (((1,), (1,)), ((), ())),
                            preferred_element_type=jnp.float32)
        m = jnp.max(s, axis=-1, keepdims=True)
        p = jnp.exp2(s - m)
        l = jnp.sum(p, axis=-1, keepdims=True)
        pv = jnp.dot(p.astype(jnp.bfloat16), v, preferred_element_type=jnp.float32)
        outs.append(pv / l)
    lane = lax.broadcasted_iota(jnp.int32, (tq, 2 * V_HEAD), 1)
    o_ref[0] = jnp.where(lane < V_HEAD, outs[0], outs[1]).astype(o_ref.dtype)


def _attention(q, k, v):
    b, s, _ = q.shape
    tq = min(ATT_TQ, s)
    return pl.pallas_call(
        _attn_kernel,
        grid=(b, MLA_HEADS // 2, s // tq),
        in_specs=[pl.BlockSpec((1, tq, 2 * HEAD_PAD), lambda bi, j, i: (bi, i, j)),
                  pl.BlockSpec((1, s, 2 * HEAD_PAD), lambda bi, j, i: (bi, 0, j)),
                  pl.BlockSpec((1, s, 2 * V_HEAD), lambda bi, j, i: (bi, 0, j))],
        out_specs=pl.BlockSpec((1, tq, 2 * V_HEAD), lambda bi, j, i: (bi, i, j)),
        out_shape=jax.ShapeDtypeStruct((b, s, MLA_HEADS * V_HEAD), jnp.bfloat16),
        compiler_params=pltpu.CompilerParams(dimension_semantics=("arbitrary",) * 3,
                                             vmem_limit_bytes=56 << 20),
        name="mla_attention",
    )(q, k, v)


def _mla_pallas(positions, q_a, kv_a, kr, krs, q_a_norm_w, kv_a_norm_w, w_q_b, w_kv_b):
    b, s = positions.shape
    q, k, v = _mla_proj(positions, q_a, kv_a, kr, krs, q_a_norm_w, kv_a_norm_w, w_q_b, w_kv_b)
    y = _attention(q.reshape(b, s, -1), k.reshape(b, s, -1), v.reshape(b, s, -1))
    return y.reshape(b * s, -1)


PEER_TB = 128
PEER_K = PEER_HEADS * PEER_TOPK
HALF_D = D_MODEL // 2
NEG_INF = float("-inf")


def _top16_rows(x, extra=None):
    rows = x.shape[0]
    iota = lax.broadcasted_iota(jnp.int32, x.shape, 0)
    vals, poss, exts = [], [], []
    for _ in range(PEER_TOPK):
        m = jnp.max(x, axis=0, keepdims=True)
        p = jnp.min(jnp.where(x == m, iota, rows), axis=0, keepdims=True)
        hit = iota == p
        if extra is not None:
            exts.append(jnp.sum(jnp.where(hit, extra, 0), axis=0, keepdims=True))
        x = jnp.where(hit, NEG_INF, x)
        vals.append(m)
        poss.append(p)
    vals = jnp.concatenate(vals, axis=0)
    poss = jnp.concatenate(poss, axis=0)
    if extra is None:
        return vals, poss
    return vals, poss, jnp.concatenate(exts, axis=0)


def _peer_topk_kernel(q_ref, keys_ref, idx_ref, gate_ref):
    for h in range(PEER_HEADS):
        v, ids = [], []
        for i in range(2):
            c0 = (h * 2 + i) * KEY_HALF
            qhi = q_ref[:, c0:c0 + KEY_HALF]
            sc = lax.dot_general(keys_ref[i], qhi, (((1,), (1,)), ((), ())),
                                 preferred_element_type=jnp.float32)
            vi, pi = _top16_rows(sc)
            v.append(vi)
            ids.append(pi)
        cand = jnp.concatenate([v[0][a:a + 1, :] + v[1] for a in range(PEER_TOPK)], axis=0)
        cidx = jnp.concatenate([ids[0][a:a + 1, :] * N_KEYS + ids[1] for a in range(PEER_TOPK)], axis=0)
        best, _, eidx = _top16_rows(cand, cidx)
        e = jnp.exp(best - best[0:1, :])
        gate = e / jnp.sum(e, axis=0, keepdims=True)
        idx_ref[h * PEER_TOPK:(h + 1) * PEER_TOPK, :] = eidx
        gate_ref[h * PEER_TOPK:(h + 1) * PEER_TOPK, :] = gate


def _peer_topk(q, sub_keys):
    t = q.shape[0]
    return pl.pallas_call(
        _peer_topk_kernel,
        grid=(t // PEER_TB,),
        in_specs=[pl.BlockSpec((PEER_TB, q.shape[1]), lambda i: (i, 0)),
                  pl.BlockSpec(sub_keys.shape, lambda i: (0, 0, 0))],
        out_specs=[pl.BlockSpec((PEER_K, PEER_TB), lambda i: (0, i)),
                   pl.BlockSpec((PEER_K, PEER_TB), lambda i: (0, i))],
        out_shape=[jax.ShapeDtypeStruct((PEER_K, t), jnp.int32),
                   jax.ShapeDtypeStruct((PEER_K, t), jnp.float32)],
        name="peer_topk",
    )(q, sub_keys)


def _pack_table(tab):
    e = tab.shape[0]
    tb = tab.astype(jnp.bfloat16)
    lo = lax.bitcast_convert_type(tb[:, :HALF_D], jnp.uint16).astype(jnp.uint32)
    hi = lax.bitcast_convert_type(tb[:, HALF_D:], jnp.uint16).astype(jnp.uint32)
    return (lo | (hi << 16)).reshape(e, HALF_D // 128, 128)


def _gather_rows(idx_ref, tok, tab_ref, scr):
    nchunk = HALF_D // 128
    for k in range(PEER_K):
        e = idx_ref[tok * PEER_K + k]
        scr[k // 8, pl.ds(k % 8, nchunk, stride=8), :] = tab_ref[e]
    gm = jnp.concatenate(
        [scr[:, j * 8:(j + 1) * 8, :].reshape(PEER_K, 128) for j in range(nchunk)], axis=1)
    return pltpu.bitcast(gm, jnp.bfloat16)


def _peer_u_kernel(idx_ref, ut_ref, tab_ref, out_ref, scr0, scr1):
    shp = (8, 2 * PEER_K)
    row = lax.broadcasted_iota(jnp.int32, shp, 0)
    even = (lax.broadcasted_iota(jnp.int32, shp, 1) & 1) == 0

    def group(g, carry):
        base = pl.multiple_of(g * 8, 8)
        ut8 = ut_ref[pl.ds(base, 8), :]
        lhs = jnp.concatenate([ut8[:, :HALF_D], ut8[:, HALF_D:]], axis=0).astype(jnp.bfloat16)
        acc = jnp.zeros(shp, jnp.float32)
        for i in range(8):
            b = _gather_rows(idx_ref, base + i, tab_ref, (scr0, scr1)[i % 2])
            r = lax.dot_general(lhs, b, (((1,), (1,)), ((), ())),
                                preferred_element_type=jnp.float32)
            acc = jnp.where(row == i, jnp.where(even, r[0:8], r[8:16]), acc)
        out_ref[pl.ds(base, 8), :] = acc
        return carry

    lax.fori_loop(0, PEER_TB // 8, group, 0)


def _peer_v_kernel(idx_ref, a_ref, gate_ref, tab_ref, out_ref, scr0, scr1):
    shp = (8, 2 * PEER_K)
    row = lax.broadcasted_iota(jnp.int32, (8, HALF_D), 0)
    even = (lax.broadcasted_iota(jnp.int32, shp, 1) & 1) == 0

    def group(g, carry):
        base = pl.multiple_of(g * 8, 8)
        a8 = a_ref[pl.ds(base, 8), :]
        s = a8 + pltpu.roll(a8, 2 * PEER_K - 1, axis=1)
        act = jnp.where(even, s, pltpu.roll(s, 1, axis=1))
        gelu = 0.5 * act * (1.0 + lax.erf(act * (2.0 ** -0.5)))
        w = gelu * gate_ref[pl.ds(base, 8), :]
        lhs = jnp.concatenate([jnp.where(even, w, 0.0), jnp.where(even, 0.0, w)],
                              axis=0).astype(jnp.bfloat16)
        acc_lo = jnp.zeros((8, HALF_D), jnp.float32)
        acc_hi = jnp.zeros((8, HALF_D), jnp.float32)
        for i in range(8):
            b = _gather_rows(idx_ref, base + i, tab_ref, (scr0, scr1)[i % 2])
            r = jnp.dot(lhs, b, preferred_element_type=jnp.float32)
            acc_lo = jnp.where(row == i, r[0:8], acc_lo)
            acc_hi = jnp.where(row == i, r[8:16], acc_hi)
        out_ref[pl.ds(base, 8), 0:HALF_D] = acc_lo
        out_ref[pl.ds(base, 8), HALF_D:] = acc_hi
        return carry

    lax.fori_loop(0, PEER_TB // 8, group, 0)


def _peer_scratch():
    return [pltpu.VMEM((PEER_K // 8, 8 * (HALF_D // 128), 128), jnp.uint32) for _ in range(2)]


def _peer_params(tab):
    table_bytes = tab.size * 4
    return pltpu.CompilerParams(dimension_semantics=("arbitrary",),
                                vmem_limit_bytes=table_bytes + (16 << 20))


def _peer_u_pass(idx_flat, un, tab):
    t = un.shape[0]
    return pl.pallas_call(
        _peer_u_kernel,
        grid=(t // PEER_TB,),
        in_specs=[pl.BlockSpec((PEER_TB * PEER_K,), lambda i: (i,), memory_space=pltpu.SMEM),
                  pl.BlockSpec((PEER_TB, D_MODEL), lambda i: (i, 0)),
                  pl.BlockSpec(memory_space=pltpu.VMEM)],
        out_specs=pl.BlockSpec((PEER_TB, 2 * PEER_K), lambda i: (i, 0)),
        out_shape=jax.ShapeDtypeStruct((t, 2 * PEER_K), jnp.float32),
        scratch_shapes=_peer_scratch(),
        compiler_params=_peer_params(tab),
        name="peer_u_pass",
    )(idx_flat, un, tab)


def _peer_v_pass(idx_flat, a, gate2, tab):
    t = a.shape[0]
    return pl.pallas_call(
        _peer_v_kernel,
        grid=(t // PEER_TB,),
        in_specs=[pl.BlockSpec((PEER_TB * PEER_K,), lambda i: (i,), memory_space=pltpu.SMEM),
                  pl.BlockSpec((PEER_TB, 2 * PEER_K), lambda i: (i, 0)),
                  pl.BlockSpec((PEER_TB, 2 * PEER_K), lambda i: (i, 0)),
                  pl.BlockSpec(memory_space=pltpu.VMEM)],
        out_specs=pl.BlockSpec((PEER_TB, D_MODEL), lambda i: (i, 0)),
        out_shape=jax.ShapeDtypeStruct((t, D_MODEL), jnp.float32),
        scratch_shapes=_peer_scratch(),
        compiler_params=_peer_params(tab),
        name="peer_v_pass",
    )(idx_flat, a, gate2, tab)


def _peer_pallas(un, q, sub_keys, peer_u, peer_v):
    idx_t, gate_t = _peer_topk(q, sub_keys)
    idx_flat = idx_t.T.reshape(-1)
    gate2 = jnp.repeat(gate_t.T, 2, axis=1)
    a = _peer_u_pass(idx_flat, un, _pack_table(peer_u))
    return _peer_v_pass(idx_flat, a, gate2, _pack_table(peer_v))


def dev_stages():
    s = jax.ShapeDtypeStruct
    f32, i32, u32 = jnp.float32, jnp.int32, jnp.uint32
    t = 32768
    bf16 = jnp.bfloat16
    return {
        "mlaproj": (_mla_proj, (s((8, 4096), i32), s((t, 384), f32), s((t, 256), f32), s((t, 128), f32),
                                s((t, 128), f32), s((384,), f32), s((256,), f32), s((384, 1536), f32),
                                s((256, 2048), f32))),
        "inproj": (_in_proj, (s((t, 1024), f32), s((1024,), f32), s((1024, 5312), f32))),
        "merge": (_merge, (s((t, 1024), f32), s((t, 1024), f32), s((t, 1024), bf16), s((t, 2048), f32),
                           s((1024,), f32), s((1024, 1024), f32), s((1024, 1024), f32),
                           s((1024, 1024), f32), s((1024, 2048), f32))),
        "attn": (_attention, (s((8, 4096, 2048), bf16), s((8, 4096, 2048), bf16), s((8, 4096, 1024), bf16))),
        "topk": (_peer_topk, (s((t, 2048), f32), s((2, 128, 128), f32))),
        "upass": (_peer_u_pass, (s((t * 128,), i32), s((t, 1024), f32), s((16384, 4, 128), u32))),
        "vpass": (_peer_v_pass, (s((t * 128,), i32), s((t, 256), f32), s((t, 256), f32), s((16384, 4, 128), u32))),
    }


def _final_norm_kernel(h_ref, p_ref, w_ref, o_ref):
    hh = h_ref[...] + p_ref[...]
    ms = jnp.mean(hh * hh, axis=-1, keepdims=True)
    o_ref[...] = hh * lax.rsqrt(ms + EPS) * w_ref[...]


def _final_norm(h, p, w):
    t, d = h.shape
    tb = 512
    return pl.pallas_call(
        _final_norm_kernel,
        grid=(t // tb,),
        in_specs=[pl.BlockSpec((tb, d), lambda i: (i, 0)),
                  pl.BlockSpec((tb, d), lambda i: (i, 0)),
                  pl.BlockSpec((1, d), lambda i: (0, 0))],
        out_specs=pl.BlockSpec((tb, d), lambda i: (i, 0)),
        out_shape=jax.ShapeDtypeStruct((t, d), jnp.float32),
    )(h, p, w.reshape(1, d))


def kernel(x, positions, norm_mix_w, w_in, conv_w, conv_b, dt_bias_fwd, dt_bias_bwd, a_log_fwd, a_log_bwd, d_skip, ssd_norm_w, q_a_norm_w, w_q_b, kv_a_norm_w, w_kv_b, w_proj_ssd, w_proj_mla, w_out, norm_ffn_w, peer_w_q, peer_sub_keys, peer_u, peer_v, norm_final_w):
    b, s, _ = x.shape
    x2 = x.reshape(b * s, D_MODEL)
    z, xbc, dt128, q_a, kv_a, kr, krs, gate_logits = _in_proj(x2, norm_mix_w, w_in)
    z = z.reshape(b, s, -1)
    xbc = xbc.reshape(b, s, -1)
    dt_raw = dt128[:, :2 * SSD_HEADS].reshape(b, s, -1)

    k = conv_w.shape[0]
    xbc = lax.conv_general_dilated(xbc, conv_w[:, None, :], window_strides=(1,),
                                   padding=[(k // 2, k // 2)],
                                   dimension_numbers=('NWC', 'WIO', 'NWC'),
                                   feature_group_count=xbc.shape[-1]) + conv_b
    xbc = jax.nn.silu(xbc)
    xs, Bm, Cm = jnp.split(xbc, [SSD_D_INNER, SSD_D_INNER + SSD_GROUPS * SSD_STATE], axis=-1)
    xh = xs.reshape(b, s, SSD_GROUPS, SSD_HPG, SSD_HEADDIM)
    Bm = Bm.reshape(b, s, SSD_GROUPS, SSD_STATE)
    Cm = Cm.reshape(b, s, SSD_GROUPS, SSD_STATE)
    dt_f_raw, dt_b_raw = jnp.split(dt_raw, 2, axis=-1)
    dt_f = jax.nn.softplus(dt_f_raw + dt_bias_fwd).reshape(b, s, SSD_GROUPS, SSD_HPG)
    dt_b = jax.nn.softplus(dt_b_raw + dt_bias_bwd).reshape(b, s, SSD_GROUPS, SSD_HPG)
    A_f = -jnp.exp(a_log_fwd).reshape(SSD_GROUPS, SSD_HPG)
    A_b = -jnp.exp(a_log_bwd).reshape(SSD_GROUPS, SSD_HPG)
    flip = lambda t: jnp.flip(t, axis=1)
    y_f = _ssd_scan(xh, dt_f, A_f, Bm, Cm)
    y_b = flip(_ssd_scan(flip(xh), flip(dt_b), A_b, flip(Bm), flip(Cm)))
    y = (y_f + y_b) + xh * d_skip.reshape(SSD_GROUPS, SSD_HPG, 1)
    yg = (y.reshape(b, s, SSD_D_INNER) * jax.nn.silu(z)).reshape(b, s, SSD_GROUPS, SSD_D_INNER // SSD_GROUPS)
    y_ssd = _rms(yg, ssd_norm_w.reshape(SSD_GROUPS, SSD_D_INNER // SSD_GROUPS)).reshape(b, s, SSD_D_INNER)

    y_mla = _mla_pallas(positions, q_a, kv_a, kr, krs, q_a_norm_w, kv_a_norm_w, w_q_b, w_kv_b)

    h2, un, q_peer = _merge(x2, y_ssd.reshape(b * s, -1), y_mla, gate_logits, norm_ffn_w,
                            w_proj_ssd, w_proj_mla, w_out, peer_w_q)
    pf = _peer_pallas(un, q_peer, peer_sub_keys, peer_u, peer_v)
    out = _final_norm(h2, pf, norm_final_w)
    return out.reshape(b, s, D_MODEL)
```

```python
import functools
import math

import jax
import jax.numpy as jnp
from jax import lax
from jax.experimental import pallas as pl
from jax.experimental.pallas import tpu as pltpu

D_MODEL = 1024
SSD_D_INNER = 1024
SSD_HEADDIM = 64
SSD_HEADS = 16
SSD_GROUPS = 2
SSD_HPG = 8
SSD_STATE = 128
CONV_WIDTH = 3
CHUNK = 128
SSD_CONV_DIM = SSD_D_INNER + 2 * SSD_GROUPS * SSD_STATE
MLA_HEADS = 16
QK_NOPE = 64
QK_ROPE = 32
V_HEAD = 64
Q_LORA = 384
KV_LORA = 256
Q_BLOCK = 128
ROPE_THETA = 10000.0
PEER_HEADS = 8
N_KEYS = 128
KEY_HALF = 128
PEER_TOPK = 16
TOK_BLOCK = 128
EPS = 1e-6
HEAD_PAD = 128
QK_DIM = QK_NOPE + QK_ROPE
LOG2E = 1.4426950408889634
IN_SIZES = (SSD_D_INNER, SSD_CONV_DIM, 2 * SSD_HEADS, Q_LORA, KV_LORA, QK_ROPE, 2 * D_MODEL)


def _split_points(sizes):
    pts, acc = [], 0
    for s in sizes[:-1]:
        acc += s
        pts.append(acc)
    return pts


def _rms(x, w):
    y = x * lax.rsqrt(jnp.mean(x * x, axis=-1, keepdims=True) + EPS)
    return y * w


def _segsum(a):
    t = a.shape[-1]
    cs = jnp.cumsum(a, axis=-1)
    diff = cs[..., :, None] - cs[..., None, :]
    mask = jnp.tril(jnp.ones((t, t), dtype=bool))
    return jnp.where(mask, diff, -jnp.inf)


def _ssd_scan(xh, dt, A, Bm, Cm):
    b, l, g, r, p = xh.shape
    n = Bm.shape[-1]
    c = l // CHUNK
    xd = (xh * dt[..., None]).reshape(b, c, CHUNK, g, r, p)
    a = (dt * A).reshape(b, c, CHUNK, g, r)
    a = jnp.moveaxis(a, 2, -1)
    Bc = Bm.reshape(b, c, CHUNK, g, n)
    Cc = Cm.reshape(b, c, CHUNK, g, n)
    a_cs = jnp.cumsum(a, axis=-1)
    L = jnp.exp(_segsum(a))
    CB = jnp.einsum('bclgn,bcsgn->bcgls', Cc, Bc)
    y_diag = jnp.einsum('bcgls,bcgrls,bcsgrp->bclgrp', CB, L, xd)
    decay_states = jnp.exp(a_cs[..., -1:] - a_cs)
    states = jnp.einsum('bcsgn,bcgrs,bcsgrp->bcgrpn', Bc, decay_states, xd)
    chunk_tot = jnp.moveaxis(a_cs[..., -1], 1, -1)
    decay_chunk = jnp.exp(_segsum(jnp.pad(chunk_tot, ((0, 0), (0, 0), (0, 0), (1, 0)))))
    states0 = jnp.pad(states, ((0, 0), (1, 0), (0, 0), (0, 0), (0, 0), (0, 0)))
    new_states = jnp.einsum('bgrzc,bcgrpn->bzgrpn', decay_chunk, states0)
    prev = new_states[:, :-1]
    y_off = jnp.einsum('bclgn,bcgrpn,bcgrl->bclgrp', Cc, prev, jnp.exp(a_cs))
    return (y_diag + y_off).reshape(b, l, g, r, p)


def _rope_tables(positions, dim):
    inv = 1.0 / (ROPE_THETA ** (jnp.arange(0, dim, 2, dtype=jnp.float32) / dim))
    ang = positions.astype(jnp.float32)[..., None] * inv
    return jnp.cos(ang), jnp.sin(ang)


def _apply_rope(x, cos, sin):
    x1, x2 = jnp.split(x, 2, axis=-1)
    return jnp.concatenate([x1 * cos - x2 * sin, x1 * sin + x2 * cos], axis=-1)


def _mla_attention(q_nope, q_rope, k_nope, k_rope, v):
    b, s, h, _ = q_nope.shape
    nb = s // Q_BLOCK
    scale = (QK_NOPE + QK_ROPE) ** -0.5

    def block(args):
        qn, qr = args
        sc = (jnp.einsum('bqhd,bkhd->bhqk', qn, k_nope)
              + jnp.einsum('bqhd,bkd->bhqk', qr, k_rope))
        p = jax.nn.softmax(sc * scale, axis=-1)
        return jnp.einsum('bhqk,bkhd->bqhd', p, v)

    qn_b = q_nope.reshape(b, nb, Q_BLOCK, h, QK_NOPE).swapaxes(0, 1)
    qr_b = q_rope.reshape(b, nb, Q_BLOCK, h, QK_ROPE).swapaxes(0, 1)
    out = lax.map(block, (qn_b, qr_b))
    return out.swapaxes(0, 1).reshape(b, s, h * V_HEAD)


def _peer_ffn(u, w_q, sub_keys, peer_u, peer_v):
    b, s, d = u.shape
    n_exp = peer_u.shape[0]
    ub = u.reshape((b * s) // TOK_BLOCK, TOK_BLOCK, d)

    def block(ut):
        q = (ut @ w_q).reshape(TOK_BLOCK, PEER_HEADS, 2, KEY_HALF)
        sc = jnp.einsum('thid,ind->thin', q, sub_keys)
        v_top, i_top = lax.top_k(sc, PEER_TOPK)
        cand = v_top[:, :, 0, :, None] + v_top[:, :, 1, None, :]
        cand_idx = i_top[:, :, 0, :, None] * N_KEYS + i_top[:, :, 1, None, :]
        cand = cand.reshape(TOK_BLOCK, PEER_HEADS, PEER_TOPK * PEER_TOPK)
        cand_idx = cand_idx.reshape(TOK_BLOCK, PEER_HEADS, PEER_TOPK * PEER_TOPK)
        best, pos = lax.top_k(cand, PEER_TOPK)
        idx = jnp.take_along_axis(cand_idx, pos, axis=-1)
        gate = jax.nn.softmax(best, axis=-1)
        act = jax.nn.gelu(jnp.einsum('td,thkd->thk', ut, peer_u[idx]), approximate=False)
        return jnp.einsum('thk,thkd->td', gate * act, peer_v[idx])

    return lax.map(block, ub).reshape(b, s, d)


IN_TB = 256
DT_PAD = 128
GATE_W = 2 * D_MODEL
IN_OUT_WIDTHS = (SSD_D_INNER, SSD_CONV_DIM, DT_PAD, Q_LORA, KV_LORA, HEAD_PAD, HEAD_PAD, GATE_W)


def _in_proj_kernel(x_ref, nw_ref, w_ref, *out_refs):
    u = _rms_rows(x_ref[...], nw_ref[...]).astype(jnp.bfloat16)
    col = 0
    for o_ref, width in zip(out_refs, IN_OUT_WIDTHS):
        o_ref[...] = jnp.dot(u, w_ref[:, col:col + width], preferred_element_type=jnp.float32)
        col += width


def _arrange_w_in(w_in):
    z, xbc, dt, qa, kva, kr, gates = jnp.split(w_in, _split_points(IN_SIZES), axis=1)
    d = w_in.shape[0]
    half = QK_ROPE // 2
    dt_p = jnp.pad(dt, ((0, 0), (0, DT_PAD - dt.shape[1])))
    kr3 = kr.reshape(d, 1, QK_ROPE)
    kr_p = _pad_heads(None, kr3)
    krs_p = _pad_heads(None, _rope_partner(kr3[..., :half], kr3[..., half:]))
    return jnp.concatenate([z, xbc, dt_p, qa, kva, kr_p, krs_p, gates], axis=1).astype(jnp.bfloat16)


def _in_proj(x2, norm_w, w_in):
    t, d = x2.shape
    w = _arrange_w_in(w_in)
    tb = min(IN_TB, t)
    return pl.pallas_call(
        _in_proj_kernel,
        grid=(t // tb,),
        in_specs=[pl.BlockSpec((tb, d), lambda i: (i, 0)),
                  pl.BlockSpec((1, d), lambda i: (0, 0)),
                  pl.BlockSpec(w.shape, lambda i: (0, 0))],
        out_specs=[pl.BlockSpec((tb, n), lambda i: (i, 0)) for n in IN_OUT_WIDTHS],
        out_shape=[jax.ShapeDtypeStruct((t, n), jnp.float32) for n in IN_OUT_WIDTHS],
        compiler_params=pltpu.CompilerParams(dimension_semantics=("arbitrary",),
                                             vmem_limit_bytes=56 << 20),
        name="in_proj",
    )(x2, norm_w.reshape(1, d), w)


HEAD_W = SSD_HEADS * SSD_HEADDIM
GROUP_W = HEAD_W // SSD_GROUPS
BC_W = SSD_GROUPS * SSD_STATE
_HI = lax.Precision.HIGHEST


def _ssd_kernel(direction, xc_ref, xp_ref, xn_ref, dt_ref, cw_ref, cb_ref, dtb_ref, a_ref, e_ref,
                tri_ref, *rest):
    f32, bf16 = jnp.float32, jnp.bfloat16
    fwd = direction == 0
    if fwd:
        dsk_ref, y_ref, h_scr = rest
    else:
        y1_ref, z_ref, nw_ref, y_ref, h_scr = rest
    c = pl.program_id(1)
    nc = pl.num_programs(1)
    cm = c if fwd else nc - 1 - c

    @pl.when(c == 0)
    def _():
        h_scr[...] = jnp.zeros_like(h_scr)

    xc = xc_ref[0]
    row = lax.broadcasted_iota(jnp.int32, xc.shape, 0)
    prev_row = jnp.where(cm > 0, xp_ref[0, 7:8, :], 0.0)
    next_row = jnp.where(cm < nc - 1, xn_ref[0, 0:1, :], 0.0)
    x_m1 = jnp.where(row == 0, prev_row, pltpu.roll(xc, 1, axis=0))
    x_p1 = jnp.where(row == CHUNK - 1, next_row, pltpu.roll(xc, CHUNK - 1, axis=0))
    conv = cw_ref[0:1, :] * x_m1 + cw_ref[1:2, :] * xc + cw_ref[2:3, :] * x_p1 + cb_ref[...]
    act = conv * jax.nn.sigmoid(conv)
    xs = act[:, :HEAD_W]

    dtr = dt_ref[0] + dtb_ref[...]
    dt = jnp.maximum(dtr, 0.0) + jnp.log(1.0 + jnp.exp(-jnp.abs(dtr)))
    a = dt * a_ref[...]
    cs = jnp.dot(tri_ref[...], a, precision=_HI, preferred_element_type=f32)
    expand = lambda v: jnp.dot(v, e_ref[...], precision=_HI, preferred_element_type=f32)
    cs_e = expand(cs)
    tot_e = cs_e[CHUNK - 1:CHUNK, :]
    if fwd:
        pos = cs
        w_state = jnp.exp(tot_e - cs_e)
        out_scale = jnp.exp(cs_e)
    else:
        pos = cs - a
        e_e = cs_e - expand(a)
        w_state = jnp.exp(e_e)
        out_scale = jnp.exp(tot_e - e_e)
    pos_t = pos.T
    xd = xs * expand(dt)
    xdb = xd.astype(bf16)
    xw = (xd * w_state).astype(bf16)
    chunk_decay = jnp.exp(tot_e)

    li = lax.broadcasted_iota(jnp.int32, (CHUNK, CHUNK), 0)
    si = lax.broadcasted_iota(jnp.int32, (CHUNK, CHUNK), 1)
    mask = (si <= li) if fwd else (si >= li)
    for g in range(SSD_GROUPS):
        gsl = slice(g * GROUP_W, (g + 1) * GROUP_W)
        bg = act[:, HEAD_W + g * SSD_STATE:HEAD_W + (g + 1) * SSD_STATE]
        cg = act[:, HEAD_W + BC_W + g * SSD_STATE:HEAD_W + BC_W + (g + 1) * SSD_STATE].astype(bf16)
        cb = lax.dot_general(cg, bg.astype(bf16), (((1,), (1,)), ((), ())), preferred_element_type=f32)
        h_in = h_scr[:, gsl]
        y_off = jnp.dot(cg, h_in.astype(bf16), preferred_element_type=f32) * out_scale[:, gsl]
        s_new = jnp.dot(bg.T.astype(bf16), xw[:, gsl], preferred_element_type=f32)
        h_scr[:, gsl] = h_in * chunk_decay[:, gsl] + s_new
        for r in range(SSD_HPG):
            hd = g * SSD_HPG + r
            lane = direction * SSD_HEADS + hd
            col = pos[:, lane:lane + 1]
            rw = pos_t[lane:lane + 1, :]
            diff = (col - rw) if fwd else (rw - col)
            m = (cb * jnp.exp(jnp.where(mask, diff, NEG_INF))).astype(bf16)
            hsl = slice(hd * SSD_HEADDIM, (hd + 1) * SSD_HEADDIM)
            y_d = jnp.dot(m, xdb[:, hsl], preferred_element_type=f32)
            y_ref[0, :, hsl] = y_d + y_off[:, r * SSD_HEADDIM:(r + 1) * SSD_HEADDIM]

    if fwd:
        y_ref[0] = y_ref[0] + xs * dsk_ref[...]
    else:
        yg = (y_ref[0] + y1_ref[0]) * (z_ref[0] * jax.nn.sigmoid(z_ref[0]))
        for g in range(SSD_GROUPS):
            gsl = slice(g * GROUP_W, (g + 1) * GROUP_W)
            y_ref[0, :, gsl] = _rms_rows(yg[:, gsl], nw_ref[:, gsl])


def _ssd_pass(direction, xbc, dt128, consts, extra):
    b, s, cw_ = xbc.shape
    nc = s // CHUNK
    per8 = CHUNK // 8
    cm = (lambda c: c) if direction == 0 else (lambda c: nc - 1 - c)
    cur = lambda n: pl.BlockSpec((1, CHUNK, n), lambda bi, c: (bi, cm(c), 0))
    prev = pl.BlockSpec((1, 8, cw_), lambda bi, c: (bi, jnp.maximum(cm(c) * per8 - 1, 0), 0))
    nxt = pl.BlockSpec((1, 8, cw_), lambda bi, c: (bi, jnp.minimum((cm(c) + 1) * per8, s // 8 - 1), 0))
    full = lambda a: pl.BlockSpec(a.shape, lambda bi, c: (0,) * a.ndim)
    if direction == 0:
        extra_specs = [full(extra[0])]
    else:
        extra_specs = [cur(HEAD_W), cur(HEAD_W), full(extra[2])]
    return pl.pallas_call(
        functools.partial(_ssd_kernel, direction),
        grid=(b, nc),
        in_specs=[cur(cw_), prev, nxt, cur(DT_PAD)] + [full(a) for a in consts] + extra_specs,
        out_specs=cur(HEAD_W),
        out_shape=jax.ShapeDtypeStruct((b, s, HEAD_W), jnp.float32),
        scratch_shapes=[pltpu.VMEM((SSD_STATE, HEAD_W), jnp.float32)],
        compiler_params=pltpu.CompilerParams(dimension_semantics=("arbitrary", "arbitrary"),
                                             vmem_limit_bytes=48 << 20),
        name="ssd_fwd" if direction == 0 else "ssd_bwd",
    )(xbc, xbc, xbc, dt128, *consts, *extra)


def _ssd_pallas(xbc, dt128, z, conv_w, conv_b, dt_bias_fwd, dt_bias_bwd, a_log_fwd, a_log_bwd,
                d_skip, ssd_norm_w):
    f32 = jnp.float32
    pad = lambda v: jnp.pad(v, (0, DT_PAD - v.shape[0])).reshape(1, DT_PAD)
    dtb = pad(jnp.concatenate([dt_bias_fwd, dt_bias_bwd]))
    a_neg = pad(-jnp.exp(jnp.concatenate([a_log_fwd, a_log_bwd])))
    tri = (jnp.arange(CHUNK)[None, :] <= jnp.arange(CHUNK)[:, None]).astype(f32)
    head_of_lane = jnp.arange(HEAD_W) // SSD_HEADDIM
    y = None
    for direction in range(2):
        e = (jnp.arange(DT_PAD)[:, None] == (direction * SSD_HEADS + head_of_lane)[None, :]).astype(f32)
        consts = (conv_w, conv_b.reshape(1, -1), dtb, a_neg, e, tri)
        if direction == 0:
            extra = (jnp.repeat(d_skip, SSD_HEADDIM).reshape(1, HEAD_W),)
        else:
            extra = (y, z, ssd_norm_w.reshape(1, HEAD_W))
        y = _ssd_pass(direction, xbc, dt128, consts, extra)
    return y


MERGE_TB = 256


def _merge_kernel(x_ref, ys_ref, ym_ref, g_ref, nw_ref, pa_ref, pb_ref, wo_ref, wq_ref,
                  h_ref, un_ref, q_ref):
    bf16 = jnp.bfloat16
    f32 = jnp.float32
    g = jax.nn.sigmoid(g_ref[...])
    ma = jnp.dot(ys_ref[...].astype(bf16), pa_ref[...], preferred_element_type=f32)
    mb = jnp.dot(ym_ref[...].astype(bf16), pb_ref[...], preferred_element_type=f32)
    m = g[:, :D_MODEL] * ma + g[:, D_MODEL:] * mb
    h = x_ref[...] + jnp.dot(m.astype(bf16), wo_ref[...], preferred_element_type=f32)
    h_ref[...] = h
    un = _rms_rows(h, nw_ref[...])
    un_ref[...] = un
    q_ref[...] = jnp.dot(un.astype(bf16), wq_ref[...], preferred_element_type=f32)


def _merge(x2, y_ssd, y_mla, gates, norm_ffn_w, w_proj_ssd, w_proj_mla, w_out, peer_w_q):
    t, d = x2.shape
    bf16 = jnp.bfloat16
    tb = min(MERGE_TB, t)
    nq = peer_w_q.shape[1]
    row = lambda n: pl.BlockSpec((tb, n), lambda i: (i, 0))
    full = lambda a: pl.BlockSpec(a.shape, lambda i: (0, 0))
    ws = [w.astype(bf16) for w in (w_proj_ssd, w_proj_mla, w_out, peer_w_q)]
    nw = norm_ffn_w.reshape(1, d)
    return pl.pallas_call(
        _merge_kernel,
        grid=(t // tb,),
        in_specs=[row(d), row(d), row(d), row(GATE_W), full(nw)] + [full(w) for w in ws],
        out_specs=[row(d), row(d), row(nq)],
        out_shape=[jax.ShapeDtypeStruct((t, d), jnp.float32), jax.ShapeDtypeStruct((t, d), jnp.float32),
                   jax.ShapeDtypeStruct((t, nq), jnp.float32)],
        compiler_params=pltpu.CompilerParams(dimension_semantics=("arbitrary",),
                                             vmem_limit_bytes=56 << 20),
        name="merge",
    )(x2, y_ssd, y_mla, gates, nw, *ws)


MLA_TB = 512
ATT_TQ = 512


def _rms_rows(x, w):
    return x * lax.rsqrt(jnp.mean(x * x, axis=-1, keepdims=True) + EPS) * w


def _mla_proj_kernel(pos_ref, qa_ref, kva_ref, kr_ref, krs_ref, qnw_ref, kvnw_ref, inv_ref,
                     wq_ref, wqs_ref, wk_ref, wv_ref, q_out, k_out, v_out):
    bf16 = jnp.bfloat16
    ang = pos_ref[...].astype(jnp.float32) * inv_ref[...]
    cos, sin = jnp.cos(ang), jnp.sin(ang)
    qa = _rms_rows(qa_ref[...], qnw_ref[...]).astype(bf16)
    q = jnp.dot(qa, wq_ref[...], preferred_element_type=jnp.float32)
    qs = jnp.dot(qa, wqs_ref[...], preferred_element_type=jnp.float32)
    kva = _rms_rows(kva_ref[...], kvnw_ref[...]).astype(bf16)
    k = jnp.dot(kva, wk_ref[...], preferred_element_type=jnp.float32)
    v_out[...] = jnp.dot(kva, wv_ref[...], preferred_element_type=jnp.float32).astype(bf16)
    kro = kr_ref[...] * cos + krs_ref[...] * sin
    c = (QK_DIM ** -0.5) * LOG2E
    for h in range(MLA_HEADS):
        sl = slice(h * HEAD_PAD, (h + 1) * HEAD_PAD)
        q_out[:, sl] = ((q[:, sl] * cos + qs[:, sl] * sin) * c).astype(bf16)
        k_out[:, sl] = (k[:, sl] + kro).astype(bf16)


def _rope_partner(w1, w2):
    return jnp.concatenate([-w2, w1], axis=-1)


def _pad_heads(nope, rope):
    ref = nope if nope is not None else rope
    lead = ref.shape[:-1]
    z = lambda n: jnp.zeros(lead + (n,), ref.dtype)
    parts = [nope if nope is not None else z(QK_NOPE), rope if rope is not None else z(QK_ROPE),
             z(HEAD_PAD - QK_DIM)]
    out = jnp.concatenate(parts, axis=-1)
    return out.reshape(out.shape[:-2] + (out.shape[-2] * HEAD_PAD,))


def _mla_proj(positions, q_a, kv_a, kr, krs, q_a_norm_w, kv_a_norm_w, w_q_b, w_kv_b):
    t = q_a.shape[0]
    bf16 = jnp.bfloat16
    half = QK_ROPE // 2
    wq = w_q_b.reshape(Q_LORA, MLA_HEADS, QK_DIM)
    wq_n, wq_r = wq[..., :QK_NOPE], wq[..., QK_NOPE:]
    wq_main = _pad_heads(wq_n, wq_r).astype(bf16)
    wq_part = _pad_heads(None, _rope_partner(wq_r[..., :half], wq_r[..., half:])).astype(bf16)
    wkv = w_kv_b.reshape(KV_LORA, MLA_HEADS, QK_NOPE + V_HEAD)
    wk = _pad_heads(wkv[..., :QK_NOPE], None).astype(bf16)
    wv = wkv[..., QK_NOPE:].reshape(KV_LORA, MLA_HEADS * V_HEAD).astype(bf16)
    inv = 1.0 / (ROPE_THETA ** (jnp.arange(0, QK_ROPE, 2, dtype=jnp.float32) / QK_ROPE))
    inv_lane = jnp.concatenate([jnp.zeros((QK_NOPE,), jnp.float32), inv, inv,
                                jnp.zeros((HEAD_PAD - QK_DIM,), jnp.float32)]).reshape(1, HEAD_PAD)
    tb = min(MLA_TB, t)
    row = lambda n: pl.BlockSpec((tb, n), lambda i: (i, 0))
    full = lambda a: pl.BlockSpec(a.shape, lambda i: (0,) * a.ndim)
    qn = q_a_norm_w.reshape(1, Q_LORA)
    kvn = kv_a_norm_w.reshape(1, KV_LORA)
    args = (positions.reshape(t, 1), q_a, kv_a, kr, krs, qn, kvn, inv_lane, wq_main, wq_part, wk, wv)
    in_specs = [row(1), row(Q_LORA), row(KV_LORA), row(HEAD_PAD), row(HEAD_PAD)] + [full(a) for a in args[5:]]
    nq = MLA_HEADS * HEAD_PAD
    return pl.pallas_call(
        _mla_proj_kernel,
        grid=(t // tb,),
        in_specs=in_specs,
        out_specs=[row(nq), row(nq), row(MLA_HEADS * V_HEAD)],
        out_shape=[jax.ShapeDtypeStruct((t, nq), bf16), jax.ShapeDtypeStruct((t, nq), bf16),
                   jax.ShapeDtypeStruct((t, MLA_HEADS * V_HEAD), bf16)],
        compiler_params=pltpu.CompilerParams(dimension_semantics=("arbitrary",),
                                             vmem_limit_bytes=48 << 20),
        name="mla_proj",
    )(*args)


def _attn_kernel(q_ref, k_ref, v_ref, o_ref):
    tq = q_ref.shape[1]
    v = v_ref[0]
    outs = []
    for hh in range(2):
        sl = slice(hh * HEAD_PAD, (hh + 1) * HEAD_PAD)
        s = lax.dot_general(q_ref[0, :, sl], k_ref[0, :, sl], (((1,), (1,)), ((), ())),
                            preferred_element_type=jnp.float32)
        m = jnp.max(s, axis=-1, keepdims=True)
        p = jnp.exp2(s - m)
        l = jnp.sum(p, axis=-1, keepdims=True)
        pv = jnp.dot(p.astype(jnp.bfloat16), v, preferred_element_type=jnp.float32)
        outs.append(pv / l)
    lane = lax.broadcasted_iota(jnp.int32, (tq, 2 * V_HEAD), 1)
    o_ref[0] = jnp.where(lane < V_HEAD, outs[0], outs[1]).astype(o_ref.dtype)


def _attention(q, k, v):
    b, s, _ = q.shape
    tq = min(ATT_TQ, s)
    return pl.pallas_call(
        _attn_kernel,
        grid=(b, MLA_HEADS // 2, s // tq),
        in_specs=[pl.BlockSpec((1, tq, 2 * HEAD_PAD), lambda bi, j, i: (bi, i, j)),
                  pl.BlockSpec((1, s, 2 * HEAD_PAD), lambda bi, j, i: (bi, 0, j)),
                  pl.BlockSpec((1, s, 2 * V_HEAD), lambda bi, j, i: (bi, 0, j))],
        out_specs=pl.BlockSpec((1, tq, 2 * V_HEAD), lambda bi, j, i: (bi, i, j)),
        out_shape=jax.ShapeDtypeStruct((b, s, MLA_HEADS * V_HEAD), jnp.bfloat16),
        compiler_params=pltpu.CompilerParams(dimension_semantics=("arbitrary",) * 3,
                                             vmem_limit_bytes=56 << 20),
        name="mla_attention",
    )(q, k, v)


def _mla_pallas(positions, q_a, kv_a, kr, krs, q_a_norm_w, kv_a_norm_w, w_q_b, w_kv_b):
    b, s = positions.shape
    q, k, v = _mla_proj(positions, q_a, kv_a, kr, krs, q_a_norm_w, kv_a_norm_w, w_q_b, w_kv_b)
    y = _attention(q.reshape(b, s, -1), k.reshape(b, s, -1), v.reshape(b, s, -1))
    return y.reshape(b * s, -1)


PEER_TB = 128
PEER_K = PEER_HEADS * PEER_TOPK
HALF_D = D_MODEL // 2
NEG_INF = float("-inf")


def _top16_rows(x, extra=None):
    rows = x.shape[0]
    iota = lax.broadcasted_iota(jnp.int32, x.shape, 0)
    vals, poss, exts = [], [], []
    for _ in range(PEER_TOPK):
        m = jnp.max(x, axis=0, keepdims=True)
        p = jnp.min(jnp.where(x == m, iota, rows), axis=0, keepdims=True)
        hit = iota == p
        if extra is not None:
            exts.append(jnp.sum(jnp.where(hit, extra, 0), axis=0, keepdims=True))
        x = jnp.where(hit, NEG_INF, x)
        vals.append(m)
        poss.append(p)
    vals = jnp.concatenate(vals, axis=0)
    poss = jnp.concatenate(poss, axis=0)
    if extra is None:
        return vals, poss
    return vals, poss, jnp.concatenate(exts, axis=0)


def _peer_topk_kernel(q_ref, keys_ref, idx_ref, gate_ref):
    for h in range(PEER_HEADS):
        v, ids = [], []
        for i in range(2):
            c0 = (h * 2 + i) * KEY_HALF
            qhi = q_ref[:, c0:c0 + KEY_HALF]
            sc = lax.dot_general(keys_ref[i], qhi, (((1,), (1,)), ((), ())),
                                 preferred_element_type=jnp.float32)
            vi, pi = _top16_rows(sc)
            v.append(vi)
            ids.append(pi)
        cand = jnp.concatenate([v[0][a:a + 1, :] + v[1] for a in range(PEER_TOPK)], axis=0)
        cidx = jnp.concatenate([ids[0][a:a + 1, :] * N_KEYS + ids[1] for a in range(PEER_TOPK)], axis=0)
        best, _, eidx = _top16_rows(cand, cidx)
        e = jnp.exp(best - best[0:1, :])
        gate = e / jnp.sum(e, axis=0, keepdims=True)
        idx_ref[h * PEER_TOPK:(h + 1) * PEER_TOPK, :] = eidx
        gate_ref[h * PEER_TOPK:(h + 1) * PEER_TOPK, :] = gate


def _peer_topk(q, sub_keys):
    t = q.shape[0]
    return pl.pallas_call(
        _peer_topk_kernel,
        grid=(t // PEER_TB,),
        in_specs=[pl.BlockSpec((PEER_TB, q.shape[1]), lambda i: (i, 0)),
                  pl.BlockSpec(sub_keys.shape, lambda i: (0, 0, 0))],
        out_specs=[pl.BlockSpec((PEER_K, PEER_TB), lambda i: (0, i)),
                   pl.BlockSpec((PEER_K, PEER_TB), lambda i: (0, i))],
        out_shape=[jax.ShapeDtypeStruct((PEER_K, t), jnp.int32),
                   jax.ShapeDtypeStruct((PEER_K, t), jnp.float32)],
        name="peer_topk",
    )(q, sub_keys)


def _pack_table(tab):
    e = tab.shape[0]
    tb = tab.astype(jnp.bfloat16)
    lo = lax.bitcast_convert_type(tb[:, :HALF_D], jnp.uint16).astype(jnp.uint32)
    hi = lax.bitcast_convert_type(tb[:, HALF_D:], jnp.uint16).astype(jnp.uint32)
    return (lo | (hi << 16)).reshape(e, HALF_D // 128, 128)


def _gather_rows(idx_ref, tok, tab_ref, scr):
    nchunk = HALF_D // 128
    for k in range(PEER_K):
        e = idx_ref[tok * PEER_K + k]
        scr[k // 8, pl.ds(k % 8, nchunk, stride=8), :] = tab_ref[e]
    gm = jnp.concatenate(
        [scr[:, j * 8:(j + 1) * 8, :].reshape(PEER_K, 128) for j in range(nchunk)], axis=1)
    return pltpu.bitcast(gm, jnp.bfloat16)


def _peer_u_kernel(idx_ref, ut_ref, tab_ref, out_ref, scr0, scr1):
    shp = (8, 2 * PEER_K)
    row = lax.broadcasted_iota(jnp.int32, shp, 0)
    even = (lax.broadcasted_iota(jnp.int32, shp, 1) & 1) == 0

    def group(g, carry):
        base = pl.multiple_of(g * 8, 8)
        ut8 = ut_ref[pl.ds(base, 8), :]
        lhs = jnp.concatenate([ut8[:, :HALF_D], ut8[:, HALF_D:]], axis=0).astype(jnp.bfloat16)
        acc = jnp.zeros(shp, jnp.float32)
        for i in range(8):
            b = _gather_rows(idx_ref, base + i, tab_ref, (scr0, scr1)[i % 2])
            r = lax.dot_general(lhs, b, (((1,), (1,)), ((), ())),
                                preferred_element_type=jnp.float32)
            acc = jnp.where(row == i, jnp.where(even, r[0:8], r[8:16]), acc)
        out_ref[pl.ds(base, 8), :] = acc
        return carry

    lax.fori_loop(0, PEER_TB // 8, group, 0)


def _peer_v_kernel(idx_ref, a_ref, gate_ref, tab_ref, out_ref, scr0, scr1):
    shp = (8, 2 * PEER_K)
    row = lax.broadcasted_iota(jnp.int32, (8, HALF_D), 0)
    even = (lax.broadcasted_iota(jnp.int32, shp, 1) & 1) == 0

    def group(g, carry):
        base = pl.multiple_of(g * 8, 8)
        a8 = a_ref[pl.ds(base, 8), :]
        s = a8 + pltpu.roll(a8, 2 * PEER_K - 1, axis=1)
        act = jnp.where(even, s, pltpu.roll(s, 1, axis=1))
        gelu = 0.5 * act * (1.0 + lax.erf(act * (2.0 ** -0.5)))
        w = gelu * gate_ref[pl.ds(base, 8), :]
        lhs = jnp.concatenate([jnp.where(even, w, 0.0), jnp.where(even, 0.0, w)],
                              axis=0).astype(jnp.bfloat16)
        acc_lo = jnp.zeros((8, HALF_D), jnp.float32)
        acc_hi = jnp.zeros((8, HALF_D), jnp.float32)
        for i in range(8):
            b = _gather_rows(idx_ref, base + i, tab_ref, (scr0, scr1)[i % 2])
            r = jnp.dot(lhs, b, preferred_element_type=jnp.float32)
            acc_lo = jnp.where(row == i, r[0:8], acc_lo)
            acc_hi = jnp.where(row == i, r[8:16], acc_hi)
        out_ref[pl.ds(base, 8), 0:HALF_D] = acc_lo
        out_ref[pl.ds(base, 8), HALF_D:] = acc_hi
        return carry

    lax.fori_loop(0, PEER_TB // 8, group, 0)


def _peer_scratch():
    return [pltpu.VMEM((PEER_K // 8, 8 * (HALF_D // 128), 128), jnp.uint32) for _ in range(2)]


def _peer_params(tab):
    table_bytes = tab.size * 4
    return pltpu.CompilerParams(dimension_semantics=("arbitrary",),
                                vmem_limit_bytes=table_bytes + (16 << 20))


def _peer_u_pass(idx_flat, un, tab):
    t = un.shape[0]
    return pl.pallas_call(
        _peer_u_kernel,
        grid=(t // PEER_TB,),
        in_specs=[pl.BlockSpec((PEER_TB * PEER_K,), lambda i: (i,), memory_space=pltpu.SMEM),
                  pl.BlockSpec((PEER_TB, D_MODEL), lambda i: (i, 0)),
                  pl.BlockSpec(memory_space=pltpu.VMEM)],
        out_specs=pl.BlockSpec((PEER_TB, 2 * PEER_K), lambda i: (i, 0)),
        out_shape=jax.ShapeDtypeStruct((t, 2 * PEER_K), jnp.float32),
        scratch_shapes=_peer_scratch(),
        compiler_params=_peer_params(tab),
        name="peer_u_pass",
    )(idx_flat, un, tab)


def _peer_v_pass(idx_flat, a, gate2, tab):
    t = a.shape[0]
    return pl.pallas_call(
        _peer_v_kernel,
        grid=(t // PEER_TB,),
        in_specs=[pl.BlockSpec((PEER_TB * PEER_K,), lambda i: (i,), memory_space=pltpu.SMEM),
                  pl.BlockSpec((PEER_TB, 2 * PEER_K), lambda i: (i, 0)),
                  pl.BlockSpec((PEER_TB, 2 * PEER_K), lambda i: (i, 0)),
                  pl.BlockSpec(memory_space=pltpu.VMEM)],
        out_specs=pl.BlockSpec((PEER_TB, D_MODEL), lambda i: (i, 0)),
        out_shape=jax.ShapeDtypeStruct((t, D_MODEL), jnp.float32),
        scratch_shapes=_peer_scratch(),
        compiler_params=_peer_params(tab),
        name="peer_v_pass",
    )(idx_flat, a, gate2, tab)


def _peer_pallas(un, q, sub_keys, peer_u, peer_v):
    idx_t, gate_t = _peer_topk(q, sub_keys)
    idx_flat = idx_t.T.reshape(-1)
    gate2 = jnp.repeat(gate_t.T, 2, axis=1)
    a = _peer_u_pass(idx_flat, un, _pack_table(peer_u))
    return _peer_v_pass(idx_flat, a, gate2, _pack_table(peer_v))


def dev_stages():
    s = jax.ShapeDtypeStruct
    f32, i32, u32 = jnp.float32, jnp.int32, jnp.uint32
    t = 32768
    bf16 = jnp.bfloat16
    return {
        "mlaproj": (_mla_proj, (s((8, 4096), i32), s((t, 384), f32), s((t, 256), f32), s((t, 128), f32),
                                s((t, 128), f32), s((384,), f32), s((256,), f32), s((384, 1536), f32),
                                s((256, 2048), f32))),
        "ssd": (_ssd_pallas, (s((8, 4096, 1536), f32), s((8, 4096, 128), f32), s((8, 4096, 1024), f32),
                              s((3, 1536), f32), s((1536,), f32), s((16,), f32), s((16,), f32), s((16,), f32),
                              s((16,), f32), s((16,), f32), s((1024,), f32))),
        "inproj": (_in_proj, (s((t, 1024), f32), s((1024,), f32), s((1024, 5312), f32))),
        "merge": (_merge, (s((t, 1024), f32), s((t, 1024), f32), s((t, 1024), bf16), s((t, 2048), f32),
                           s((1024,), f32), s((1024, 1024), f32), s((1024, 1024), f32),
                           s((1024, 1024), f32), s((1024, 2048), f32))),
        "attn": (_attention, (s((8, 4096, 2048), bf16), s((8, 4096, 2048), bf16), s((8, 4096, 1024), bf16))),
        "topk": (_peer_topk, (s((t, 2048), f32), s((2, 128, 128), f32))),
        "upass": (_peer_u_pass, (s((t * 128,), i32), s((t, 1024), f32), s((16384, 4, 128), u32))),
        "vpass": (_peer_v_pass, (s((t * 128,), i32), s((t, 256), f32), s((t, 256), f32), s((16384, 4, 128), u32))),
    }


def _final_norm_kernel(h_ref, p_ref, w_ref, o_ref):
    hh = h_ref[...] + p_ref[...]
    ms = jnp.mean(hh * hh, axis=-1, keepdims=True)
    o_ref[...] = hh * lax.rsqrt(ms + EPS) * w_ref[...]


def _final_norm(h, p, w):
    t, d = h.shape
    tb = 512
    return pl.pallas_call(
        _final_norm_kernel,
        grid=(t // tb,),
        in_specs=[pl.BlockSpec((tb, d), lambda i: (i, 0)),
                  pl.BlockSpec((tb, d), lambda i: (i, 0)),
                  pl.BlockSpec((1, d), lambda i: (0, 0))],
        out_specs=pl.BlockSpec((tb, d), lambda i: (i, 0)),
        out_shape=jax.ShapeDtypeStruct((t, d), jnp.float32),
    )(h, p, w.reshape(1, d))


def kernel(x, positions, norm_mix_w, w_in, conv_w, conv_b, dt_bias_fwd, dt_bias_bwd, a_log_fwd, a_log_bwd, d_skip, ssd_norm_w, q_a_norm_w, w_q_b, kv_a_norm_w, w_kv_b, w_proj_ssd, w_proj_mla, w_out, norm_ffn_w, peer_w_q, peer_sub_keys, peer_u, peer_v, norm_final_w):
    b, s, _ = x.shape
    x2 = x.reshape(b * s, D_MODEL)
    z, xbc, dt128, q_a, kv_a, kr, krs, gate_logits = _in_proj(x2, norm_mix_w, w_in)
    y_ssd = _ssd_pallas(xbc.reshape(b, s, -1), dt128.reshape(b, s, -1), z.reshape(b, s, -1), conv_w, conv_b,
                        dt_bias_fwd, dt_bias_bwd, a_log_fwd, a_log_bwd, d_skip, ssd_norm_w)

    y_mla = _mla_pallas(positions, q_a, kv_a, kr, krs, q_a_norm_w, kv_a_norm_w, w_q_b, w_kv_b)

    h2, un, q_peer = _merge(x2, y_ssd.reshape(b * s, -1), y_mla, gate_logits, norm_ffn_w,
                            w_proj_ssd, w_proj_mla, w_out, peer_w_q)
    pf = _peer_pallas(un, q_peer, peer_sub_keys, peer_u, peer_v)
    out = _final_norm(h2, pf, norm_final_w)
    return out.reshape(b, s, D_MODEL)
```

```python
import functools
import math

import jax
import jax.numpy as jnp
from jax import lax
from jax.experimental import pallas as pl
from jax.experimental.pallas import tpu as pltpu

D_MODEL = 1024
SSD_D_INNER = 1024
SSD_HEADDIM = 64
SSD_HEADS = 16
SSD_GROUPS = 2
SSD_HPG = 8
SSD_STATE = 128
CONV_WIDTH = 3
CHUNK = 128
SSD_CONV_DIM = SSD_D_INNER + 2 * SSD_GROUPS * SSD_STATE
MLA_HEADS = 16
QK_NOPE = 64
QK_ROPE = 32
V_HEAD = 64
Q_LORA = 384
KV_LORA = 256
Q_BLOCK = 128
ROPE_THETA = 10000.0
PEER_HEADS = 8
N_KEYS = 128
KEY_HALF = 128
PEER_TOPK = 16
TOK_BLOCK = 128
EPS = 1e-6
HEAD_PAD = 128
QK_DIM = QK_NOPE + QK_ROPE
LOG2E = 1.4426950408889634
IN_SIZES = (SSD_D_INNER, SSD_CONV_DIM, 2 * SSD_HEADS, Q_LORA, KV_LORA, QK_ROPE, 2 * D_MODEL)


def _split_points(sizes):
    pts, acc = [], 0
    for s in sizes[:-1]:
        acc += s
        pts.append(acc)
    return pts


def _rms(x, w):
    y = x * lax.rsqrt(jnp.mean(x * x, axis=-1, keepdims=True) + EPS)
    return y * w


def _segsum(a):
    t = a.shape[-1]
    cs = jnp.cumsum(a, axis=-1)
    diff = cs[..., :, None] - cs[..., None, :]
    mask = jnp.tril(jnp.ones((t, t), dtype=bool))
    return jnp.where(mask, diff, -jnp.inf)


def _ssd_scan(xh, dt, A, Bm, Cm):
    b, l, g, r, p = xh.shape
    n = Bm.shape[-1]
    c = l // CHUNK
    xd = (xh * dt[..., None]).reshape(b, c, CHUNK, g, r, p)
    a = (dt * A).reshape(b, c, CHUNK, g, r)
    a = jnp.moveaxis(a, 2, -1)
    Bc = Bm.reshape(b, c, CHUNK, g, n)
    Cc = Cm.reshape(b, c, CHUNK, g, n)
    a_cs = jnp.cumsum(a, axis=-1)
    L = jnp.exp(_segsum(a))
    CB = jnp.einsum('bclgn,bcsgn->bcgls', Cc, Bc)
    y_diag = jnp.einsum('bcgls,bcgrls,bcsgrp->bclgrp', CB, L, xd)
    decay_states = jnp.exp(a_cs[..., -1:] - a_cs)
    states = jnp.einsum('bcsgn,bcgrs,bcsgrp->bcgrpn', Bc, decay_states, xd)
    chunk_tot = jnp.moveaxis(a_cs[..., -1], 1, -1)
    decay_chunk = jnp.exp(_segsum(jnp.pad(chunk_tot, ((0, 0), (0, 0), (0, 0), (1, 0)))))
    states0 = jnp.pad(states, ((0, 0), (1, 0), (0, 0), (0, 0), (0, 0), (0, 0)))
    new_states = jnp.einsum('bgrzc,bcgrpn->bzgrpn', decay_chunk, states0)
    prev = new_states[:, :-1]
    y_off = jnp.einsum('bclgn,bcgrpn,bcgrl->bclgrp', Cc, prev, jnp.exp(a_cs))
    return (y_diag + y_off).reshape(b, l, g, r, p)


def _rope_tables(positions, dim):
    inv = 1.0 / (ROPE_THETA ** (jnp.arange(0, dim, 2, dtype=jnp.float32) / dim))
    ang = positions.astype(jnp.float32)[..., None] * inv
    return jnp.cos(ang), jnp.sin(ang)


def _apply_rope(x, cos, sin):
    x1, x2 = jnp.split(x, 2, axis=-1)
    return jnp.concatenate([x1 * cos - x2 * sin, x1 * sin + x2 * cos], axis=-1)


def _mla_attention(q_nope, q_rope, k_nope, k_rope, v):
    b, s, h, _ = q_nope.shape
    nb = s // Q_BLOCK
    scale = (QK_NOPE + QK_ROPE) ** -0.5

    def block(args):
        qn, qr = args
        sc = (jnp.einsum('bqhd,bkhd->bhqk', qn, k_nope)
              + jnp.einsum('bqhd,bkd->bhqk', qr, k_rope))
        p = jax.nn.softmax(sc * scale, axis=-1)
        return jnp.einsum('bhqk,bkhd->bqhd', p, v)

    qn_b = q_nope.reshape(b, nb, Q_BLOCK, h, QK_NOPE).swapaxes(0, 1)
    qr_b = q_rope.reshape(b, nb, Q_BLOCK, h, QK_ROPE).swapaxes(0, 1)
    out = lax.map(block, (qn_b, qr_b))
    return out.swapaxes(0, 1).reshape(b, s, h * V_HEAD)


def _peer_ffn(u, w_q, sub_keys, peer_u, peer_v):
    b, s, d = u.shape
    n_exp = peer_u.shape[0]
    ub = u.reshape((b * s) // TOK_BLOCK, TOK_BLOCK, d)

    def block(ut):
        q = (ut @ w_q).reshape(TOK_BLOCK, PEER_HEADS, 2, KEY_HALF)
        sc = jnp.einsum('thid,ind->thin', q, sub_keys)
        v_top, i_top = lax.top_k(sc, PEER_TOPK)
        cand = v_top[:, :, 0, :, None] + v_top[:, :, 1, None, :]
        cand_idx = i_top[:, :, 0, :, None] * N_KEYS + i_top[:, :, 1, None, :]
        cand = cand.reshape(TOK_BLOCK, PEER_HEADS, PEER_TOPK * PEER_TOPK)
        cand_idx = cand_idx.reshape(TOK_BLOCK, PEER_HEADS, PEER_TOPK * PEER_TOPK)
        best, pos = lax.top_k(cand, PEER_TOPK)
        idx = jnp.take_along_axis(cand_idx, pos, axis=-1)
        gate = jax.nn.softmax(best, axis=-1)
        act = jax.nn.gelu(jnp.einsum('td,thkd->thk', ut, peer_u[idx]), approximate=False)
        return jnp.einsum('thk,thkd->td', gate * act, peer_v[idx])

    return lax.map(block, ub).reshape(b, s, d)


IN_TB = 256
DT_PAD = 128
GATE_W = 2 * D_MODEL
IN_OUT_WIDTHS = (SSD_D_INNER, SSD_CONV_DIM, DT_PAD, Q_LORA, KV_LORA, HEAD_PAD, HEAD_PAD, GATE_W)


def _in_proj_kernel(x_ref, nw_ref, w_ref, *out_refs):
    u = _rms_rows(x_ref[...], nw_ref[...]).astype(jnp.bfloat16)
    col = 0
    for o_ref, width in zip(out_refs, IN_OUT_WIDTHS):
        o_ref[...] = jnp.dot(u, w_ref[:, col:col + width], preferred_element_type=jnp.float32)
        col += width


def _arrange_w_in(w_in):
    z, xbc, dt, qa, kva, kr, gates = jnp.split(w_in, _split_points(IN_SIZES), axis=1)
    d = w_in.shape[0]
    half = QK_ROPE // 2
    dt_p = jnp.pad(dt, ((0, 0), (0, DT_PAD - dt.shape[1])))
    kr3 = kr.reshape(d, 1, QK_ROPE)
    kr_p = _pad_heads(None, kr3)
    krs_p = _pad_heads(None, _rope_partner(kr3[..., :half], kr3[..., half:]))
    return jnp.concatenate([z, xbc, dt_p, qa, kva, kr_p, krs_p, gates], axis=1).astype(jnp.bfloat16)


def _in_proj(x2, norm_w, w_in):
    t, d = x2.shape
    w = _arrange_w_in(w_in)
    tb = min(IN_TB, t)
    return pl.pallas_call(
        _in_proj_kernel,
        grid=(t // tb,),
        in_specs=[pl.BlockSpec((tb, d), lambda i: (i, 0)),
                  pl.BlockSpec((1, d), lambda i: (0, 0)),
                  pl.BlockSpec(w.shape, lambda i: (0, 0))],
        out_specs=[pl.BlockSpec((tb, n), lambda i: (i, 0)) for n in IN_OUT_WIDTHS],
        out_shape=[jax.ShapeDtypeStruct((t, n), jnp.float32) for n in IN_OUT_WIDTHS],
        compiler_params=pltpu.CompilerParams(dimension_semantics=("arbitrary",),
                                             vmem_limit_bytes=56 << 20),
        name="in_proj",
    )(x2, norm_w.reshape(1, d), w)


HEAD_W = SSD_HEADS * SSD_HEADDIM
GROUP_W = HEAD_W // SSD_GROUPS
BC_W = SSD_GROUPS * SSD_STATE
_HI = lax.Precision.HIGHEST


def _ssd_kernel(direction, xc_ref, xp_ref, xn_ref, dt_ref, cw_ref, cb_ref, dtb_ref, a_ref, e_ref,
                tri_ref, *rest):
    f32, bf16 = jnp.float32, jnp.bfloat16
    fwd = direction == 0
    if fwd:
        dsk_ref, y_ref, h_scr = rest
    else:
        y1_ref, z_ref, nw_ref, y_ref, h_scr = rest
    c = pl.program_id(1)
    nc = pl.num_programs(1)
    cm = c if fwd else nc - 1 - c

    @pl.when(c == 0)
    def _():
        h_scr[...] = jnp.zeros_like(h_scr)

    xc = xc_ref[0]
    row = lax.broadcasted_iota(jnp.int32, xc.shape, 0)
    prev_row = jnp.where(cm > 0, xp_ref[0, 7:8, :], 0.0)
    next_row = jnp.where(cm < nc - 1, xn_ref[0, 0:1, :], 0.0)
    x_m1 = jnp.where(row == 0, prev_row, pltpu.roll(xc, 1, axis=0))
    x_p1 = jnp.where(row == CHUNK - 1, next_row, pltpu.roll(xc, CHUNK - 1, axis=0))
    conv = cw_ref[0:1, :] * x_m1 + cw_ref[1:2, :] * xc + cw_ref[2:3, :] * x_p1 + cb_ref[...]
    act = conv * jax.nn.sigmoid(conv)
    xs = act[:, :HEAD_W]

    dtr = dt_ref[0] + dtb_ref[...]
    dt = jnp.maximum(dtr, 0.0) + jnp.log(1.0 + jnp.exp(-jnp.abs(dtr)))
    a = dt * a_ref[...]
    cs = jnp.dot(tri_ref[...], a, precision=_HI, preferred_element_type=f32)
    expand = lambda v: jnp.dot(v, e_ref[...], precision=_HI, preferred_element_type=f32)
    cs_e = expand(cs)
    tot_e = cs_e[CHUNK - 1:CHUNK, :]
    if fwd:
        pos = cs
        w_state = jnp.exp(tot_e - cs_e)
        out_scale = jnp.exp(cs_e)
    else:
        pos = cs - a
        e_e = cs_e - expand(a)
        w_state = jnp.exp(e_e)
        out_scale = jnp.exp(tot_e - e_e)
    pos_t = pos.T
    xd = xs * expand(dt)
    xdb = xd.astype(bf16)
    xw = (xd * w_state).astype(bf16)
    chunk_decay = jnp.exp(tot_e)

    li = lax.broadcasted_iota(jnp.int32, (CHUNK, CHUNK), 0)
    si = lax.broadcasted_iota(jnp.int32, (CHUNK, CHUNK), 1)
    mask = (si <= li) if fwd else (si >= li)
    for g in range(SSD_GROUPS):
        gsl = slice(g * GROUP_W, (g + 1) * GROUP_W)
        bg = act[:, HEAD_W + g * SSD_STATE:HEAD_W + (g + 1) * SSD_STATE]
        cg = act[:, HEAD_W + BC_W + g * SSD_STATE:HEAD_W + BC_W + (g + 1) * SSD_STATE].astype(bf16)
        cb = lax.dot_general(cg, bg.astype(bf16), (((1,), (1,)), ((), ())), preferred_element_type=f32)
        h_in = h_scr[:, gsl]
        y_off = jnp.dot(cg, h_in.astype(bf16), preferred_element_type=f32) * out_scale[:, gsl]
        s_new = jnp.dot(bg.T.astype(bf16), xw[:, gsl], preferred_element_type=f32)
        h_scr[:, gsl] = h_in * chunk_decay[:, gsl] + s_new
        for r in range(SSD_HPG):
            hd = g * SSD_HPG + r
            lane = direction * SSD_HEADS + hd
            col = pos[:, lane:lane + 1]
            rw = pos_t[lane:lane + 1, :]
            diff = (col - rw) if fwd else (rw - col)
            m = (cb * jnp.exp(jnp.where(mask, diff, NEG_INF))).astype(bf16)
            hsl = slice(hd * SSD_HEADDIM, (hd + 1) * SSD_HEADDIM)
            y_d = jnp.dot(m, xdb[:, hsl], preferred_element_type=f32)
            y_ref[0, :, hsl] = y_d + y_off[:, r * SSD_HEADDIM:(r + 1) * SSD_HEADDIM]

    if fwd:
        y_ref[0] = y_ref[0] + xs * dsk_ref[...]
    else:
        yg = (y_ref[0] + y1_ref[0]) * (z_ref[0] * jax.nn.sigmoid(z_ref[0]))
        for g in range(SSD_GROUPS):
            gsl = slice(g * GROUP_W, (g + 1) * GROUP_W)
            y_ref[0, :, gsl] = _rms_rows(yg[:, gsl], nw_ref[:, gsl])


def _ssd_pass(direction, xbc, dt128, consts, extra):
    b, s, cw_ = xbc.shape
    nc = s // CHUNK
    per8 = CHUNK // 8
    cm = (lambda c: c) if direction == 0 else (lambda c: nc - 1 - c)
    cur = lambda n: pl.BlockSpec((1, CHUNK, n), lambda bi, c: (bi, cm(c), 0))
    prev = pl.BlockSpec((1, 8, cw_), lambda bi, c: (bi, jnp.maximum(cm(c) * per8 - 1, 0), 0))
    nxt = pl.BlockSpec((1, 8, cw_), lambda bi, c: (bi, jnp.minimum((cm(c) + 1) * per8, s // 8 - 1), 0))
    full = lambda a: pl.BlockSpec(a.shape, lambda bi, c: (0,) * a.ndim)
    if direction == 0:
        extra_specs = [full(extra[0])]
    else:
        extra_specs = [cur(HEAD_W), cur(HEAD_W), full(extra[2])]
    return pl.pallas_call(
        functools.partial(_ssd_kernel, direction),
        grid=(b, nc),
        in_specs=[cur(cw_), prev, nxt, cur(DT_PAD)] + [full(a) for a in consts] + extra_specs,
        out_specs=cur(HEAD_W),
        out_shape=jax.ShapeDtypeStruct((b, s, HEAD_W), jnp.float32),
        scratch_shapes=[pltpu.VMEM((SSD_STATE, HEAD_W), jnp.float32)],
        compiler_params=pltpu.CompilerParams(dimension_semantics=("arbitrary", "arbitrary"),
                                             vmem_limit_bytes=48 << 20),
        name="ssd_fwd" if direction == 0 else "ssd_bwd",
    )(xbc, xbc, xbc, dt128, *consts, *extra)


def _ssd_pallas(xbc, dt128, z, conv_w, conv_b, dt_bias_fwd, dt_bias_bwd, a_log_fwd, a_log_bwd,
                d_skip, ssd_norm_w):
    f32 = jnp.float32
    pad = lambda v: jnp.pad(v, (0, DT_PAD - v.shape[0])).reshape(1, DT_PAD)
    dtb = pad(jnp.concatenate([dt_bias_fwd, dt_bias_bwd]))
    a_neg = pad(-jnp.exp(jnp.concatenate([a_log_fwd, a_log_bwd])))
    tri = (jnp.arange(CHUNK)[None, :] <= jnp.arange(CHUNK)[:, None]).astype(f32)
    head_of_lane = jnp.arange(HEAD_W) // SSD_HEADDIM
    y = None
    for direction in range(2):
        e = (jnp.arange(DT_PAD)[:, None] == (direction * SSD_HEADS + head_of_lane)[None, :]).astype(f32)
        consts = (conv_w, conv_b.reshape(1, -1), dtb, a_neg, e, tri)
        if direction == 0:
            extra = (jnp.repeat(d_skip, SSD_HEADDIM).reshape(1, HEAD_W),)
        else:
            extra = (y, z, ssd_norm_w.reshape(1, HEAD_W))
        y = _ssd_pass(direction, xbc, dt128, consts, extra)
    return y


MERGE_TB = 256


def _merge_kernel(x_ref, ys_ref, ym_ref, g_ref, nw_ref, pa_ref, pb_ref, wo_ref, wq_ref,
                  h_ref, un_ref, q_ref):
    bf16 = jnp.bfloat16
    f32 = jnp.float32
    g = jax.nn.sigmoid(g_ref[...])
    ma = jnp.dot(ys_ref[...].astype(bf16), pa_ref[...], preferred_element_type=f32)
    mb = jnp.dot(ym_ref[...].astype(bf16), pb_ref[...], preferred_element_type=f32)
    m = g[:, :D_MODEL] * ma + g[:, D_MODEL:] * mb
    h = x_ref[...] + jnp.dot(m.astype(bf16), wo_ref[...], preferred_element_type=f32)
    h_ref[...] = h
    un = _rms_rows(h, nw_ref[...])
    un_ref[...] = un
    q_ref[...] = jnp.dot(un.astype(bf16), wq_ref[...], preferred_element_type=f32)


def _merge(x2, y_ssd, y_mla, gates, norm_ffn_w, w_proj_ssd, w_proj_mla, w_out, peer_w_q):
    t, d = x2.shape
    bf16 = jnp.bfloat16
    tb = min(MERGE_TB, t)
    nq = peer_w_q.shape[1]
    row = lambda n: pl.BlockSpec((tb, n), lambda i: (i, 0))
    full = lambda a: pl.BlockSpec(a.shape, lambda i: (0, 0))
    ws = [w.astype(bf16) for w in (w_proj_ssd, w_proj_mla, w_out, peer_w_q)]
    nw = norm_ffn_w.reshape(1, d)
    return pl.pallas_call(
        _merge_kernel,
        grid=(t // tb,),
        in_specs=[row(d), row(d), row(d), row(GATE_W), full(nw)] + [full(w) for w in ws],
        out_specs=[row(d), row(d), row(nq)],
        out_shape=[jax.ShapeDtypeStruct((t, d), jnp.float32), jax.ShapeDtypeStruct((t, d), jnp.float32),
                   jax.ShapeDtypeStruct((t, nq), jnp.float32)],
        compiler_params=pltpu.CompilerParams(dimension_semantics=("arbitrary",),
                                             vmem_limit_bytes=56 << 20),
        name="merge",
    )(x2, y_ssd, y_mla, gates, nw, *ws)


MLA_TB = 512
ATT_TQ = 512


def _rms_rows(x, w):
    return x * lax.rsqrt(jnp.mean(x * x, axis=-1, keepdims=True) + EPS) * w


def _mla_proj_kernel(pos_ref, qa_ref, kva_ref, kr_ref, krs_ref, qnw_ref, kvnw_ref, inv_ref,
                     wq_ref, wqs_ref, wk_ref, wv_ref, q_out, k_out, v_out):
    bf16 = jnp.bfloat16
    ang = pos_ref[...].astype(jnp.float32) * inv_ref[...]
    cos, sin = jnp.cos(ang), jnp.sin(ang)
    qa = _rms_rows(qa_ref[...], qnw_ref[...]).astype(bf16)
    q = jnp.dot(qa, wq_ref[...], preferred_element_type=jnp.float32)
    qs = jnp.dot(qa, wqs_ref[...], preferred_element_type=jnp.float32)
    kva = _rms_rows(kva_ref[...], kvnw_ref[...]).astype(bf16)
    k = jnp.dot(kva, wk_ref[...], preferred_element_type=jnp.float32)
    v_out[...] = jnp.dot(kva, wv_ref[...], preferred_element_type=jnp.float32).astype(bf16)
    kro = kr_ref[...] * cos + krs_ref[...] * sin
    c = (QK_DIM ** -0.5) * LOG2E
    for h in range(MLA_HEADS):
        sl = slice(h * HEAD_PAD, (h + 1) * HEAD_PAD)
        q_out[:, sl] = ((q[:, sl] * cos + qs[:, sl] * sin) * c).astype(bf16)
        k_out[:, sl] = (k[:, sl] + kro).astype(bf16)


def _rope_partner(w1, w2):
    return jnp.concatenate([-w2, w1], axis=-1)


def _pad_heads(nope, rope):
    ref = nope if nope is not None else rope
    lead = ref.shape[:-1]
    z = lambda n: jnp.zeros(lead + (n,), ref.dtype)
    parts = [nope if nope is not None else z(QK_NOPE), rope if rope is not None else z(QK_ROPE),
             z(HEAD_PAD - QK_DIM)]
    out = jnp.concatenate(parts, axis=-1)
    return out.reshape(out.shape[:-2] + (out.shape[-2] * HEAD_PAD,))


def _mla_proj(positions, q_a, kv_a, kr, krs, q_a_norm_w, kv_a_norm_w, w_q_b, w_kv_b):
    t = q_a.shape[0]
    bf16 = jnp.bfloat16
    half = QK_ROPE // 2
    wq = w_q_b.reshape(Q_LORA, MLA_HEADS, QK_DIM)
    wq_n, wq_r = wq[..., :QK_NOPE], wq[..., QK_NOPE:]
    wq_main = _pad_heads(wq_n, wq_r).astype(bf16)
    wq_part = _pad_heads(None, _rope_partner(wq_r[..., :half], wq_r[..., half:])).astype(bf16)
    wkv = w_kv_b.reshape(KV_LORA, MLA_HEADS, QK_NOPE + V_HEAD)
    wk = _pad_heads(wkv[..., :QK_NOPE], None).astype(bf16)
    wv = wkv[..., QK_NOPE:].reshape(KV_LORA, MLA_HEADS * V_HEAD).astype(bf16)
    inv = 1.0 / (ROPE_THETA ** (jnp.arange(0, QK_ROPE, 2, dtype=jnp.float32) / QK_ROPE))
    inv_lane = jnp.concatenate([jnp.zeros((QK_NOPE,), jnp.float32), inv, inv,
                                jnp.zeros((HEAD_PAD - QK_DIM,), jnp.float32)]).reshape(1, HEAD_PAD)
    tb = min(MLA_TB, t)
    row = lambda n: pl.BlockSpec((tb, n), lambda i: (i, 0))
    full = lambda a: pl.BlockSpec(a.shape, lambda i: (0,) * a.ndim)
    qn = q_a_norm_w.reshape(1, Q_LORA)
    kvn = kv_a_norm_w.reshape(1, KV_LORA)
    args = (positions.reshape(t, 1), q_a, kv_a, kr, krs, qn, kvn, inv_lane, wq_main, wq_part, wk, wv)
    in_specs = [row(1), row(Q_LORA), row(KV_LORA), row(HEAD_PAD), row(HEAD_PAD)] + [full(a) for a in args[5:]]
    nq = MLA_HEADS * HEAD_PAD
    return pl.pallas_call(
        _mla_proj_kernel,
        grid=(t // tb,),
        in_specs=in_specs,
        out_specs=[row(nq), row(nq), row(MLA_HEADS * V_HEAD)],
        out_shape=[jax.ShapeDtypeStruct((t, nq), bf16), jax.ShapeDtypeStruct((t, nq), bf16),
                   jax.ShapeDtypeStruct((t, MLA_HEADS * V_HEAD), bf16)],
        compiler_params=pltpu.CompilerParams(dimension_semantics=("arbitrary",),
                                             vmem_limit_bytes=48 << 20),
        name="mla_proj",
    )(*args)


def _attn_kernel(q_ref, k_ref, v_ref, o_ref):
    tq = q_ref.shape[1]
    v = v_ref[0]
    outs = []
    for hh in range(2):
        sl = slice(hh * HEAD_PAD, (hh + 1) * HEAD_PAD)
        s = lax.dot_general(q_ref[0, :, sl], k_ref[0, :, sl], (((1,), (1,)), ((), ())),
                            preferred_element_type=jnp.float32)
        m = jnp.max(s, axis=-1, keepdims=True)
        p = jnp.exp2(s - m)
        l = jnp.sum(p, axis=-1, keepdims=True)
        pv = jnp.dot(p.astype(jnp.bfloat16), v, preferred_element_type=jnp.float32)
        outs.append(pv / l)
    lane = lax.broadcasted_iota(jnp.int32, (tq, 2 * V_HEAD), 1)
    o_ref[0] = jnp.where(lane < V_HEAD, outs[0], outs[1]).astype(o_ref.dtype)


def _attention(q, k, v):
    b, s, _ = q.shape
    tq = min(ATT_TQ, s)
    return pl.pallas_call(
        _attn_kernel,
        grid=(b, MLA_HEADS // 2, s // tq),
        in_specs=[pl.BlockSpec((1, tq, 2 * HEAD_PAD), lambda bi, j, i: (bi, i, j)),
                  pl.BlockSpec((1, s, 2 * HEAD_PAD), lambda bi, j, i: (bi, 0, j)),
                  pl.BlockSpec((1, s, 2 * V_HEAD), lambda bi, j, i: (bi, 0, j))],
        out_specs=pl.BlockSpec((1, tq, 2 * V_HEAD), lambda bi, j, i: (bi, i, j)),
        out_shape=jax.ShapeDtypeStruct((b, s, MLA_HEADS * V_HEAD), jnp.bfloat16),
        compiler_params=pltpu.CompilerParams(dimension_semantics=("arbitrary",) * 3,
                                             vmem_limit_bytes=56 << 20),
        name="mla_attention",
    )(q, k, v)


def _mla_pallas(positions, q_a, kv_a, kr, krs, q_a_norm_w, kv_a_norm_w, w_q_b, w_kv_b):
    b, s = positions.shape
    q, k, v = _mla_proj(positions, q_a, kv_a, kr, krs, q_a_norm_w, kv_a_norm_w, w_q_b, w_kv_b)
    y = _attention(q.reshape(b, s, -1), k.reshape(b, s, -1), v.reshape(b, s, -1))
    return y.reshape(b * s, -1)


PEER_TB = 128
PEER_K = PEER_HEADS * PEER_TOPK
HALF_D = D_MODEL // 2
NEG_INF = float("-inf")


def _top16_rows(xs, extras=None):
    n = len(xs)
    rows = xs[0].shape[0]
    iota = lax.broadcasted_iota(jnp.int32, xs[0].shape, 0).astype(jnp.float32)
    xs = list(xs)
    vals, poss, exts = [[] for _ in xs], [[] for _ in xs], [[] for _ in xs]
    for _ in range(PEER_TOPK):
        for j in range(n):
            m = jnp.max(xs[j], axis=0, keepdims=True)
            p = jnp.min(jnp.where(xs[j] == m, iota, float(rows)), axis=0, keepdims=True)
            hit = iota == p
            if extras is not None:
                exts[j].append(jnp.sum(jnp.where(hit, extras[j], 0.0), axis=0, keepdims=True))
            xs[j] = jnp.where(hit, NEG_INF, xs[j])
            vals[j].append(m)
            poss[j].append(p)
    cat = lambda parts: jnp.concatenate(parts, axis=0)
    return [(cat(vals[j]), cat(poss[j]), cat(exts[j]) if extras is not None else None) for j in range(n)]


_CAND_PAIRS = [(a, b) for a in range(PEER_TOPK) for b in range(PEER_TOPK) if (a + 1) * (b + 1) <= PEER_TOPK]
CAND_ROWS = -(-len(_CAND_PAIRS) // 8) * 8
ROW_WORDS = 4
TOPK_LOCKSTEP_HEADS = 4


def _cand_select():
    g0 = [[0.0] * PEER_TOPK for _ in range(CAND_ROWS)]
    g1 = [[0.0] * PEER_TOPK for _ in range(CAND_ROWS)]
    for r, (a, b) in enumerate(_CAND_PAIRS):
        g0[r][a] = 1.0
        g1[r][b] = 1.0
    return jnp.array(g0, jnp.float32), jnp.array(g1, jnp.float32)


def _peer_topk_kernel(q_ref, keys_ref, g0_ref, g1_ref, idx_ref, gate_ref):
    f32 = jnp.float32
    pick = lambda g_ref, v: jnp.dot(g_ref[...], v, precision=lax.Precision.HIGHEST, preferred_element_type=f32)
    pad_row = lax.broadcasted_iota(jnp.int32, (CAND_ROWS, q_ref.shape[0]), 0) >= len(_CAND_PAIRS)
    for h0 in range(0, PEER_HEADS, TOPK_LOCKSTEP_HEADS):
        heads = range(h0, h0 + TOPK_LOCKSTEP_HEADS)
        scores = []
        for h in heads:
            for i in range(2):
                c0 = (h * 2 + i) * KEY_HALF
                scores.append(lax.dot_general(keys_ref[i], q_ref[:, c0:c0 + KEY_HALF], (((1,), (1,)), ((), ())),
                                              preferred_element_type=f32))
        tops = _top16_rows(scores)
        cands, cidxs = [], []
        for j in range(len(heads)):
            (v0, p0, _), (v1, p1, _) = tops[2 * j], tops[2 * j + 1]
            cands.append(jnp.where(pad_row, NEG_INF, pick(g0_ref, v0) + pick(g1_ref, v1)))
            cidxs.append((pick(g0_ref, p0) * N_KEYS + pick(g1_ref, p1)) * ROW_WORDS)
        for h, (best, _, eidx) in zip(heads, _top16_rows(cands, cidxs)):
            e = jnp.exp(best - best[0:1, :])
            gate = e / jnp.sum(e, axis=0, keepdims=True)
            idx_ref[h * PEER_TOPK:(h + 1) * PEER_TOPK, :] = eidx.astype(jnp.int32)
            gate_ref[h * PEER_TOPK:(h + 1) * PEER_TOPK, :] = gate


def _peer_topk(q, sub_keys):
    t = q.shape[0]
    g0, g1 = _cand_select()
    return pl.pallas_call(
        _peer_topk_kernel,
        grid=(t // PEER_TB,),
        in_specs=[pl.BlockSpec((PEER_TB, q.shape[1]), lambda i: (i, 0)),
                  pl.BlockSpec(sub_keys.shape, lambda i: (0, 0, 0)),
                  pl.BlockSpec(g0.shape, lambda i: (0, 0)),
                  pl.BlockSpec(g1.shape, lambda i: (0, 0))],
        out_specs=[pl.BlockSpec((PEER_K, PEER_TB), lambda i: (0, i)),
                   pl.BlockSpec((PEER_K, PEER_TB), lambda i: (0, i))],
        out_shape=[jax.ShapeDtypeStruct((PEER_K, t), jnp.int32),
                   jax.ShapeDtypeStruct((PEER_K, t), jnp.float32)],
        name="peer_topk",
    )(q, sub_keys, g0, g1)


def _pack_table(tab):
    e = tab.shape[0]
    tb = tab.astype(jnp.bfloat16)
    lo = lax.bitcast_convert_type(tb[:, :HALF_D], jnp.uint16).astype(jnp.uint32)
    hi = lax.bitcast_convert_type(tb[:, HALF_D:], jnp.uint16).astype(jnp.uint32)
    return (lo | (hi << 16)).reshape(e * ROW_WORDS, 128)


def _gather_rows(idx_ref, tok, tab_ref, scr):
    nchunk = HALF_D // 128
    for k in range(PEER_K):
        e = idx_ref[tok * PEER_K + k]
        scr[k // 8, pl.ds(k % 8, nchunk, stride=8), :] = tab_ref[pl.ds(pl.multiple_of(e, ROW_WORDS), ROW_WORDS), :]
    gm = jnp.concatenate(
        [scr[:, j * 8:(j + 1) * 8, :].reshape(PEER_K, 128) for j in range(nchunk)], axis=1)
    return pltpu.bitcast(gm, jnp.bfloat16)


def _peer_u_kernel(idx_ref, ut_ref, tab_ref, out_ref, scr0, scr1):
    shp = (8, 2 * PEER_K)
    row = lax.broadcasted_iota(jnp.int32, shp, 0)
    even = (lax.broadcasted_iota(jnp.int32, shp, 1) & 1) == 0

    def group(g, carry):
        base = pl.multiple_of(g * 8, 8)
        ut8 = ut_ref[pl.ds(base, 8), :]
        lhs = jnp.concatenate([ut8[:, :HALF_D], ut8[:, HALF_D:]], axis=0).astype(jnp.bfloat16)
        acc = jnp.zeros(shp, jnp.float32)
        for i in range(8):
            b = _gather_rows(idx_ref, base + i, tab_ref, (scr0, scr1)[i % 2])
            r = lax.dot_general(lhs, b, (((1,), (1,)), ((), ())),
                                preferred_element_type=jnp.float32)
            acc = jnp.where(row == i, jnp.where(even, r[0:8], r[8:16]), acc)
        out_ref[pl.ds(base, 8), :] = acc
        return carry

    lax.fori_loop(0, PEER_TB // 8, group, 0)


def _peer_v_kernel(idx_ref, a_ref, gate_ref, tab_ref, out_ref, scr0, scr1):
    shp = (8, 2 * PEER_K)
    row = lax.broadcasted_iota(jnp.int32, (8, HALF_D), 0)
    even = (lax.broadcasted_iota(jnp.int32, shp, 1) & 1) == 0

    def group(g, carry):
        base = pl.multiple_of(g * 8, 8)
        a8 = a_ref[pl.ds(base, 8), :]
        s = a8 + pltpu.roll(a8, 2 * PEER_K - 1, axis=1)
        act = jnp.where(even, s, pltpu.roll(s, 1, axis=1))
        gelu = 0.5 * act * (1.0 + lax.erf(act * (2.0 ** -0.5)))
        w = gelu * gate_ref[pl.ds(base, 8), :]
        lhs = jnp.concatenate([jnp.where(even, w, 0.0), jnp.where(even, 0.0, w)],
                              axis=0).astype(jnp.bfloat16)
        acc_lo = jnp.zeros((8, HALF_D), jnp.float32)
        acc_hi = jnp.zeros((8, HALF_D), jnp.float32)
        for i in range(8):
            b = _gather_rows(idx_ref, base + i, tab_ref, (scr0, scr1)[i % 2])
            r = jnp.dot(lhs, b, preferred_element_type=jnp.float32)
            acc_lo = jnp.where(row == i, r[0:8], acc_lo)
            acc_hi = jnp.where(row == i, r[8:16], acc_hi)
        out_ref[pl.ds(base, 8), 0:HALF_D] = acc_lo
        out_ref[pl.ds(base, 8), HALF_D:] = acc_hi
        return carry

    lax.fori_loop(0, PEER_TB // 8, group, 0)


def _peer_scratch():
    return [pltpu.VMEM((PEER_K // 8, 8 * (HALF_D // 128), 128), jnp.uint32) for _ in range(2)]


def _peer_params(tab):
    table_bytes = tab.size * 4
    return pltpu.CompilerParams(dimension_semantics=("arbitrary",),
                                vmem_limit_bytes=table_bytes + (16 << 20))


def _peer_u_pass(idx_flat, un, tab):
    t = un.shape[0]
    return pl.pallas_call(
        _peer_u_kernel,
        grid=(t // PEER_TB,),
        in_specs=[pl.BlockSpec((PEER_TB * PEER_K,), lambda i: (i,), memory_space=pltpu.SMEM),
                  pl.BlockSpec((PEER_TB, D_MODEL), lambda i: (i, 0)),
                  pl.BlockSpec(memory_space=pltpu.VMEM)],
        out_specs=pl.BlockSpec((PEER_TB, 2 * PEER_K), lambda i: (i, 0)),
        out_shape=jax.ShapeDtypeStruct((t, 2 * PEER_K), jnp.float32),
        scratch_shapes=_peer_scratch(),
        compiler_params=_peer_params(tab),
        name="peer_u_pass",
    )(idx_flat, un, tab)


def _peer_v_pass(idx_flat, a, gate2, tab):
    t = a.shape[0]
    return pl.pallas_call(
        _peer_v_kernel,
        grid=(t // PEER_TB,),
        in_specs=[pl.BlockSpec((PEER_TB * PEER_K,), lambda i: (i,), memory_space=pltpu.SMEM),
                  pl.BlockSpec((PEER_TB, 2 * PEER_K), lambda i: (i, 0)),
                  pl.BlockSpec((PEER_TB, 2 * PEER_K), lambda i: (i, 0)),
                  pl.BlockSpec(memory_space=pltpu.VMEM)],
        out_specs=pl.BlockSpec((PEER_TB, D_MODEL), lambda i: (i, 0)),
        out_shape=jax.ShapeDtypeStruct((t, D_MODEL), jnp.float32),
        scratch_shapes=_peer_scratch(),
        compiler_params=_peer_params(tab),
        name="peer_v_pass",
    )(idx_flat, a, gate2, tab)


def _peer_pallas(un, q, sub_keys, peer_u, peer_v):
    idx_t, gate_t = _peer_topk(q, sub_keys)
    idx_flat = idx_t.T.reshape(-1)
    gate2 = jnp.repeat(gate_t.T, 2, axis=1)
    a = _peer_u_pass(idx_flat, un, _pack_table(peer_u))
    return _peer_v_pass(idx_flat, a, gate2, _pack_table(peer_v))


def dev_stages():
    s = jax.ShapeDtypeStruct
    f32, i32, u32 = jnp.float32, jnp.int32, jnp.uint32
    t = 32768
    bf16 = jnp.bfloat16
    return {
        "mlaproj": (_mla_proj, (s((8, 4096), i32), s((t, 384), f32), s((t, 256), f32), s((t, 128), f32),
                                s((t, 128), f32), s((384,), f32), s((256,), f32), s((384, 1536), f32),
                                s((256, 2048), f32))),
        "ssd": (_ssd_pallas, (s((8, 4096, 1536), f32), s((8, 4096, 128), f32), s((8, 4096, 1024), f32),
                              s((3, 1536), f32), s((1536,), f32), s((16,), f32), s((16,), f32), s((16,), f32),
                              s((16,), f32), s((16,), f32), s((1024,), f32))),
        "inproj": (_in_proj, (s((t, 1024), f32), s((1024,), f32), s((1024, 5312), f32))),
        "merge": (_merge, (s((t, 1024), f32), s((t, 1024), f32), s((t, 1024), bf16), s((t, 2048), f32),
                           s((1024,), f32), s((1024, 1024), f32), s((1024, 1024), f32),
                           s((1024, 1024), f32), s((1024, 2048), f32))),
        "attn": (_attention, (s((8, 4096, 2048), bf16), s((8, 4096, 2048), bf16), s((8, 4096, 1024), bf16))),
        "topk": (_peer_topk, (s((t, 2048), f32), s((2, 128, 128), f32))),
        "upass": (_peer_u_pass, (s((t * 128,), i32), s((t, 1024), f32), s((65536, 128), u32))),
        "vpass": (_peer_v_pass, (s((t * 128,), i32), s((t, 256), f32), s((t, 256), f32), s((65536, 128), u32))),
    }


def _final_norm_kernel(h_ref, p_ref, w_ref, o_ref):
    hh = h_ref[...] + p_ref[...]
    ms = jnp.mean(hh * hh, axis=-1, keepdims=True)
    o_ref[...] = hh * lax.rsqrt(ms + EPS) * w_ref[...]


def _final_norm(h, p, w):
    t, d = h.shape
    tb = 512
    return pl.pallas_call(
        _final_norm_kernel,
        grid=(t // tb,),
        in_specs=[pl.BlockSpec((tb, d), lambda i: (i, 0)),
                  pl.BlockSpec((tb, d), lambda i: (i, 0)),
                  pl.BlockSpec((1, d), lambda i: (0, 0))],
        out_specs=pl.BlockSpec((tb, d), lambda i: (i, 0)),
        out_shape=jax.ShapeDtypeStruct((t, d), jnp.float32),
    )(h, p, w.reshape(1, d))


def kernel(x, positions, norm_mix_w, w_in, conv_w, conv_b, dt_bias_fwd, dt_bias_bwd, a_log_fwd, a_log_bwd, d_skip, ssd_norm_w, q_a_norm_w, w_q_b, kv_a_norm_w, w_kv_b, w_proj_ssd, w_proj_mla, w_out, norm_ffn_w, peer_w_q, peer_sub_keys, peer_u, peer_v, norm_final_w):
    b, s, _ = x.shape
    x2 = x.reshape(b * s, D_MODEL)
    z, xbc, dt128, q_a, kv_a, kr, krs, gate_logits = _in_proj(x2, norm_mix_w, w_in)
    y_ssd = _ssd_pallas(xbc.reshape(b, s, -1), dt128.reshape(b, s, -1), z.reshape(b, s, -1), conv_w, conv_b,
                        dt_bias_fwd, dt_bias_bwd, a_log_fwd, a_log_bwd, d_skip, ssd_norm_w)

    y_mla = _mla_pallas(positions, q_a, kv_a, kr, krs, q_a_norm_w, kv_a_norm_w, w_q_b, w_kv_b)

    h2, un, q_peer = _merge(x2, y_ssd.reshape(b * s, -1), y_mla, gate_logits, norm_ffn_w,
                            w_proj_ssd, w_proj_mla, w_out, peer_w_q)
    pf = _peer_pallas(un, q_peer, peer_sub_keys, peer_u, peer_v)
    out = _final_norm(h2, pf, norm_final_w)
    return out.reshape(b, s, D_MODEL)
```

```python
import functools
import math

import jax
import jax.numpy as jnp
from jax import lax
from jax.experimental import pallas as pl
from jax.experimental.pallas import tpu as pltpu

D_MODEL = 1024
SSD_D_INNER = 1024
SSD_HEADDIM = 64
SSD_HEADS = 16
SSD_GROUPS = 2
SSD_HPG = 8
SSD_STATE = 128
CONV_WIDTH = 3
CHUNK = 128
SSD_CONV_DIM = SSD_D_INNER + 2 * SSD_GROUPS * SSD_STATE
MLA_HEADS = 16
QK_NOPE = 64
QK_ROPE = 32
V_HEAD = 64
Q_LORA = 384
KV_LORA = 256
Q_BLOCK = 128
ROPE_THETA = 10000.0
PEER_HEADS = 8
N_KEYS = 128
KEY_HALF = 128
PEER_TOPK = 16
TOK_BLOCK = 128
EPS = 1e-6
HEAD_PAD = 128
QK_DIM = QK_NOPE + QK_ROPE
LOG2E = 1.4426950408889634
IN_SIZES = (SSD_D_INNER, SSD_CONV_DIM, 2 * SSD_HEADS, Q_LORA, KV_LORA, QK_ROPE, 2 * D_MODEL)


def _split_points(sizes):
    pts, acc = [], 0
    for s in sizes[:-1]:
        acc += s
        pts.append(acc)
    return pts


def _rms(x, w):
    y = x * lax.rsqrt(jnp.mean(x * x, axis=-1, keepdims=True) + EPS)
    return y * w


def _segsum(a):
    t = a.shape[-1]
    cs = jnp.cumsum(a, axis=-1)
    diff = cs[..., :, None] - cs[..., None, :]
    mask = jnp.tril(jnp.ones((t, t), dtype=bool))
    return jnp.where(mask, diff, -jnp.inf)


def _ssd_scan(xh, dt, A, Bm, Cm):
    b, l, g, r, p = xh.shape
    n = Bm.shape[-1]
    c = l // CHUNK
    xd = (xh * dt[..., None]).reshape(b, c, CHUNK, g, r, p)
    a = (dt * A).reshape(b, c, CHUNK, g, r)
    a = jnp.moveaxis(a, 2, -1)
    Bc = Bm.reshape(b, c, CHUNK, g, n)
    Cc = Cm.reshape(b, c, CHUNK, g, n)
    a_cs = jnp.cumsum(a, axis=-1)
    L = jnp.exp(_segsum(a))
    CB = jnp.einsum('bclgn,bcsgn->bcgls', Cc, Bc)
    y_diag = jnp.einsum('bcgls,bcgrls,bcsgrp->bclgrp', CB, L, xd)
    decay_states = jnp.exp(a_cs[..., -1:] - a_cs)
    states = jnp.einsum('bcsgn,bcgrs,bcsgrp->bcgrpn', Bc, decay_states, xd)
    chunk_tot = jnp.moveaxis(a_cs[..., -1], 1, -1)
    decay_chunk = jnp.exp(_segsum(jnp.pad(chunk_tot, ((0, 0), (0, 0), (0, 0), (1, 0)))))
    states0 = jnp.pad(states, ((0, 0), (1, 0), (0, 0), (0, 0), (0, 0), (0, 0)))
    new_states = jnp.einsum('bgrzc,bcgrpn->bzgrpn', decay_chunk, states0)
    prev = new_states[:, :-1]
    y_off = jnp.einsum('bclgn,bcgrpn,bcgrl->bclgrp', Cc, prev, jnp.exp(a_cs))
    return (y_diag + y_off).reshape(b, l, g, r, p)


def _rope_tables(positions, dim):
    inv = 1.0 / (ROPE_THETA ** (jnp.arange(0, dim, 2, dtype=jnp.float32) / dim))
    ang = positions.astype(jnp.float32)[..., None] * inv
    return jnp.cos(ang), jnp.sin(ang)


def _apply_rope(x, cos, sin):
    x1, x2 = jnp.split(x, 2, axis=-1)
    return jnp.concatenate([x1 * cos - x2 * sin, x1 * sin + x2 * cos], axis=-1)


def _mla_attention(q_nope, q_rope, k_nope, k_rope, v):
    b, s, h, _ = q_nope.shape
    nb = s // Q_BLOCK
    scale = (QK_NOPE + QK_ROPE) ** -0.5

    def block(args):
        qn, qr = args
        sc = (jnp.einsum('bqhd,bkhd->bhqk', qn, k_nope)
              + jnp.einsum('bqhd,bkd->bhqk', qr, k_rope))
        p = jax.nn.softmax(sc * scale, axis=-1)
        return jnp.einsum('bhqk,bkhd->bqhd', p, v)

    qn_b = q_nope.reshape(b, nb, Q_BLOCK, h, QK_NOPE).swapaxes(0, 1)
    qr_b = q_rope.reshape(b, nb, Q_BLOCK, h, QK_ROPE).swapaxes(0, 1)
    out = lax.map(block, (qn_b, qr_b))
    return out.swapaxes(0, 1).reshape(b, s, h * V_HEAD)


def _peer_ffn(u, w_q, sub_keys, peer_u, peer_v):
    b, s, d = u.shape
    n_exp = peer_u.shape[0]
    ub = u.reshape((b * s) // TOK_BLOCK, TOK_BLOCK, d)

    def block(ut):
        q = (ut @ w_q).reshape(TOK_BLOCK, PEER_HEADS, 2, KEY_HALF)
        sc = jnp.einsum('thid,ind->thin', q, sub_keys)
        v_top, i_top = lax.top_k(sc, PEER_TOPK)
        cand = v_top[:, :, 0, :, None] + v_top[:, :, 1, None, :]
        cand_idx = i_top[:, :, 0, :, None] * N_KEYS + i_top[:, :, 1, None, :]
        cand = cand.reshape(TOK_BLOCK, PEER_HEADS, PEER_TOPK * PEER_TOPK)
        cand_idx = cand_idx.reshape(TOK_BLOCK, PEER_HEADS, PEER_TOPK * PEER_TOPK)
        best, pos = lax.top_k(cand, PEER_TOPK)
        idx = jnp.take_along_axis(cand_idx, pos, axis=-1)
        gate = jax.nn.softmax(best, axis=-1)
        act = jax.nn.gelu(jnp.einsum('td,thkd->thk', ut, peer_u[idx]), approximate=False)
        return jnp.einsum('thk,thkd->td', gate * act, peer_v[idx])

    return lax.map(block, ub).reshape(b, s, d)


IN_TB = 256
DT_PAD = 128
GATE_W = 2 * D_MODEL
IN_OUT_WIDTHS = (SSD_D_INNER, SSD_CONV_DIM, DT_PAD, Q_LORA, KV_LORA, HEAD_PAD, HEAD_PAD, GATE_W)


def _in_proj_kernel(x_ref, nw_ref, w_ref, *out_refs):
    u = _rms_rows(x_ref[...], nw_ref[...]).astype(jnp.bfloat16)
    col = 0
    for o_ref, width in zip(out_refs, IN_OUT_WIDTHS):
        o_ref[...] = jnp.dot(u, w_ref[:, col:col + width], preferred_element_type=jnp.float32)
        col += width


def _arrange_w_in(w_in):
    z, xbc, dt, qa, kva, kr, gates = jnp.split(w_in, _split_points(IN_SIZES), axis=1)
    d = w_in.shape[0]
    half = QK_ROPE // 2
    dt_p = jnp.pad(dt, ((0, 0), (0, DT_PAD - dt.shape[1])))
    kr3 = kr.reshape(d, 1, QK_ROPE)
    kr_p = _pad_heads(None, kr3)
    krs_p = _pad_heads(None, _rope_partner(kr3[..., :half], kr3[..., half:]))
    return jnp.concatenate([z, xbc, dt_p, qa, kva, kr_p, krs_p, gates], axis=1).astype(jnp.bfloat16)


def _in_proj(x2, norm_w, w_in):
    t, d = x2.shape
    w = _arrange_w_in(w_in)
    tb = min(IN_TB, t)
    return pl.pallas_call(
        _in_proj_kernel,
        grid=(t // tb,),
        in_specs=[pl.BlockSpec((tb, d), lambda i: (i, 0)),
                  pl.BlockSpec((1, d), lambda i: (0, 0)),
                  pl.BlockSpec(w.shape, lambda i: (0, 0))],
        out_specs=[pl.BlockSpec((tb, n), lambda i: (i, 0)) for n in IN_OUT_WIDTHS],
        out_shape=[jax.ShapeDtypeStruct((t, n), jnp.float32) for n in IN_OUT_WIDTHS],
        compiler_params=pltpu.CompilerParams(dimension_semantics=("arbitrary",),
                                             vmem_limit_bytes=56 << 20),
        name="in_proj",
    )(x2, norm_w.reshape(1, d), w)


HEAD_W = SSD_HEADS * SSD_HEADDIM
GROUP_W = HEAD_W // SSD_GROUPS
BC_W = SSD_GROUPS * SSD_STATE
_HI = lax.Precision.HIGHEST


def _ssd_kernel(direction, xc_ref, xp_ref, xn_ref, dt_ref, cw_ref, cb_ref, dtb_ref, a_ref, e_ref,
                tri_ref, *rest):
    f32, bf16 = jnp.float32, jnp.bfloat16
    fwd = direction == 0
    if fwd:
        dsk_ref, y_ref, h_scr = rest
    else:
        y1_ref, z_ref, nw_ref, y_ref, h_scr = rest
    c = pl.program_id(1)
    nc = pl.num_programs(1)
    cm = c if fwd else nc - 1 - c

    @pl.when(c == 0)
    def _():
        h_scr[...] = jnp.zeros_like(h_scr)

    xc = xc_ref[0]
    row = lax.broadcasted_iota(jnp.int32, xc.shape, 0)
    prev_row = jnp.where(cm > 0, xp_ref[0, 7:8, :], 0.0)
    next_row = jnp.where(cm < nc - 1, xn_ref[0, 0:1, :], 0.0)
    x_m1 = jnp.where(row == 0, prev_row, pltpu.roll(xc, 1, axis=0))
    x_p1 = jnp.where(row == CHUNK - 1, next_row, pltpu.roll(xc, CHUNK - 1, axis=0))
    conv = cw_ref[0:1, :] * x_m1 + cw_ref[1:2, :] * xc + cw_ref[2:3, :] * x_p1 + cb_ref[...]
    act = conv * jax.nn.sigmoid(conv)
    xs = act[:, :HEAD_W]

    dtr = dt_ref[0] + dtb_ref[...]
    dt = jnp.maximum(dtr, 0.0) + jnp.log(1.0 + jnp.exp(-jnp.abs(dtr)))
    a = dt * a_ref[...]
    cs = jnp.dot(tri_ref[...], a, precision=_HI, preferred_element_type=f32)
    expand = lambda v: jnp.dot(v, e_ref[...], precision=_HI, preferred_element_type=f32)
    cs_e = expand(cs)
    tot_e = cs_e[CHUNK - 1:CHUNK, :]
    if fwd:
        pos = cs
        w_state = jnp.exp(tot_e - cs_e)
        out_scale = jnp.exp(cs_e)
    else:
        pos = cs - a
        e_e = cs_e - expand(a)
        w_state = jnp.exp(e_e)
        out_scale = jnp.exp(tot_e - e_e)
    pos_t = pos.T
    xd = xs * expand(dt)
    xdb = xd.astype(bf16)
    xw = (xd * w_state).astype(bf16)
    chunk_decay = jnp.exp(tot_e)

    li = lax.broadcasted_iota(jnp.int32, (CHUNK, CHUNK), 0)
    si = lax.broadcasted_iota(jnp.int32, (CHUNK, CHUNK), 1)
    mask = (si <= li) if fwd else (si >= li)
    for g in range(SSD_GROUPS):
        gsl = slice(g * GROUP_W, (g + 1) * GROUP_W)
        bg = act[:, HEAD_W + g * SSD_STATE:HEAD_W + (g + 1) * SSD_STATE]
        cg = act[:, HEAD_W + BC_W + g * SSD_STATE:HEAD_W + BC_W + (g + 1) * SSD_STATE].astype(bf16)
        cb = lax.dot_general(cg, bg.astype(bf16), (((1,), (1,)), ((), ())), preferred_element_type=f32)
        h_in = h_scr[:, gsl]
        y_off = jnp.dot(cg, h_in.astype(bf16), preferred_element_type=f32) * out_scale[:, gsl]
        s_new = jnp.dot(bg.T.astype(bf16), xw[:, gsl], preferred_element_type=f32)
        h_scr[:, gsl] = h_in * chunk_decay[:, gsl] + s_new
        for r in range(SSD_HPG):
            hd = g * SSD_HPG + r
            lane = direction * SSD_HEADS + hd
            col = pos[:, lane:lane + 1]
            rw = pos_t[lane:lane + 1, :]
            diff = (col - rw) if fwd else (rw - col)
            m = (cb * jnp.exp(jnp.where(mask, diff, NEG_INF))).astype(bf16)
            hsl = slice(hd * SSD_HEADDIM, (hd + 1) * SSD_HEADDIM)
            y_d = jnp.dot(m, xdb[:, hsl], preferred_element_type=f32)
            y_ref[0, :, hsl] = y_d + y_off[:, r * SSD_HEADDIM:(r + 1) * SSD_HEADDIM]

    if fwd:
        y_ref[0] = y_ref[0] + xs * dsk_ref[...]
    else:
        yg = (y_ref[0] + y1_ref[0]) * (z_ref[0] * jax.nn.sigmoid(z_ref[0]))
        for g in range(SSD_GROUPS):
            gsl = slice(g * GROUP_W, (g + 1) * GROUP_W)
            y_ref[0, :, gsl] = _rms_rows(yg[:, gsl], nw_ref[:, gsl])


def _ssd_pass(direction, xbc, dt128, consts, extra):
    b, s, cw_ = xbc.shape
    nc = s // CHUNK
    per8 = CHUNK // 8
    cm = (lambda c: c) if direction == 0 else (lambda c: nc - 1 - c)
    cur = lambda n: pl.BlockSpec((1, CHUNK, n), lambda bi, c: (bi, cm(c), 0))
    prev = pl.BlockSpec((1, 8, cw_), lambda bi, c: (bi, jnp.maximum(cm(c) * per8 - 1, 0), 0))
    nxt = pl.BlockSpec((1, 8, cw_), lambda bi, c: (bi, jnp.minimum((cm(c) + 1) * per8, s // 8 - 1), 0))
    full = lambda a: pl.BlockSpec(a.shape, lambda bi, c: (0,) * a.ndim)
    if direction == 0:
        extra_specs = [full(extra[0])]
    else:
        extra_specs = [cur(HEAD_W), cur(HEAD_W), full(extra[2])]
    return pl.pallas_call(
        functools.partial(_ssd_kernel, direction),
        grid=(b, nc),
        in_specs=[cur(cw_), prev, nxt, cur(DT_PAD)] + [full(a) for a in consts] + extra_specs,
        out_specs=cur(HEAD_W),
        out_shape=jax.ShapeDtypeStruct((b, s, HEAD_W), jnp.float32),
        scratch_shapes=[pltpu.VMEM((SSD_STATE, HEAD_W), jnp.float32)],
        compiler_params=pltpu.CompilerParams(dimension_semantics=("arbitrary", "arbitrary"),
                                             vmem_limit_bytes=48 << 20),
        name="ssd_fwd" if direction == 0 else "ssd_bwd",
    )(xbc, xbc, xbc, dt128, *consts, *extra)


def _ssd_pallas(xbc, dt128, z, conv_w, conv_b, dt_bias_fwd, dt_bias_bwd, a_log_fwd, a_log_bwd,
                d_skip, ssd_norm_w):
    f32 = jnp.float32
    pad = lambda v: jnp.pad(v, (0, DT_PAD - v.shape[0])).reshape(1, DT_PAD)
    dtb = pad(jnp.concatenate([dt_bias_fwd, dt_bias_bwd]))
    a_neg = pad(-jnp.exp(jnp.concatenate([a_log_fwd, a_log_bwd])))
    tri = (jnp.arange(CHUNK)[None, :] <= jnp.arange(CHUNK)[:, None]).astype(f32)
    head_of_lane = jnp.arange(HEAD_W) // SSD_HEADDIM
    y = None
    for direction in range(2):
        e = (jnp.arange(DT_PAD)[:, None] == (direction * SSD_HEADS + head_of_lane)[None, :]).astype(f32)
        consts = (conv_w, conv_b.reshape(1, -1), dtb, a_neg, e, tri)
        if direction == 0:
            extra = (jnp.repeat(d_skip, SSD_HEADDIM).reshape(1, HEAD_W),)
        else:
            extra = (y, z, ssd_norm_w.reshape(1, HEAD_W))
        y = _ssd_pass(direction, xbc, dt128, consts, extra)
    return y


MERGE_TB = 256


def _merge_kernel(x_ref, ys_ref, ym_ref, g_ref, nw_ref, pa_ref, pb_ref, wo_ref, wq_ref,
                  h_ref, un_ref, q_ref):
    bf16 = jnp.bfloat16
    f32 = jnp.float32
    g = jax.nn.sigmoid(g_ref[...])
    ma = jnp.dot(ys_ref[...].astype(bf16), pa_ref[...], preferred_element_type=f32)
    mb = jnp.dot(ym_ref[...].astype(bf16), pb_ref[...], preferred_element_type=f32)
    m = g[:, :D_MODEL] * ma + g[:, D_MODEL:] * mb
    h = x_ref[...] + jnp.dot(m.astype(bf16), wo_ref[...], preferred_element_type=f32)
    h_ref[...] = h
    un = _rms_rows(h, nw_ref[...])
    un_ref[...] = un
    q_ref[...] = jnp.dot(un.astype(bf16), wq_ref[...], preferred_element_type=f32)


def _merge(x2, y_ssd, y_mla, gates, norm_ffn_w, w_proj_ssd, w_proj_mla, w_out, peer_w_q):
    t, d = x2.shape
    bf16 = jnp.bfloat16
    tb = min(MERGE_TB, t)
    nq = peer_w_q.shape[1]
    row = lambda n: pl.BlockSpec((tb, n), lambda i: (i, 0))
    full = lambda a: pl.BlockSpec(a.shape, lambda i: (0, 0))
    ws = [w.astype(bf16) for w in (w_proj_ssd, w_proj_mla, w_out, peer_w_q)]
    nw = norm_ffn_w.reshape(1, d)
    return pl.pallas_call(
        _merge_kernel,
        grid=(t // tb,),
        in_specs=[row(d), row(d), row(d), row(GATE_W), full(nw)] + [full(w) for w in ws],
        out_specs=[row(d), row(d), row(nq)],
        out_shape=[jax.ShapeDtypeStruct((t, d), jnp.float32), jax.ShapeDtypeStruct((t, d), jnp.float32),
                   jax.ShapeDtypeStruct((t, nq), jnp.float32)],
        compiler_params=pltpu.CompilerParams(dimension_semantics=("arbitrary",),
                                             vmem_limit_bytes=56 << 20),
        name="merge",
    )(x2, y_ssd, y_mla, gates, nw, *ws)


MLA_TB = 512
ATT_TQ = 256
ATT_HEADS = 4


def _rms_rows(x, w):
    return x * lax.rsqrt(jnp.mean(x * x, axis=-1, keepdims=True) + EPS) * w


def _mla_proj_kernel(pos_ref, qa_ref, kva_ref, kr_ref, krs_ref, qnw_ref, kvnw_ref, inv_ref,
                     wq_ref, wqs_ref, wk_ref, wv_ref, q_out, k_out, v_out):
    bf16 = jnp.bfloat16
    ang = pos_ref[...].astype(jnp.float32) * inv_ref[...]
    cos, sin = jnp.cos(ang), jnp.sin(ang)
    qa = _rms_rows(qa_ref[...], qnw_ref[...]).astype(bf16)
    q = jnp.dot(qa, wq_ref[...], preferred_element_type=jnp.float32)
    qs = jnp.dot(qa, wqs_ref[...], preferred_element_type=jnp.float32)
    kva = _rms_rows(kva_ref[...], kvnw_ref[...]).astype(bf16)
    k = jnp.dot(kva, wk_ref[...], preferred_element_type=jnp.float32)
    v_out[...] = jnp.dot(kva, wv_ref[...], preferred_element_type=jnp.float32).astype(bf16)
    kro = kr_ref[...] * cos + krs_ref[...] * sin
    c = (QK_DIM ** -0.5) * LOG2E
    for h in range(MLA_HEADS):
        sl = slice(h * HEAD_PAD, (h + 1) * HEAD_PAD)
        q_out[:, sl] = ((q[:, sl] * cos + qs[:, sl] * sin) * c).astype(bf16)
        k_out[:, sl] = (k[:, sl] + kro).astype(bf16)


def _rope_partner(w1, w2):
    return jnp.concatenate([-w2, w1], axis=-1)


def _pad_heads(nope, rope):
    ref = nope if nope is not None else rope
    lead = ref.shape[:-1]
    z = lambda n: jnp.zeros(lead + (n,), ref.dtype)
    parts = [nope if nope is not None else z(QK_NOPE), rope if rope is not None else z(QK_ROPE),
             z(HEAD_PAD - QK_DIM)]
    out = jnp.concatenate(parts, axis=-1)
    return out.reshape(out.shape[:-2] + (out.shape[-2] * HEAD_PAD,))


def _mla_proj(positions, q_a, kv_a, kr, krs, q_a_norm_w, kv_a_norm_w, w_q_b, w_kv_b):
    t = q_a.shape[0]
    bf16 = jnp.bfloat16
    half = QK_ROPE // 2
    wq = w_q_b.reshape(Q_LORA, MLA_HEADS, QK_DIM)
    wq_n, wq_r = wq[..., :QK_NOPE], wq[..., QK_NOPE:]
    wq_main = _pad_heads(wq_n, wq_r).astype(bf16)
    wq_part = _pad_heads(None, _rope_partner(wq_r[..., :half], wq_r[..., half:])).astype(bf16)
    wkv = w_kv_b.reshape(KV_LORA, MLA_HEADS, QK_NOPE + V_HEAD)
    wk = _pad_heads(wkv[..., :QK_NOPE], None).astype(bf16)
    wv = wkv[..., QK_NOPE:].reshape(KV_LORA, MLA_HEADS * V_HEAD).astype(bf16)
    inv = 1.0 / (ROPE_THETA ** (jnp.arange(0, QK_ROPE, 2, dtype=jnp.float32) / QK_ROPE))
    inv_lane = jnp.concatenate([jnp.zeros((QK_NOPE,), jnp.float32), inv, inv,
                                jnp.zeros((HEAD_PAD - QK_DIM,), jnp.float32)]).reshape(1, HEAD_PAD)
    tb = min(MLA_TB, t)
    row = lambda n: pl.BlockSpec((tb, n), lambda i: (i, 0))
    full = lambda a: pl.BlockSpec(a.shape, lambda i: (0,) * a.ndim)
    qn = q_a_norm_w.reshape(1, Q_LORA)
    kvn = kv_a_norm_w.reshape(1, KV_LORA)
    args = (positions.reshape(t, 1), q_a, kv_a, kr, krs, qn, kvn, inv_lane, wq_main, wq_part, wk, wv)
    in_specs = [row(1), row(Q_LORA), row(KV_LORA), row(HEAD_PAD), row(HEAD_PAD)] + [full(a) for a in args[5:]]
    nq = MLA_HEADS * HEAD_PAD
    return pl.pallas_call(
        _mla_proj_kernel,
        grid=(t // tb,),
        in_specs=in_specs,
        out_specs=[row(nq), row(nq), row(MLA_HEADS * V_HEAD)],
        out_shape=[jax.ShapeDtypeStruct((t, nq), bf16), jax.ShapeDtypeStruct((t, nq), bf16),
                   jax.ShapeDtypeStruct((t, MLA_HEADS * V_HEAD), bf16)],
        compiler_params=pltpu.CompilerParams(dimension_semantics=("arbitrary",),
                                             vmem_limit_bytes=48 << 20),
        name="mla_proj",
    )(*args)


def _attn_kernel(q_ref, k_ref, v_ref, o_ref):
    tq = q_ref.shape[1]
    lane = lax.broadcasted_iota(jnp.int32, (tq, 2 * V_HEAD), 1)
    for pair in range(ATT_HEADS // 2):
        vsl = slice(pair * 2 * V_HEAD, (pair + 1) * 2 * V_HEAD)
        v = v_ref[0, :, vsl]
        outs = []
        for hh in range(2 * pair, 2 * pair + 2):
            sl = slice(hh * HEAD_PAD, (hh + 1) * HEAD_PAD)
            s = lax.dot_general(q_ref[0, :, sl], k_ref[0, :, sl], (((1,), (1,)), ((), ())),
                                preferred_element_type=jnp.float32)
            m = jnp.max(s, axis=-1, keepdims=True)
            p = jnp.exp2(s - m)
            l = jnp.sum(p, axis=-1, keepdims=True)
            pv = jnp.dot(p.astype(jnp.bfloat16), v, preferred_element_type=jnp.float32)
            outs.append(pv / l)
        o_ref[0, :, vsl] = jnp.where(lane < V_HEAD, outs[0], outs[1]).astype(o_ref.dtype)


def _attention(q, k, v):
    b, s, _ = q.shape
    tq = min(ATT_TQ, s)
    qw, vw = ATT_HEADS * HEAD_PAD, ATT_HEADS * V_HEAD
    return pl.pallas_call(
        _attn_kernel,
        grid=(b, MLA_HEADS // ATT_HEADS, s // tq),
        in_specs=[pl.BlockSpec((1, tq, qw), lambda bi, j, i: (bi, i, j)),
                  pl.BlockSpec((1, s, qw), lambda bi, j, i: (bi, 0, j)),
                  pl.BlockSpec((1, s, vw), lambda bi, j, i: (bi, 0, j))],
        out_specs=pl.BlockSpec((1, tq, vw), lambda bi, j, i: (bi, i, j)),
        out_shape=jax.ShapeDtypeStruct((b, s, MLA_HEADS * V_HEAD), jnp.bfloat16),
        compiler_params=pltpu.CompilerParams(dimension_semantics=("arbitrary",) * 3,
                                             vmem_limit_bytes=56 << 20),
        name="mla_attention",
    )(q, k, v)


def _mla_pallas(positions, q_a, kv_a, kr, krs, q_a_norm_w, kv_a_norm_w, w_q_b, w_kv_b):
    b, s = positions.shape
    q, k, v = _mla_proj(positions, q_a, kv_a, kr, krs, q_a_norm_w, kv_a_norm_w, w_q_b, w_kv_b)
    y = _attention(q.reshape(b, s, -1), k.reshape(b, s, -1), v.reshape(b, s, -1))
    return y.reshape(b * s, -1)


PEER_TB = 128
PEER_K = PEER_HEADS * PEER_TOPK
PEER_GROUP = 32
HALF_D = D_MODEL // 2
NEG_INF = float("-inf")


def _top16_rows(xs, extras=None):
    n = len(xs)
    rows = xs[0].shape[0]
    iota = lax.broadcasted_iota(jnp.int32, xs[0].shape, 0).astype(jnp.float32)
    xs = list(xs)
    vals, poss, exts = [[] for _ in xs], [[] for _ in xs], [[] for _ in xs]
    for _ in range(PEER_TOPK):
        for j in range(n):
            m = jnp.max(xs[j], axis=0, keepdims=True)
            p = jnp.min(jnp.where(xs[j] == m, iota, float(rows)), axis=0, keepdims=True)
            hit = iota == p
            if extras is not None:
                exts[j].append(jnp.sum(jnp.where(hit, extras[j], 0.0), axis=0, keepdims=True))
            xs[j] = jnp.where(hit, NEG_INF, xs[j])
            vals[j].append(m)
            poss[j].append(p)
    cat = lambda parts: jnp.concatenate(parts, axis=0)
    return [(cat(vals[j]), cat(poss[j]), cat(exts[j]) if extras is not None else None) for j in range(n)]


_CAND_PAIRS = [(a, b) for a in range(PEER_TOPK) for b in range(PEER_TOPK) if (a + 1) * (b + 1) <= PEER_TOPK]
CAND_ROWS = -(-len(_CAND_PAIRS) // 8) * 8
ROW_WORDS = 4
TOPK_LOCKSTEP_HEADS = 4


def _cand_select():
    g0 = [[0.0] * PEER_TOPK for _ in range(CAND_ROWS)]
    g1 = [[0.0] * PEER_TOPK for _ in range(CAND_ROWS)]
    for r, (a, b) in enumerate(_CAND_PAIRS):
        g0[r][a] = 1.0
        g1[r][b] = 1.0
    return jnp.array(g0, jnp.float32), jnp.array(g1, jnp.float32)


def _peer_topk_kernel(q_ref, keys_ref, g0_ref, g1_ref, idx_ref, gate_ref):
    f32 = jnp.float32
    pick = lambda g_ref, v: jnp.dot(g_ref[...], v, precision=lax.Precision.HIGHEST, preferred_element_type=f32)
    pad_row = lax.broadcasted_iota(jnp.int32, (CAND_ROWS, q_ref.shape[0]), 0) >= len(_CAND_PAIRS)
    for h0 in range(0, PEER_HEADS, TOPK_LOCKSTEP_HEADS):
        heads = range(h0, h0 + TOPK_LOCKSTEP_HEADS)
        scores = []
        for h in heads:
            for i in range(2):
                c0 = (h * 2 + i) * KEY_HALF
                scores.append(lax.dot_general(keys_ref[i], q_ref[:, c0:c0 + KEY_HALF], (((1,), (1,)), ((), ())),
                                              preferred_element_type=f32))
        tops = _top16_rows(scores)
        cands, cidxs = [], []
        for j in range(len(heads)):
            (v0, p0, _), (v1, p1, _) = tops[2 * j], tops[2 * j + 1]
            cands.append(jnp.where(pad_row, NEG_INF, pick(g0_ref, v0) + pick(g1_ref, v1)))
            cidxs.append((pick(g0_ref, p0) * N_KEYS + pick(g1_ref, p1)) * ROW_WORDS)
        for h, (best, _, eidx) in zip(heads, _top16_rows(cands, cidxs)):
            e = jnp.exp(best - best[0:1, :])
            gate = e / jnp.sum(e, axis=0, keepdims=True)
            idx_ref[h * PEER_TOPK:(h + 1) * PEER_TOPK, :] = eidx.astype(jnp.int32)
            gate_ref[h * PEER_TOPK:(h + 1) * PEER_TOPK, :] = gate


def _peer_topk(q, sub_keys):
    t = q.shape[0]
    g0, g1 = _cand_select()
    return pl.pallas_call(
        _peer_topk_kernel,
        grid=(t // PEER_TB,),
        in_specs=[pl.BlockSpec((PEER_TB, q.shape[1]), lambda i: (i, 0)),
                  pl.BlockSpec(sub_keys.shape, lambda i: (0, 0, 0)),
                  pl.BlockSpec(g0.shape, lambda i: (0, 0)),
                  pl.BlockSpec(g1.shape, lambda i: (0, 0))],
        out_specs=[pl.BlockSpec((PEER_K, PEER_TB), lambda i: (0, i)),
                   pl.BlockSpec((PEER_K, PEER_TB), lambda i: (0, i))],
        out_shape=[jax.ShapeDtypeStruct((PEER_K, t), jnp.int32),
                   jax.ShapeDtypeStruct((PEER_K, t), jnp.float32)],
        name="peer_topk",
    )(q, sub_keys, g0, g1)


def _pack_table(tab):
    e = tab.shape[0]
    tb = tab.astype(jnp.bfloat16)
    lo = lax.bitcast_convert_type(tb[:, :HALF_D], jnp.uint16).astype(jnp.uint32)
    hi = lax.bitcast_convert_type(tb[:, HALF_D:], jnp.uint16).astype(jnp.uint32)
    return (lo | (hi << 16)).reshape(e * ROW_WORDS, 128)


def _token_groups(eight_tokens):
    def trip(g, carry):
        for sub in range(PEER_GROUP // 8):
            eight_tokens(pl.multiple_of(g * PEER_GROUP + sub * 8, 8))
        return carry

    lax.fori_loop(0, PEER_TB // PEER_GROUP, trip, 0)


def _gather_rows(idx_ref, tok, tab_ref, scr):
    nchunk = HALF_D // 128
    for k in range(PEER_K):
        e = idx_ref[tok * PEER_K + k]
        scr[k // 8, pl.ds(k % 8, nchunk, stride=8), :] = tab_ref[pl.ds(pl.multiple_of(e, ROW_WORDS), ROW_WORDS), :]
    gm = jnp.concatenate(
        [scr[:, j * 8:(j + 1) * 8, :].reshape(PEER_K, 128) for j in range(nchunk)], axis=1)
    return pltpu.bitcast(gm, jnp.bfloat16)


def _peer_u_kernel(idx_ref, ut_ref, tab_ref, out_ref, scr0, scr1):
    shp = (8, 2 * PEER_K)
    row = lax.broadcasted_iota(jnp.int32, shp, 0)
    even = (lax.broadcasted_iota(jnp.int32, shp, 1) & 1) == 0

    def eight_tokens(base):
        ut8 = ut_ref[pl.ds(base, 8), :]
        lhs = jnp.concatenate([ut8[:, :HALF_D], ut8[:, HALF_D:]], axis=0).astype(jnp.bfloat16)
        acc = jnp.zeros(shp, jnp.float32)
        for i in range(8):
            b = _gather_rows(idx_ref, base + i, tab_ref, (scr0, scr1)[i % 2])
            r = lax.dot_general(lhs, b, (((1,), (1,)), ((), ())),
                                preferred_element_type=jnp.float32)
            acc = jnp.where(row == i, jnp.where(even, r[0:8], r[8:16]), acc)
        out_ref[pl.ds(base, 8), :] = acc

    _token_groups(eight_tokens)


def _peer_v_kernel(idx_ref, a_ref, gate_ref, tab_ref, out_ref, scr0, scr1):
    shp = (8, 2 * PEER_K)
    row = lax.broadcasted_iota(jnp.int32, (8, HALF_D), 0)
    even = (lax.broadcasted_iota(jnp.int32, shp, 1) & 1) == 0

    def eight_tokens(base):
        a8 = a_ref[pl.ds(base, 8), :]
        s = a8 + pltpu.roll(a8, 2 * PEER_K - 1, axis=1)
        act = jnp.where(even, s, pltpu.roll(s, 1, axis=1))
        gelu = 0.5 * act * (1.0 + lax.erf(act * (2.0 ** -0.5)))
        w = gelu * gate_ref[pl.ds(base, 8), :]
        lhs = jnp.concatenate([jnp.where(even, w, 0.0), jnp.where(even, 0.0, w)],
                              axis=0).astype(jnp.bfloat16)
        acc_lo = jnp.zeros((8, HALF_D), jnp.float32)
        acc_hi = jnp.zeros((8, HALF_D), jnp.float32)
        for i in range(8):
            b = _gather_rows(idx_ref, base + i, tab_ref, (scr0, scr1)[i % 2])
            r = jnp.dot(lhs, b, preferred_element_type=jnp.float32)
            acc_lo = jnp.where(row == i, r[0:8], acc_lo)
            acc_hi = jnp.where(row == i, r[8:16], acc_hi)
        out_ref[pl.ds(base, 8), 0:HALF_D] = acc_lo
        out_ref[pl.ds(base, 8), HALF_D:] = acc_hi

    _token_groups(eight_tokens)


def _peer_scratch():
    return [pltpu.VMEM((PEER_K // 8, 8 * (HALF_D // 128), 128), jnp.uint32) for _ in range(2)]


def _peer_params(tab):
    table_bytes = tab.size * 4
    return pltpu.CompilerParams(dimension_semantics=("arbitrary",),
                                vmem_limit_bytes=table_bytes + (16 << 20))


def _peer_u_pass(idx_flat, un, tab):
    t = un.shape[0]
    return pl.pallas_call(
        _peer_u_kernel,
        grid=(t // PEER_TB,),
        in_specs=[pl.BlockSpec((PEER_TB * PEER_K,), lambda i: (i,), memory_space=pltpu.SMEM),
                  pl.BlockSpec((PEER_TB, D_MODEL), lambda i: (i, 0)),
                  pl.BlockSpec(memory_space=pltpu.VMEM)],
        out_specs=pl.BlockSpec((PEER_TB, 2 * PEER_K), lambda i: (i, 0)),
        out_shape=jax.ShapeDtypeStruct((t, 2 * PEER_K), jnp.float32),
        scratch_shapes=_peer_scratch(),
        compiler_params=_peer_params(tab),
        name="peer_u_pass",
    )(idx_flat, un, tab)


def _peer_v_pass(idx_flat, a, gate2, tab):
    t = a.shape[0]
    return pl.pallas_call(
        _peer_v_kernel,
        grid=(t // PEER_TB,),
        in_specs=[pl.BlockSpec((PEER_TB * PEER_K,), lambda i: (i,), memory_space=pltpu.SMEM),
                  pl.BlockSpec((PEER_TB, 2 * PEER_K), lambda i: (i, 0)),
                  pl.BlockSpec((PEER_TB, 2 * PEER_K), lambda i: (i, 0)),
                  pl.BlockSpec(memory_space=pltpu.VMEM)],
        out_specs=pl.BlockSpec((PEER_TB, D_MODEL), lambda i: (i, 0)),
        out_shape=jax.ShapeDtypeStruct((t, D_MODEL), jnp.float32),
        scratch_shapes=_peer_scratch(),
        compiler_params=_peer_params(tab),
        name="peer_v_pass",
    )(idx_flat, a, gate2, tab)


def _peer_pallas(un, q, sub_keys, peer_u, peer_v):
    idx_t, gate_t = _peer_topk(q, sub_keys)
    idx_flat = idx_t.T.reshape(-1)
    gate2 = jnp.repeat(gate_t.T, 2, axis=1)
    a = _peer_u_pass(idx_flat, un, _pack_table(peer_u))
    return _peer_v_pass(idx_flat, a, gate2, _pack_table(peer_v))


def dev_stages():
    s = jax.ShapeDtypeStruct
    f32, i32, u32 = jnp.float32, jnp.int32, jnp.uint32
    t = 32768
    bf16 = jnp.bfloat16
    return {
        "mlaproj": (_mla_proj, (s((8, 4096), i32), s((t, 384), f32), s((t, 256), f32), s((t, 128), f32),
                                s((t, 128), f32), s((384,), f32), s((256,), f32), s((384, 1536), f32),
                                s((256, 2048), f32))),
        "ssd": (_ssd_pallas, (s((8, 4096, 1536), f32), s((8, 4096, 128), f32), s((8, 4096, 1024), f32),
                              s((3, 1536), f32), s((1536,), f32), s((16,), f32), s((16,), f32), s((16,), f32),
                              s((16,), f32), s((16,), f32), s((1024,), f32))),
        "inproj": (_in_proj, (s((t, 1024), f32), s((1024,), f32), s((1024, 5312), f32))),
        "merge": (_merge, (s((t, 1024), f32), s((t, 1024), f32), s((t, 1024), bf16), s((t, 2048), f32),
                           s((1024,), f32), s((1024, 1024), f32), s((1024, 1024), f32),
                           s((1024, 1024), f32), s((1024, 2048), f32))),
        "attn": (_attention, (s((8, 4096, 2048), bf16), s((8, 4096, 2048), bf16), s((8, 4096, 1024), bf16))),
        "topk": (_peer_topk, (s((t, 2048), f32), s((2, 128, 128), f32))),
        "upass": (_peer_u_pass, (s((t * 128,), i32), s((t, 1024), f32), s((65536, 128), u32))),
        "vpass": (_peer_v_pass, (s((t * 128,), i32), s((t, 256), f32), s((t, 256), f32), s((65536, 128), u32))),
    }


def _final_norm_kernel(h_ref, p_ref, w_ref, o_ref):
    hh = h_ref[...] + p_ref[...]
    ms = jnp.mean(hh * hh, axis=-1, keepdims=True)
    o_ref[...] = hh * lax.rsqrt(ms + EPS) * w_ref[...]


def _final_norm(h, p, w):
    t, d = h.shape
    tb = 512
    return pl.pallas_call(
        _final_norm_kernel,
        grid=(t // tb,),
        in_specs=[pl.BlockSpec((tb, d), lambda i: (i, 0)),
                  pl.BlockSpec((tb, d), lambda i: (i, 0)),
                  pl.BlockSpec((1, d), lambda i: (0, 0))],
        out_specs=pl.BlockSpec((tb, d), lambda i: (i, 0)),
        out_shape=jax.ShapeDtypeStruct((t, d), jnp.float32),
    )(h, p, w.reshape(1, d))


def kernel(x, positions, norm_mix_w, w_in, conv_w, conv_b, dt_bias_fwd, dt_bias_bwd, a_log_fwd, a_log_bwd, d_skip, ssd_norm_w, q_a_norm_w, w_q_b, kv_a_norm_w, w_kv_b, w_proj_ssd, w_proj_mla, w_out, norm_ffn_w, peer_w_q, peer_sub_keys, peer_u, peer_v, norm_final_w):
    b, s, _ = x.shape
    x2 = x.reshape(b * s, D_MODEL)
    z, xbc, dt128, q_a, kv_a, kr, krs, gate_logits = _in_proj(x2, norm_mix_w, w_in)
    y_ssd = _ssd_pallas(xbc.reshape(b, s, -1), dt128.reshape(b, s, -1), z.reshape(b, s, -1), conv_w, conv_b,
                        dt_bias_fwd, dt_bias_bwd, a_log_fwd, a_log_bwd, d_skip, ssd_norm_w)

    y_mla = _mla_pallas(positions, q_a, kv_a, kr, krs, q_a_norm_w, kv_a_norm_w, w_q_b, w_kv_b)

    h2, un, q_peer = _merge(x2, y_ssd.reshape(b * s, -1), y_mla, gate_logits, norm_ffn_w,
                            w_proj_ssd, w_proj_mla, w_out, peer_w_q)
    pf = _peer_pallas(un, q_peer, peer_sub_keys, peer_u, peer_v)
    out = _final_norm(h2, pf, norm_final_w)
    return out.reshape(b, s, D_MODEL)
```

```python
import functools
import math

import jax
import jax.numpy as jnp
from jax import lax
from jax.experimental import pallas as pl
from jax.experimental.pallas import tpu as pltpu

D_MODEL = 1024
SSD_D_INNER = 1024
SSD_HEADDIM = 64
SSD_HEADS = 16
SSD_GROUPS = 2
SSD_HPG = 8
SSD_STATE = 128
CONV_WIDTH = 3
CHUNK = 128
SSD_CONV_DIM = SSD_D_INNER + 2 * SSD_GROUPS * SSD_STATE
MLA_HEADS = 16
QK_NOPE = 64
QK_ROPE = 32
V_HEAD = 64
Q_LORA = 384
KV_LORA = 256
Q_BLOCK = 128
ROPE_THETA = 10000.0
PEER_HEADS = 8
N_KEYS = 128
KEY_HALF = 128
PEER_TOPK = 16
TOK_BLOCK = 128
EPS = 1e-6
HEAD_PAD = 128
QK_DIM = QK_NOPE + QK_ROPE
LOG2E = 1.4426950408889634
IN_SIZES = (SSD_D_INNER, SSD_CONV_DIM, 2 * SSD_HEADS, Q_LORA, KV_LORA, QK_ROPE, 2 * D_MODEL)


def _split_points(sizes):
    pts, acc = [], 0
    for s in sizes[:-1]:
        acc += s
        pts.append(acc)
    return pts


def _rms(x, w):
    y = x * lax.rsqrt(jnp.mean(x * x, axis=-1, keepdims=True) + EPS)
    return y * w


def _segsum(a):
    t = a.shape[-1]
    cs = jnp.cumsum(a, axis=-1)
    diff = cs[..., :, None] - cs[..., None, :]
    mask = jnp.tril(jnp.ones((t, t), dtype=bool))
    return jnp.where(mask, diff, -jnp.inf)


def _ssd_scan(xh, dt, A, Bm, Cm):
    b, l, g, r, p = xh.shape
    n = Bm.shape[-1]
    c = l // CHUNK
    xd = (xh * dt[..., None]).reshape(b, c, CHUNK, g, r, p)
    a = (dt * A).reshape(b, c, CHUNK, g, r)
    a = jnp.moveaxis(a, 2, -1)
    Bc = Bm.reshape(b, c, CHUNK, g, n)
    Cc = Cm.reshape(b, c, CHUNK, g, n)
    a_cs = jnp.cumsum(a, axis=-1)
    L = jnp.exp(_segsum(a))
    CB = jnp.einsum('bclgn,bcsgn->bcgls', Cc, Bc)
    y_diag = jnp.einsum('bcgls,bcgrls,bcsgrp->bclgrp', CB, L, xd)
    decay_states = jnp.exp(a_cs[..., -1:] - a_cs)
    states = jnp.einsum('bcsgn,bcgrs,bcsgrp->bcgrpn', Bc, decay_states, xd)
    chunk_tot = jnp.moveaxis(a_cs[..., -1], 1, -1)
    decay_chunk = jnp.exp(_segsum(jnp.pad(chunk_tot, ((0, 0), (0, 0), (0, 0), (1, 0)))))
    states0 = jnp.pad(states, ((0, 0), (1, 0), (0, 0), (0, 0), (0, 0), (0, 0)))
    new_states = jnp.einsum('bgrzc,bcgrpn->bzgrpn', decay_chunk, states0)
    prev = new_states[:, :-1]
    y_off = jnp.einsum('bclgn,bcgrpn,bcgrl->bclgrp', Cc, prev, jnp.exp(a_cs))
    return (y_diag + y_off).reshape(b, l, g, r, p)


def _rope_tables(positions, dim):
    inv = 1.0 / (ROPE_THETA ** (jnp.arange(0, dim, 2, dtype=jnp.float32) / dim))
    ang = positions.astype(jnp.float32)[..., None] * inv
    return jnp.cos(ang), jnp.sin(ang)


def _apply_rope(x, cos, sin):
    x1, x2 = jnp.split(x, 2, axis=-1)
    return jnp.concatenate([x1 * cos - x2 * sin, x1 * sin + x2 * cos], axis=-1)


def _mla_attention(q_nope, q_rope, k_nope, k_rope, v):
    b, s, h, _ = q_nope.shape
    nb = s // Q_BLOCK
    scale = (QK_NOPE + QK_ROPE) ** -0.5

    def block(args):
        qn, qr = args
        sc = (jnp.einsum('bqhd,bkhd->bhqk', qn, k_nope)
              + jnp.einsum('bqhd,bkd->bhqk', qr, k_rope))
        p = jax.nn.softmax(sc * scale, axis=-1)
        return jnp.einsum('bhqk,bkhd->bqhd', p, v)

    qn_b = q_nope.reshape(b, nb, Q_BLOCK, h, QK_NOPE).swapaxes(0, 1)
    qr_b = q_rope.reshape(b, nb, Q_BLOCK, h, QK_ROPE).swapaxes(0, 1)
    out = lax.map(block, (qn_b, qr_b))
    return out.swapaxes(0, 1).reshape(b, s, h * V_HEAD)


def _peer_ffn(u, w_q, sub_keys, peer_u, peer_v):
    b, s, d = u.shape
    n_exp = peer_u.shape[0]
    ub = u.reshape((b * s) // TOK_BLOCK, TOK_BLOCK, d)

    def block(ut):
        q = (ut @ w_q).reshape(TOK_BLOCK, PEER_HEADS, 2, KEY_HALF)
        sc = jnp.einsum('thid,ind->thin', q, sub_keys)
        v_top, i_top = lax.top_k(sc, PEER_TOPK)
        cand = v_top[:, :, 0, :, None] + v_top[:, :, 1, None, :]
        cand_idx = i_top[:, :, 0, :, None] * N_KEYS + i_top[:, :, 1, None, :]
        cand = cand.reshape(TOK_BLOCK, PEER_HEADS, PEER_TOPK * PEER_TOPK)
        cand_idx = cand_idx.reshape(TOK_BLOCK, PEER_HEADS, PEER_TOPK * PEER_TOPK)
        best, pos = lax.top_k(cand, PEER_TOPK)
        idx = jnp.take_along_axis(cand_idx, pos, axis=-1)
        gate = jax.nn.softmax(best, axis=-1)
        act = jax.nn.gelu(jnp.einsum('td,thkd->thk', ut, peer_u[idx]), approximate=False)
        return jnp.einsum('thk,thkd->td', gate * act, peer_v[idx])

    return lax.map(block, ub).reshape(b, s, d)


IN_TB = 256
DT_PAD = 128
GATE_W = 2 * D_MODEL
IN_OUT_WIDTHS = (SSD_D_INNER, SSD_CONV_DIM, DT_PAD, Q_LORA, KV_LORA, HEAD_PAD, HEAD_PAD, GATE_W)


def _in_proj_kernel(x_ref, nw_ref, w_ref, *out_refs):
    u = _rms_rows(x_ref[...], nw_ref[...]).astype(jnp.bfloat16)
    col = 0
    for o_ref, width in zip(out_refs, IN_OUT_WIDTHS):
        o_ref[...] = jnp.dot(u, w_ref[:, col:col + width], preferred_element_type=jnp.float32)
        col += width


def _arrange_w_in(w_in):
    z, xbc, dt, qa, kva, kr, gates = jnp.split(w_in, _split_points(IN_SIZES), axis=1)
    d = w_in.shape[0]
    half = QK_ROPE // 2
    dt_p = jnp.pad(dt, ((0, 0), (0, DT_PAD - dt.shape[1])))
    kr3 = kr.reshape(d, 1, QK_ROPE)
    kr_p = _pad_heads(None, kr3)
    krs_p = _pad_heads(None, _rope_partner(kr3[..., :half], kr3[..., half:]))
    return jnp.concatenate([z, xbc, dt_p, qa, kva, kr_p, krs_p, gates], axis=1).astype(jnp.bfloat16)


def _in_proj(x2, norm_w, w_in):
    t, d = x2.shape
    w = _arrange_w_in(w_in)
    tb = min(IN_TB, t)
    return pl.pallas_call(
        _in_proj_kernel,
        grid=(t // tb,),
        in_specs=[pl.BlockSpec((tb, d), lambda i: (i, 0)),
                  pl.BlockSpec((1, d), lambda i: (0, 0)),
                  pl.BlockSpec(w.shape, lambda i: (0, 0))],
        out_specs=[pl.BlockSpec((tb, n), lambda i: (i, 0)) for n in IN_OUT_WIDTHS],
        out_shape=[jax.ShapeDtypeStruct((t, n), jnp.float32) for n in IN_OUT_WIDTHS],
        compiler_params=pltpu.CompilerParams(dimension_semantics=("arbitrary",),
                                             vmem_limit_bytes=56 << 20),
        name="in_proj",
    )(x2, norm_w.reshape(1, d), w)


HEAD_W = SSD_HEADS * SSD_HEADDIM
GROUP_W = HEAD_W // SSD_GROUPS
BC_W = SSD_GROUPS * SSD_STATE
_HI = lax.Precision.HIGHEST


def _ssd_kernel(direction, xc_ref, xp_ref, xn_ref, dt_ref, cw_ref, cb_ref, dtb_ref, a_ref, e_ref,
                tri_ref, *rest):
    f32, bf16 = jnp.float32, jnp.bfloat16
    fwd = direction == 0
    if fwd:
        dsk_ref, y_ref, h_scr = rest
    else:
        y1_ref, z_ref, nw_ref, y_ref, h_scr = rest
    c = pl.program_id(1)
    nc = pl.num_programs(1)
    cm = c if fwd else nc - 1 - c

    @pl.when(c == 0)
    def _():
        h_scr[...] = jnp.zeros_like(h_scr)

    xc = xc_ref[0]
    row = lax.broadcasted_iota(jnp.int32, xc.shape, 0)
    prev_row = jnp.where(cm > 0, xp_ref[0, 7:8, :], 0.0)
    next_row = jnp.where(cm < nc - 1, xn_ref[0, 0:1, :], 0.0)
    x_m1 = jnp.where(row == 0, prev_row, pltpu.roll(xc, 1, axis=0))
    x_p1 = jnp.where(row == CHUNK - 1, next_row, pltpu.roll(xc, CHUNK - 1, axis=0))
    conv = cw_ref[0:1, :] * x_m1 + cw_ref[1:2, :] * xc + cw_ref[2:3, :] * x_p1 + cb_ref[...]
    act = conv * jax.nn.sigmoid(conv)
    xs = act[:, :HEAD_W]

    dtr = dt_ref[0] + dtb_ref[...]
    dt = jnp.maximum(dtr, 0.0) + jnp.log(1.0 + jnp.exp(-jnp.abs(dtr)))
    a = dt * a_ref[...]
    cs = jnp.dot(tri_ref[...], a, precision=_HI, preferred_element_type=f32)
    expand = lambda v: jnp.dot(v, e_ref[...], precision=_HI, preferred_element_type=f32)
    cs_e = expand(cs)
    tot_e = cs_e[CHUNK - 1:CHUNK, :]
    if fwd:
        pos = cs
        w_state = jnp.exp(tot_e - cs_e)
        out_scale = jnp.exp(cs_e)
    else:
        pos = cs - a
        e_e = cs_e - expand(a)
        w_state = jnp.exp(e_e)
        out_scale = jnp.exp(tot_e - e_e)
    pos_t = pos.T
    xd = xs * expand(dt)
    xdb = xd.astype(bf16)
    xw = (xd * w_state).astype(bf16)
    chunk_decay = jnp.exp(tot_e)

    li = lax.broadcasted_iota(jnp.int32, (CHUNK, CHUNK), 0)
    si = lax.broadcasted_iota(jnp.int32, (CHUNK, CHUNK), 1)
    mask = (si <= li) if fwd else (si >= li)
    for g in range(SSD_GROUPS):
        gsl = slice(g * GROUP_W, (g + 1) * GROUP_W)
        bg = act[:, HEAD_W + g * SSD_STATE:HEAD_W + (g + 1) * SSD_STATE]
        cg = act[:, HEAD_W + BC_W + g * SSD_STATE:HEAD_W + BC_W + (g + 1) * SSD_STATE].astype(bf16)
        cb = lax.dot_general(cg, bg.astype(bf16), (((1,), (1,)), ((), ())), preferred_element_type=f32)
        h_in = h_scr[:, gsl]
        y_off = jnp.dot(cg, h_in.astype(bf16), preferred_element_type=f32) * out_scale[:, gsl]
        s_new = jnp.dot(bg.T.astype(bf16), xw[:, gsl], preferred_element_type=f32)
        h_scr[:, gsl] = h_in * chunk_decay[:, gsl] + s_new
        for r in range(SSD_HPG):
            hd = g * SSD_HPG + r
            lane = direction * SSD_HEADS + hd
            col = pos[:, lane:lane + 1]
            rw = pos_t[lane:lane + 1, :]
            diff = (col - rw) if fwd else (rw - col)
            m = (cb * jnp.exp(jnp.where(mask, diff, NEG_INF))).astype(bf16)
            hsl = slice(hd * SSD_HEADDIM, (hd + 1) * SSD_HEADDIM)
            y_d = jnp.dot(m, xdb[:, hsl], preferred_element_type=f32)
            y_ref[0, :, hsl] = y_d + y_off[:, r * SSD_HEADDIM:(r + 1) * SSD_HEADDIM]

    if fwd:
        y_ref[0] = y_ref[0] + xs * dsk_ref[...]
    else:
        yg = (y_ref[0] + y1_ref[0]) * (z_ref[0] * jax.nn.sigmoid(z_ref[0]))
        for g in range(SSD_GROUPS):
            gsl = slice(g * GROUP_W, (g + 1) * GROUP_W)
            y_ref[0, :, gsl] = _rms_rows(yg[:, gsl], nw_ref[:, gsl])


def _ssd_pass(direction, xbc, dt128, consts, extra):
    b, s, cw_ = xbc.shape
    nc = s // CHUNK
    per8 = CHUNK // 8
    cm = (lambda c: c) if direction == 0 else (lambda c: nc - 1 - c)
    cur = lambda n: pl.BlockSpec((1, CHUNK, n), lambda bi, c: (bi, cm(c), 0))
    prev = pl.BlockSpec((1, 8, cw_), lambda bi, c: (bi, jnp.maximum(cm(c) * per8 - 1, 0), 0))
    nxt = pl.BlockSpec((1, 8, cw_), lambda bi, c: (bi, jnp.minimum((cm(c) + 1) * per8, s // 8 - 1), 0))
    full = lambda a: pl.BlockSpec(a.shape, lambda bi, c: (0,) * a.ndim)
    if direction == 0:
        extra_specs = [full(extra[0])]
    else:
        extra_specs = [cur(HEAD_W), cur(HEAD_W), full(extra[2])]
    return pl.pallas_call(
        functools.partial(_ssd_kernel, direction),
        grid=(b, nc),
        in_specs=[cur(cw_), prev, nxt, cur(DT_PAD)] + [full(a) for a in consts] + extra_specs,
        out_specs=cur(HEAD_W),
        out_shape=jax.ShapeDtypeStruct((b, s, HEAD_W), jnp.float32),
        scratch_shapes=[pltpu.VMEM((SSD_STATE, HEAD_W), jnp.float32)],
        compiler_params=pltpu.CompilerParams(dimension_semantics=("arbitrary", "arbitrary"),
                                             vmem_limit_bytes=48 << 20),
        name="ssd_fwd" if direction == 0 else "ssd_bwd",
    )(xbc, xbc, xbc, dt128, *consts, *extra)


def _ssd_pallas(xbc, dt128, z, conv_w, conv_b, dt_bias_fwd, dt_bias_bwd, a_log_fwd, a_log_bwd,
                d_skip, ssd_norm_w):
    f32 = jnp.float32
    pad = lambda v: jnp.pad(v, (0, DT_PAD - v.shape[0])).reshape(1, DT_PAD)
    dtb = pad(jnp.concatenate([dt_bias_fwd, dt_bias_bwd]))
    a_neg = pad(-jnp.exp(jnp.concatenate([a_log_fwd, a_log_bwd])))
    tri = (jnp.arange(CHUNK)[None, :] <= jnp.arange(CHUNK)[:, None]).astype(f32)
    head_of_lane = jnp.arange(HEAD_W) // SSD_HEADDIM
    y = None
    for direction in range(2):
        e = (jnp.arange(DT_PAD)[:, None] == (direction * SSD_HEADS + head_of_lane)[None, :]).astype(f32)
        consts = (conv_w, conv_b.reshape(1, -1), dtb, a_neg, e, tri)
        if direction == 0:
            extra = (jnp.repeat(d_skip, SSD_HEADDIM).reshape(1, HEAD_W),)
        else:
            extra = (y, z, ssd_norm_w.reshape(1, HEAD_W))
        y = _ssd_pass(direction, xbc, dt128, consts, extra)
    return y


MERGE_TB = 256


def _merge_kernel(x_ref, ys_ref, ym_ref, g_ref, nw_ref, pa_ref, pb_ref, wo_ref, wq_ref,
                  h_ref, un_ref, q_ref):
    bf16 = jnp.bfloat16
    f32 = jnp.float32
    g = jax.nn.sigmoid(g_ref[...])
    ma = jnp.dot(ys_ref[...].astype(bf16), pa_ref[...], preferred_element_type=f32)
    mb = jnp.dot(ym_ref[...].astype(bf16), pb_ref[...], preferred_element_type=f32)
    m = g[:, :D_MODEL] * ma + g[:, D_MODEL:] * mb
    h = x_ref[...] + jnp.dot(m.astype(bf16), wo_ref[...], preferred_element_type=f32)
    h_ref[...] = h
    un = _rms_rows(h, nw_ref[...])
    un_ref[...] = un
    q_ref[...] = jnp.dot(un.astype(bf16), wq_ref[...], preferred_element_type=f32)


def _merge(x2, y_ssd, y_mla, gates, norm_ffn_w, w_proj_ssd, w_proj_mla, w_out, peer_w_q):
    t, d = x2.shape
    bf16 = jnp.bfloat16
    tb = min(MERGE_TB, t)
    nq = peer_w_q.shape[1]
    row = lambda n: pl.BlockSpec((tb, n), lambda i: (i, 0))
    full = lambda a: pl.BlockSpec(a.shape, lambda i: (0, 0))
    ws = [w.astype(bf16) for w in (w_proj_ssd, w_proj_mla, w_out, peer_w_q)]
    nw = norm_ffn_w.reshape(1, d)
    return pl.pallas_call(
        _merge_kernel,
        grid=(t // tb,),
        in_specs=[row(d), row(d), row(d), row(GATE_W), full(nw)] + [full(w) for w in ws],
        out_specs=[row(d), row(d), row(nq)],
        out_shape=[jax.ShapeDtypeStruct((t, d), jnp.float32), jax.ShapeDtypeStruct((t, d), jnp.float32),
                   jax.ShapeDtypeStruct((t, nq), jnp.float32)],
        compiler_params=pltpu.CompilerParams(dimension_semantics=("arbitrary",),
                                             vmem_limit_bytes=56 << 20),
        name="merge",
    )(x2, y_ssd, y_mla, gates, nw, *ws)


MLA_TB = 512
ATT_TQ = 256
ATT_HEADS = 4


def _rms_rows(x, w):
    return x * lax.rsqrt(jnp.mean(x * x, axis=-1, keepdims=True) + EPS) * w


def _mla_proj_kernel(pos_ref, qa_ref, kva_ref, kr_ref, krs_ref, qnw_ref, kvnw_ref, inv_ref,
                     wq_ref, wqs_ref, wk_ref, wv_ref, q_out, k_out, v_out):
    bf16 = jnp.bfloat16
    ang = pos_ref[...].astype(jnp.float32) * inv_ref[...]
    cos, sin = jnp.cos(ang), jnp.sin(ang)
    qa = _rms_rows(qa_ref[...], qnw_ref[...]).astype(bf16)
    q = jnp.dot(qa, wq_ref[...], preferred_element_type=jnp.float32)
    qs = jnp.dot(qa, wqs_ref[...], preferred_element_type=jnp.float32)
    kva = _rms_rows(kva_ref[...], kvnw_ref[...]).astype(bf16)
    k = jnp.dot(kva, wk_ref[...], preferred_element_type=jnp.float32)
    v_out[...] = jnp.dot(kva, wv_ref[...], preferred_element_type=jnp.float32).astype(bf16)
    kro = kr_ref[...] * cos + krs_ref[...] * sin
    c = (QK_DIM ** -0.5) * LOG2E
    for h in range(MLA_HEADS):
        sl = slice(h * HEAD_PAD, (h + 1) * HEAD_PAD)
        q_out[:, sl] = ((q[:, sl] * cos + qs[:, sl] * sin) * c).astype(bf16)
        k_out[:, sl] = (k[:, sl] + kro).astype(bf16)


def _rope_partner(w1, w2):
    return jnp.concatenate([-w2, w1], axis=-1)


def _pad_heads(nope, rope):
    ref = nope if nope is not None else rope
    lead = ref.shape[:-1]
    z = lambda n: jnp.zeros(lead + (n,), ref.dtype)
    parts = [nope if nope is not None else z(QK_NOPE), rope if rope is not None else z(QK_ROPE),
             z(HEAD_PAD - QK_DIM)]
    out = jnp.concatenate(parts, axis=-1)
    return out.reshape(out.shape[:-2] + (out.shape[-2] * HEAD_PAD,))


def _mla_proj(positions, q_a, kv_a, kr, krs, q_a_norm_w, kv_a_norm_w, w_q_b, w_kv_b):
    t = q_a.shape[0]
    bf16 = jnp.bfloat16
    half = QK_ROPE // 2
    wq = w_q_b.reshape(Q_LORA, MLA_HEADS, QK_DIM)
    wq_n, wq_r = wq[..., :QK_NOPE], wq[..., QK_NOPE:]
    wq_main = _pad_heads(wq_n, wq_r).astype(bf16)
    wq_part = _pad_heads(None, _rope_partner(wq_r[..., :half], wq_r[..., half:])).astype(bf16)
    wkv = w_kv_b.reshape(KV_LORA, MLA_HEADS, QK_NOPE + V_HEAD)
    wk = _pad_heads(wkv[..., :QK_NOPE], None).astype(bf16)
    wv = wkv[..., QK_NOPE:].reshape(KV_LORA, MLA_HEADS * V_HEAD).astype(bf16)
    inv = 1.0 / (ROPE_THETA ** (jnp.arange(0, QK_ROPE, 2, dtype=jnp.float32) / QK_ROPE))
    inv_lane = jnp.concatenate([jnp.zeros((QK_NOPE,), jnp.float32), inv, inv,
                                jnp.zeros((HEAD_PAD - QK_DIM,), jnp.float32)]).reshape(1, HEAD_PAD)
    tb = min(MLA_TB, t)
    row = lambda n: pl.BlockSpec((tb, n), lambda i: (i, 0))
    full = lambda a: pl.BlockSpec(a.shape, lambda i: (0,) * a.ndim)
    qn = q_a_norm_w.reshape(1, Q_LORA)
    kvn = kv_a_norm_w.reshape(1, KV_LORA)
    args = (positions.reshape(t, 1), q_a, kv_a, kr, krs, qn, kvn, inv_lane, wq_main, wq_part, wk, wv)
    in_specs = [row(1), row(Q_LORA), row(KV_LORA), row(HEAD_PAD), row(HEAD_PAD)] + [full(a) for a in args[5:]]
    nq = MLA_HEADS * HEAD_PAD
    return pl.pallas_call(
        _mla_proj_kernel,
        grid=(t // tb,),
        in_specs=in_specs,
        out_specs=[row(nq), row(nq), row(MLA_HEADS * V_HEAD)],
        out_shape=[jax.ShapeDtypeStruct((t, nq), bf16), jax.ShapeDtypeStruct((t, nq), bf16),
                   jax.ShapeDtypeStruct((t, MLA_HEADS * V_HEAD), bf16)],
        compiler_params=pltpu.CompilerParams(dimension_semantics=("arbitrary",),
                                             vmem_limit_bytes=48 << 20),
        name="mla_proj",
    )(*args)


def _attn_kernel(q_ref, k_ref, v_ref, o_ref):
    tq = q_ref.shape[1]
    lane = lax.broadcasted_iota(jnp.int32, (tq, 2 * V_HEAD), 1)
    for pair in range(ATT_HEADS // 2):
        vsl = slice(pair * 2 * V_HEAD, (pair + 1) * 2 * V_HEAD)
        v = v_ref[0, :, vsl]
        outs = []
        for hh in range(2 * pair, 2 * pair + 2):
            sl = slice(hh * HEAD_PAD, (hh + 1) * HEAD_PAD)
            s = lax.dot_general(q_ref[0, :, sl], k_ref[0, :, sl], (((1,), (1,)), ((), ())),
                                preferred_element_type=jnp.float32)
            m = jnp.max(s, axis=-1, keepdims=True)
            p = jnp.exp2(s - m)
            l = jnp.sum(p, axis=-1, keepdims=True)
            pv = jnp.dot(p.astype(jnp.bfloat16), v, preferred_element_type=jnp.float32)
            outs.append(pv / l)
        o_ref[0, :, vsl] = jnp.where(lane < V_HEAD, outs[0], outs[1]).astype(o_ref.dtype)


def _attention(q, k, v):
    b, s, _ = q.shape
    tq = min(ATT_TQ, s)
    qw, vw = ATT_HEADS * HEAD_PAD, ATT_HEADS * V_HEAD
    return pl.pallas_call(
        _attn_kernel,
        grid=(b, MLA_HEADS // ATT_HEADS, s // tq),
        in_specs=[pl.BlockSpec((1, tq, qw), lambda bi, j, i: (bi, i, j)),
                  pl.BlockSpec((1, s, qw), lambda bi, j, i: (bi, 0, j)),
                  pl.BlockSpec((1, s, vw), lambda bi, j, i: (bi, 0, j))],
        out_specs=pl.BlockSpec((1, tq, vw), lambda bi, j, i: (bi, i, j)),
        out_shape=jax.ShapeDtypeStruct((b, s, MLA_HEADS * V_HEAD), jnp.bfloat16),
        compiler_params=pltpu.CompilerParams(dimension_semantics=("arbitrary",) * 3,
                                             vmem_limit_bytes=56 << 20),
        name="mla_attention",
    )(q, k, v)


def _mla_pallas(positions, q_a, kv_a, kr, krs, q_a_norm_w, kv_a_norm_w, w_q_b, w_kv_b):
    b, s = positions.shape
    q, k, v = _mla_proj(positions, q_a, kv_a, kr, krs, q_a_norm_w, kv_a_norm_w, w_q_b, w_kv_b)
    y = _attention(q.reshape(b, s, -1), k.reshape(b, s, -1), v.reshape(b, s, -1))
    return y.reshape(b * s, -1)


PEER_TB = 128
PEER_K = PEER_HEADS * PEER_TOPK
PEER_GROUP = 32
HALF_D = D_MODEL // 2
NEG_INF = float("-inf")


def _top16_rows(xs, extras=None):
    n = len(xs)
    rows = xs[0].shape[0]
    iota = lax.broadcasted_iota(jnp.int32, xs[0].shape, 0).astype(jnp.float32)
    xs = list(xs)
    vals, poss, exts = [[] for _ in xs], [[] for _ in xs], [[] for _ in xs]
    for _ in range(PEER_TOPK):
        for j in range(n):
            m = jnp.max(xs[j], axis=0, keepdims=True)
            p = jnp.min(jnp.where(xs[j] == m, iota, float(rows)), axis=0, keepdims=True)
            hit = iota == p
            if extras is not None:
                exts[j].append(jnp.sum(jnp.where(hit, extras[j], 0.0), axis=0, keepdims=True))
            xs[j] = jnp.where(hit, NEG_INF, xs[j])
            vals[j].append(m)
            poss[j].append(p)
    cat = lambda parts: jnp.concatenate(parts, axis=0)
    return [(cat(vals[j]), cat(poss[j]), cat(exts[j]) if extras is not None else None) for j in range(n)]


_CAND_PAIRS = [(a, b) for a in range(PEER_TOPK) for b in range(PEER_TOPK) if (a + 1) * (b + 1) <= PEER_TOPK]
CAND_ROWS = -(-len(_CAND_PAIRS) // 8) * 8
ROW_WORDS = 4
TOPK_LOCKSTEP_HEADS = 4


def _cand_select():
    g0 = [[0.0] * PEER_TOPK for _ in range(CAND_ROWS)]
    g1 = [[0.0] * PEER_TOPK for _ in range(CAND_ROWS)]
    for r, (a, b) in enumerate(_CAND_PAIRS):
        g0[r][a] = 1.0
        g1[r][b] = 1.0
    return jnp.array(g0, jnp.float32), jnp.array(g1, jnp.float32)


def _peer_topk_kernel(q_ref, keys_ref, g0_ref, g1_ref, idx_ref, gate_ref):
    f32 = jnp.float32
    pick = lambda g_ref, v: jnp.dot(g_ref[...], v, precision=lax.Precision.HIGHEST, preferred_element_type=f32)
    pad_row = lax.broadcasted_iota(jnp.int32, (CAND_ROWS, q_ref.shape[0]), 0) >= len(_CAND_PAIRS)
    for h0 in range(0, PEER_HEADS, TOPK_LOCKSTEP_HEADS):
        heads = range(h0, h0 + TOPK_LOCKSTEP_HEADS)
        scores = []
        for h in heads:
            for i in range(2):
                c0 = (h * 2 + i) * KEY_HALF
                scores.append(lax.dot_general(keys_ref[i], q_ref[:, c0:c0 + KEY_HALF], (((1,), (1,)), ((), ())),
                                              preferred_element_type=f32))
        tops = _top16_rows(scores)
        cands, cidxs = [], []
        for j in range(len(heads)):
            (v0, p0, _), (v1, p1, _) = tops[2 * j], tops[2 * j + 1]
            cands.append(jnp.where(pad_row, NEG_INF, pick(g0_ref, v0) + pick(g1_ref, v1)))
            cidxs.append(pick(g0_ref, p0) * N_KEYS + pick(g1_ref, p1))
        for h, (best, _, eidx) in zip(heads, _top16_rows(cands, cidxs)):
            e = jnp.exp(best - best[0:1, :])
            gate = e / jnp.sum(e, axis=0, keepdims=True)
            idx_ref[h * PEER_TOPK:(h + 1) * PEER_TOPK, :] = eidx.astype(jnp.int32)
            gate_ref[h * PEER_TOPK:(h + 1) * PEER_TOPK, :] = gate


def _peer_topk(q, sub_keys):
    t = q.shape[0]
    g0, g1 = _cand_select()
    return pl.pallas_call(
        _peer_topk_kernel,
        grid=(t // PEER_TB,),
        in_specs=[pl.BlockSpec((PEER_TB, q.shape[1]), lambda i: (i, 0)),
                  pl.BlockSpec(sub_keys.shape, lambda i: (0, 0, 0)),
                  pl.BlockSpec(g0.shape, lambda i: (0, 0)),
                  pl.BlockSpec(g1.shape, lambda i: (0, 0))],
        out_specs=[pl.BlockSpec((PEER_K, PEER_TB), lambda i: (0, i)),
                   pl.BlockSpec((PEER_K, PEER_TB), lambda i: (0, i))],
        out_shape=[jax.ShapeDtypeStruct((PEER_K, t), jnp.int32),
                   jax.ShapeDtypeStruct((PEER_K, t), jnp.float32)],
        name="peer_topk",
    )(q, sub_keys, g0, g1)


def _pack_table(tab):
    e = tab.shape[0]
    tb = tab.astype(jnp.bfloat16)
    lo = lax.bitcast_convert_type(tb[:, :HALF_D], jnp.uint16).astype(jnp.uint32)
    hi = lax.bitcast_convert_type(tb[:, HALF_D:], jnp.uint16).astype(jnp.uint32)
    return (lo | (hi << 16)).reshape(e * ROW_WORDS, 128)


def _token_groups(eight_tokens, with_sub=False):
    def trip(g, carry):
        for sub in range(PEER_GROUP // 8):
            base = pl.multiple_of(g * PEER_GROUP + sub * 8, 8)
            eight_tokens(base, sub) if with_sub else eight_tokens(base)
        return carry

    lax.fori_loop(0, PEER_TB // PEER_GROUP, trip, 0)


def _gather_rows(idx_ref, tok, tab_ref, scr):
    nchunk = HALF_D // 128
    for k in range(PEER_K):
        e = idx_ref[tok * PEER_K + k]
        scr[k // 8, pl.ds(k % 8, nchunk, stride=8), :] = tab_ref[pl.ds(pl.multiple_of(e, ROW_WORDS), ROW_WORDS), :]
    gm = jnp.concatenate(
        [scr[:, j * 8:(j + 1) * 8, :].reshape(PEER_K, 128) for j in range(nchunk)], axis=1)
    return pltpu.bitcast(gm, jnp.bfloat16)


def _peer_u_kernel(idx_ref, ut_ref, tab_ref, out_ref, scr0, scr1):
    shp = (8, 2 * PEER_K)
    row = lax.broadcasted_iota(jnp.int32, shp, 0)
    even = (lax.broadcasted_iota(jnp.int32, shp, 1) & 1) == 0

    def eight_tokens(base):
        ut8 = ut_ref[pl.ds(base, 8), :]
        lhs = jnp.concatenate([ut8[:, :HALF_D], ut8[:, HALF_D:]], axis=0).astype(jnp.bfloat16)
        acc = jnp.zeros(shp, jnp.float32)
        for i in range(8):
            b = _gather_rows(idx_ref, base + i, tab_ref, (scr0, scr1)[i % 2])
            r = lax.dot_general(lhs, b, (((1,), (1,)), ((), ())),
                                preferred_element_type=jnp.float32)
            acc = jnp.where(row == i, jnp.where(even, r[0:8], r[8:16]), acc)
        out_ref[pl.ds(base, 8), :] = acc

    _token_groups(eight_tokens)


def _peer_v_kernel(idx_ref, a_ref, gate_ref, tab_ref, out_ref, scr0, scr1):
    shp = (8, 2 * PEER_K)
    row = lax.broadcasted_iota(jnp.int32, (8, HALF_D), 0)
    even = (lax.broadcasted_iota(jnp.int32, shp, 1) & 1) == 0

    def eight_tokens(base):
        a8 = a_ref[pl.ds(base, 8), :]
        s = a8 + pltpu.roll(a8, 2 * PEER_K - 1, axis=1)
        act = jnp.where(even, s, pltpu.roll(s, 1, axis=1))
        gelu = 0.5 * act * (1.0 + lax.erf(act * (2.0 ** -0.5)))
        w = gelu * gate_ref[pl.ds(base, 8), :]
        lhs = jnp.concatenate([jnp.where(even, w, 0.0), jnp.where(even, 0.0, w)],
                              axis=0).astype(jnp.bfloat16)
        acc_lo = jnp.zeros((8, HALF_D), jnp.float32)
        acc_hi = jnp.zeros((8, HALF_D), jnp.float32)
        for i in range(8):
            b = _gather_rows(idx_ref, base + i, tab_ref, (scr0, scr1)[i % 2])
            r = jnp.dot(lhs, b, preferred_element_type=jnp.float32)
            acc_lo = jnp.where(row == i, r[0:8], acc_lo)
            acc_hi = jnp.where(row == i, r[8:16], acc_hi)
        out_ref[pl.ds(base, 8), 0:HALF_D] = acc_lo
        out_ref[pl.ds(base, 8), HALF_D:] = acc_hi

    _token_groups(eight_tokens)


def _peer_scratch():
    return [pltpu.VMEM((PEER_K // 8, 8 * (HALF_D // 128), 128), jnp.uint32) for _ in range(2)]


def _peer_params(tab):
    table_bytes = tab.size * 4
    return pltpu.CompilerParams(dimension_semantics=("arbitrary",),
                                vmem_limit_bytes=table_bytes + (16 << 20))


def _peer_u_pass(idx_flat, un, tab):
    t = un.shape[0]
    return pl.pallas_call(
        _peer_u_kernel,
        grid=(t // PEER_TB,),
        in_specs=[pl.BlockSpec((PEER_TB * PEER_K,), lambda i: (i,), memory_space=pltpu.SMEM),
                  pl.BlockSpec((PEER_TB, D_MODEL), lambda i: (i, 0)),
                  pl.BlockSpec(memory_space=pltpu.VMEM)],
        out_specs=pl.BlockSpec((PEER_TB, 2 * PEER_K), lambda i: (i, 0)),
        out_shape=jax.ShapeDtypeStruct((t, 2 * PEER_K), jnp.float32),
        scratch_shapes=_peer_scratch(),
        compiler_params=_peer_params(tab),
        name="peer_u_pass",
    )(idx_flat, un, tab)


def _peer_v_pass(idx_flat, a, gate2, tab):
    t = a.shape[0]
    return pl.pallas_call(
        _peer_v_kernel,
        grid=(t // PEER_TB,),
        in_specs=[pl.BlockSpec((PEER_TB * PEER_K,), lambda i: (i,), memory_space=pltpu.SMEM),
                  pl.BlockSpec((PEER_TB, 2 * PEER_K), lambda i: (i, 0)),
                  pl.BlockSpec((PEER_TB, 2 * PEER_K), lambda i: (i, 0)),
                  pl.BlockSpec(memory_space=pltpu.VMEM)],
        out_specs=pl.BlockSpec((PEER_TB, D_MODEL), lambda i: (i, 0)),
        out_shape=jax.ShapeDtypeStruct((t, D_MODEL), jnp.float32),
        scratch_shapes=_peer_scratch(),
        compiler_params=_peer_params(tab),
        name="peer_v_pass",
    )(idx_flat, a, gate2, tab)


FUSED_ROW_WORDS = 2 * ROW_WORDS
FUSED_BLOCKS = 5
GROUP_PICKS = 8 * PEER_K
FUSED_CAP = FUSED_BLOCKS * PEER_K
FUSED_GROUP = 64
FUSED_DELAY = 4
FUSED_BUFS = FUSED_DELAY + 2


def _gather_fused(idx_ref, off, tab_ref, scr):
    for k in range(PEER_K):
        e = idx_ref[off + k]
        scr[k // 8, pl.ds(k % 8, FUSED_ROW_WORDS, stride=8), :] = (
            tab_ref[pl.ds(pl.multiple_of(e, FUSED_ROW_WORDS), FUSED_ROW_WORDS), :])

    def view(j0):
        gm = jnp.concatenate([scr[:, (j0 + j) * 8:(j0 + j + 1) * 8, :].reshape(PEER_K, 128)
                              for j in range(ROW_WORDS)], axis=1)
        return pltpu.bitcast(gm, jnp.bfloat16)

    return view(0), view(ROW_WORDS)


def _peer_fused_kernel(idx_ref, x_ref, tok_ref, gate_ref, tab_ref, out_ref, *scrs):
    f32, bf16 = jnp.float32, jnp.bfloat16
    shp = (8, 2 * PEER_K)
    rowf = lax.broadcasted_iota(jnp.int32, shp, 0).astype(f32)
    even = (lax.broadcasted_iota(jnp.int32, shp, 1) & 1) == 0

    def trip(g, carry):
        accs = {}
        queue = []

        def finish(p):
            lhs_v, bv, sub, base, last = p
            accs[sub] = accs[sub] + jnp.dot(lhs_v, bv, preferred_element_type=f32)
            if last:
                out_ref[pl.ds(base, 8), 0:HALF_D] = accs[sub][0:8]
                out_ref[pl.ds(base, 8), HALF_D:] = accs[sub][8:16]

        n = 0
        for sub in range(FUSED_GROUP // 8):
            base = pl.multiple_of(g * FUSED_GROUP + sub * 8, 8)
            x8 = x_ref[pl.ds(base, 8), :]
            lhs = jnp.concatenate([x8[:, :HALF_D], x8[:, HALF_D:]], axis=0).astype(bf16)
            tok8 = tok_ref[pl.ds(base, 8), :]
            gate8 = gate_ref[pl.ds(base, 8), :]
            accs[sub] = jnp.zeros((16, HALF_D), f32)
            for blk in range(FUSED_BLOCKS):
                off = base * (FUSED_CAP // 8) + blk * PEER_K
                bu, bv = _gather_fused(idx_ref, off, tab_ref, scrs[n % len(scrs)])
                n += 1
                r = lax.dot_general(lhs, bu, (((1,), (1,)), ((), ())), preferred_element_type=f32)
                if len(queue) == FUSED_DELAY:
                    finish(queue.pop(0))
                mine = tok8[blk:blk + 1, :] == rowf
                a = jnp.where(mine, jnp.where(even, r[0:8], r[8:16]), 0.0)
                s = a + pltpu.roll(a, 2 * PEER_K - 1, axis=1)
                act = jnp.where(even, s, pltpu.roll(s, 1, axis=1))
                gelu = 0.5 * act * (1.0 + lax.erf(act * (2.0 ** -0.5)))
                w = gelu * gate8[blk:blk + 1, :]
                lhs_v = jnp.concatenate([jnp.where(even, w, 0.0), jnp.where(even, 0.0, w)], axis=0).astype(bf16)
                queue.append((lhs_v, bv, sub, base, blk == FUSED_BLOCKS - 1))
        for p in queue:
            finish(p)
        return carry

    lax.fori_loop(0, PEER_TB // FUSED_GROUP, trip, 0)


def _peer_fused_half(idx_flat, un, tok2, gate2, tab):
    t = un.shape[0]
    per_step = PEER_TB // 8 * FUSED_CAP
    side = pl.BlockSpec((PEER_TB, 2 * PEER_K), lambda i: (i, 0))
    return pl.pallas_call(
        _peer_fused_kernel,
        grid=(t // PEER_TB,),
        in_specs=[pl.BlockSpec((per_step,), lambda i: (i,), memory_space=pltpu.SMEM),
                  pl.BlockSpec((PEER_TB, D_MODEL), lambda i: (i, 0)), side, side,
                  pl.BlockSpec(memory_space=pltpu.VMEM)],
        out_specs=pl.BlockSpec((PEER_TB, D_MODEL), lambda i: (i, 0)),
        out_shape=jax.ShapeDtypeStruct((t, D_MODEL), jnp.float32),
        scratch_shapes=[pltpu.VMEM((PEER_K // 8, 8 * FUSED_ROW_WORDS, 128), jnp.uint32) for _ in range(FUSED_BUFS)],
        compiler_params=_peer_params(tab),
        name="peer_fused",
    )(idx_flat, un, tok2, gate2, tab)


def _split_lists(e, gate, n_half):
    t = e.shape[0]
    g = t // 8
    half = (e >= n_half).astype(jnp.int32).reshape(g, GROUP_PICKS)
    tok = jnp.broadcast_to((jnp.arange(GROUP_PICKS, dtype=jnp.int32) // PEER_K)[None, :], (g, GROUP_PICKS))
    hs, es, ts, gs = lax.sort((half, e.reshape(g, GROUP_PICKS), tok, gate.reshape(g, GROUP_PICKS)),
                              dimension=1, is_stable=True, num_keys=1)
    n1 = jnp.sum(half, axis=1)
    overflow = jnp.any(n1 > FUSED_CAP) | jnp.any(GROUP_PICKS - n1 > FUSED_CAP)

    def rows8(a):
        a = jnp.repeat(a.reshape(g, FUSED_BLOCKS, PEER_K), 2, axis=2)
        return jnp.pad(a, ((0, 0), (0, 8 - FUSED_BLOCKS), (0, 0))).reshape(g * 8, 2 * PEER_K)

    sides = []
    for h in range(2):
        take = (lambda a: a[:, ::-1][:, :FUSED_CAP]) if h else (lambda a: a[:, :FUSED_CAP])
        valid = take(hs) == h
        row = jnp.where(valid, (take(es) - h * n_half) * FUSED_ROW_WORDS, 0)
        sides.append((row.reshape(-1), rows8(take(ts).astype(jnp.float32)),
                      rows8(jnp.where(valid, take(gs), 0.0))))
    return sides, overflow


def _pack_fused(peer_u, peer_v, lo, n):
    pu = _pack_table(peer_u[lo:lo + n]).reshape(n, ROW_WORDS, 128)
    pv = _pack_table(peer_v[lo:lo + n]).reshape(n, ROW_WORDS, 128)
    return jnp.concatenate([pu, pv], axis=1).reshape(n * FUSED_ROW_WORDS, 128)


def _peer_pallas(un, q, sub_keys, peer_u, peer_v):
    idx_t, gate_t = _peer_topk(q, sub_keys)
    e, gate = idx_t.T, gate_t.T
    n_half = peer_u.shape[0] // 2
    sides, overflow = _split_lists(e, gate, n_half)

    def fused():
        return tuple(_peer_fused_half(row, un, tok2, gate2, _pack_fused(peer_u, peer_v, h * n_half, n_half))
                     for h, (row, tok2, gate2) in enumerate(sides))

    def two_pass():
        idx_flat = (e * ROW_WORDS).reshape(-1)
        a = _peer_u_pass(idx_flat, un, _pack_table(peer_u))
        pf = _peer_v_pass(idx_flat, a, jnp.repeat(gate, 2, axis=1), _pack_table(peer_v))
        return pf, jnp.zeros_like(pf)

    return lax.cond(overflow, two_pass, fused)


def dev_stages():
    s = jax.ShapeDtypeStruct
    f32, i32, u32 = jnp.float32, jnp.int32, jnp.uint32
    t = 32768
    bf16 = jnp.bfloat16
    return {
        "mlaproj": (_mla_proj, (s((8, 4096), i32), s((t, 384), f32), s((t, 256), f32), s((t, 128), f32),
                                s((t, 128), f32), s((384,), f32), s((256,), f32), s((384, 1536), f32),
                                s((256, 2048), f32))),
        "ssd": (_ssd_pallas, (s((8, 4096, 1536), f32), s((8, 4096, 128), f32), s((8, 4096, 1024), f32),
                              s((3, 1536), f32), s((1536,), f32), s((16,), f32), s((16,), f32), s((16,), f32),
                              s((16,), f32), s((16,), f32), s((1024,), f32))),
        "inproj": (_in_proj, (s((t, 1024), f32), s((1024,), f32), s((1024, 5312), f32))),
        "merge": (_merge, (s((t, 1024), f32), s((t, 1024), f32), s((t, 1024), bf16), s((t, 2048), f32),
                           s((1024,), f32), s((1024, 1024), f32), s((1024, 1024), f32),
                           s((1024, 1024), f32), s((1024, 2048), f32))),
        "attn": (_attention, (s((8, 4096, 2048), bf16), s((8, 4096, 2048), bf16), s((8, 4096, 1024), bf16))),
        "fused": (_peer_fused_half, (s((t // 8 * FUSED_CAP,), i32), s((t, 1024), f32), s((t, 256), f32),
                                     s((t, 256), f32), s((65536, 128), u32))),
        "topk": (_peer_topk, (s((t, 2048), f32), s((2, 128, 128), f32))),
        "upass": (_peer_u_pass, (s((t * 128,), i32), s((t, 1024), f32), s((65536, 128), u32))),
        "vpass": (_peer_v_pass, (s((t * 128,), i32), s((t, 256), f32), s((t, 256), f32), s((65536, 128), u32))),
    }


def _final_norm_kernel(h_ref, pa_ref, pb_ref, w_ref, o_ref):
    hh = h_ref[...] + (pa_ref[...] + pb_ref[...])
    ms = jnp.mean(hh * hh, axis=-1, keepdims=True)
    o_ref[...] = hh * lax.rsqrt(ms + EPS) * w_ref[...]


def _final_norm(h, pa, pb, w):
    t, d = h.shape
    tb = min(512, t)
    row = pl.BlockSpec((tb, d), lambda i: (i, 0))
    return pl.pallas_call(
        _final_norm_kernel,
        grid=(t // tb,),
        in_specs=[row, row, row, pl.BlockSpec((1, d), lambda i: (0, 0))],
        out_specs=row,
        out_shape=jax.ShapeDtypeStruct((t, d), jnp.float32),
        name="final_norm",
    )(h, pa, pb, w.reshape(1, d))


def kernel(x, positions, norm_mix_w, w_in, conv_w, conv_b, dt_bias_fwd, dt_bias_bwd, a_log_fwd, a_log_bwd, d_skip, ssd_norm_w, q_a_norm_w, w_q_b, kv_a_norm_w, w_kv_b, w_proj_ssd, w_proj_mla, w_out, norm_ffn_w, peer_w_q, peer_sub_keys, peer_u, peer_v, norm_final_w):
    b, s, _ = x.shape
    x2 = x.reshape(b * s, D_MODEL)
    z, xbc, dt128, q_a, kv_a, kr, krs, gate_logits = _in_proj(x2, norm_mix_w, w_in)
    y_ssd = _ssd_pallas(xbc.reshape(b, s, -1), dt128.reshape(b, s, -1), z.reshape(b, s, -1), conv_w, conv_b,
                        dt_bias_fwd, dt_bias_bwd, a_log_fwd, a_log_bwd, d_skip, ssd_norm_w)

    y_mla = _mla_pallas(positions, q_a, kv_a, kr, krs, q_a_norm_w, kv_a_norm_w, w_q_b, w_kv_b)

    h2, un, q_peer = _merge(x2, y_ssd.reshape(b * s, -1), y_mla, gate_logits, norm_ffn_w,
                            w_proj_ssd, w_proj_mla, w_out, peer_w_q)
    pf_a, pf_b = _peer_pallas(un, q_peer, peer_sub_keys, peer_u, peer_v)
    out = _final_norm(h2, pf_a, pf_b, norm_final_w)
    return out.reshape(b, s, D_MODEL)
```

```python
import functools

import jax
import jax.numpy as jnp
from jax import lax
from jax.experimental import pallas as pl
from jax.experimental.pallas import tpu as pltpu

D_MODEL = 1024
SSD_D_INNER = 1024
SSD_HEADDIM = 64
SSD_HEADS = 16
SSD_GROUPS = 2
SSD_HPG = 8
SSD_STATE = 128
CHUNK = 128
SSD_CONV_DIM = SSD_D_INNER + 2 * SSD_GROUPS * SSD_STATE
MLA_HEADS = 16
QK_NOPE = 64
QK_ROPE = 32
V_HEAD = 64
Q_LORA = 384
KV_LORA = 256
ROPE_THETA = 10000.0
PEER_HEADS = 8
N_KEYS = 128
KEY_HALF = 128
PEER_TOPK = 16
EPS = 1e-6
HEAD_PAD = 128
QK_DIM = QK_NOPE + QK_ROPE
LOG2E = 1.4426950408889634
IN_SIZES = (SSD_D_INNER, SSD_CONV_DIM, 2 * SSD_HEADS, Q_LORA, KV_LORA, QK_ROPE, 2 * D_MODEL)


def _split_points(sizes):
    pts, acc = [], 0
    for s in sizes[:-1]:
        acc += s
        pts.append(acc)
    return pts


def _rms_rows(x, w):
    return x * lax.rsqrt(jnp.mean(x * x, axis=-1, keepdims=True) + EPS) * w


IN_TB = 256
DT_PAD = 128
GATE_W = 2 * D_MODEL
IN_OUT_WIDTHS = (SSD_D_INNER, SSD_CONV_DIM, DT_PAD, Q_LORA, KV_LORA, HEAD_PAD, HEAD_PAD, GATE_W)


def _in_proj_kernel(x_ref, nw_ref, w_ref, *out_refs):
    u = _rms_rows(x_ref[...], nw_ref[...]).astype(jnp.bfloat16)
    col = 0
    for o_ref, width in zip(out_refs, IN_OUT_WIDTHS):
        o_ref[...] = jnp.dot(u, w_ref[:, col:col + width], preferred_element_type=jnp.float32)
        col += width


def _arrange_w_in(w_in):
    z, xbc, dt, qa, kva, kr, gates = jnp.split(w_in, _split_points(IN_SIZES), axis=1)
    d = w_in.shape[0]
    half = QK_ROPE // 2
    dt_p = jnp.pad(dt, ((0, 0), (0, DT_PAD - dt.shape[1])))
    kr3 = kr.reshape(d, 1, QK_ROPE)
    kr_p = _pad_heads(None, kr3)
    krs_p = _pad_heads(None, _rope_partner(kr3[..., :half], kr3[..., half:]))
    return jnp.concatenate([z, xbc, dt_p, qa, kva, kr_p, krs_p, gates], axis=1).astype(jnp.bfloat16)


def _in_proj(x2, norm_w, w_in):
    t, d = x2.shape
    w = _arrange_w_in(w_in)
    tb = min(IN_TB, t)
    return pl.pallas_call(
        _in_proj_kernel,
        grid=(t // tb,),
        in_specs=[pl.BlockSpec((tb, d), lambda i: (i, 0)),
                  pl.BlockSpec((1, d), lambda i: (0, 0)),
                  pl.BlockSpec(w.shape, lambda i: (0, 0))],
        out_specs=[pl.BlockSpec((tb, n), lambda i: (i, 0)) for n in IN_OUT_WIDTHS],
        out_shape=[jax.ShapeDtypeStruct((t, n), jnp.float32) for n in IN_OUT_WIDTHS],
        compiler_params=pltpu.CompilerParams(dimension_semantics=("arbitrary",),
                                             vmem_limit_bytes=56 << 20),
        name="in_proj",
    )(x2, norm_w.reshape(1, d), w)


HEAD_W = SSD_HEADS * SSD_HEADDIM
GROUP_W = HEAD_W // SSD_GROUPS
BC_W = SSD_GROUPS * SSD_STATE
_HI = lax.Precision.HIGHEST


def _ssd_kernel(direction, xc_ref, xp_ref, xn_ref, dt_ref, cw_ref, cb_ref, dtb_ref, a_ref, e_ref,
                tri_ref, *rest):
    f32, bf16 = jnp.float32, jnp.bfloat16
    fwd = direction == 0
    if fwd:
        dsk_ref, y_ref, h_scr = rest
    else:
        y1_ref, z_ref, nw_ref, y_ref, h_scr = rest
    c = pl.program_id(1)
    nc = pl.num_programs(1)
    cm = c if fwd else nc - 1 - c

    @pl.when(c == 0)
    def _():
        h_scr[...] = jnp.zeros_like(h_scr)

    xc = xc_ref[0]
    row = lax.broadcasted_iota(jnp.int32, xc.shape, 0)
    prev_row = jnp.where(cm > 0, xp_ref[0, 7:8, :], 0.0)
    next_row = jnp.where(cm < nc - 1, xn_ref[0, 0:1, :], 0.0)
    x_m1 = jnp.where(row == 0, prev_row, pltpu.roll(xc, 1, axis=0))
    x_p1 = jnp.where(row == CHUNK - 1, next_row, pltpu.roll(xc, CHUNK - 1, axis=0))
    conv = cw_ref[0:1, :] * x_m1 + cw_ref[1:2, :] * xc + cw_ref[2:3, :] * x_p1 + cb_ref[...]
    act = conv * jax.nn.sigmoid(conv)
    xs = act[:, :HEAD_W]

    dtr = dt_ref[0] + dtb_ref[...]
    dt = jnp.maximum(dtr, 0.0) + jnp.log(1.0 + jnp.exp(-jnp.abs(dtr)))
    a = dt * a_ref[...]
    cs = jnp.dot(tri_ref[...], a, precision=_HI, preferred_element_type=f32)
    expand = lambda v: jnp.dot(v, e_ref[...], precision=_HI, preferred_element_type=f32)
    cs_e = expand(cs)
    tot_e = cs_e[CHUNK - 1:CHUNK, :]
    if fwd:
        pos = cs
        w_state = jnp.exp(tot_e - cs_e)
        out_scale = jnp.exp(cs_e)
    else:
        pos = cs - a
        e_e = cs_e - expand(a)
        w_state = jnp.exp(e_e)
        out_scale = jnp.exp(tot_e - e_e)
    pos_t = pos.T
    xd = xs * expand(dt)
    xdb = xd.astype(bf16)
    xw = (xd * w_state).astype(bf16)
    chunk_decay = jnp.exp(tot_e)

    li = lax.broadcasted_iota(jnp.int32, (CHUNK, CHUNK), 0)
    si = lax.broadcasted_iota(jnp.int32, (CHUNK, CHUNK), 1)
    mask = (si <= li) if fwd else (si >= li)
    for g in range(SSD_GROUPS):
        gsl = slice(g * GROUP_W, (g + 1) * GROUP_W)
        bg = act[:, HEAD_W + g * SSD_STATE:HEAD_W + (g + 1) * SSD_STATE]
        cg = act[:, HEAD_W + BC_W + g * SSD_STATE:HEAD_W + BC_W + (g + 1) * SSD_STATE].astype(bf16)
        cb = lax.dot_general(cg, bg.astype(bf16), (((1,), (1,)), ((), ())), preferred_element_type=f32)
        h_in = h_scr[:, gsl]
        y_off = jnp.dot(cg, h_in.astype(bf16), preferred_element_type=f32) * out_scale[:, gsl]
        s_new = jnp.dot(bg.T.astype(bf16), xw[:, gsl], preferred_element_type=f32)
        h_scr[:, gsl] = h_in * chunk_decay[:, gsl] + s_new
        for r in range(SSD_HPG):
            hd = g * SSD_HPG + r
            lane = direction * SSD_HEADS + hd
            col = pos[:, lane:lane + 1]
            rw = pos_t[lane:lane + 1, :]
            diff = (col - rw) if fwd else (rw - col)
            m = (cb * jnp.exp(jnp.where(mask, diff, NEG_INF))).astype(bf16)
            hsl = slice(hd * SSD_HEADDIM, (hd + 1) * SSD_HEADDIM)
            y_d = jnp.dot(m, xdb[:, hsl], preferred_element_type=f32)
            y_ref[0, :, hsl] = y_d + y_off[:, r * SSD_HEADDIM:(r + 1) * SSD_HEADDIM]

    if fwd:
        y_ref[0] = y_ref[0] + xs * dsk_ref[...]
    else:
        yg = (y_ref[0] + y1_ref[0]) * (z_ref[0] * jax.nn.sigmoid(z_ref[0]))
        for g in range(SSD_GROUPS):
            gsl = slice(g * GROUP_W, (g + 1) * GROUP_W)
            y_ref[0, :, gsl] = _rms_rows(yg[:, gsl], nw_ref[:, gsl])


def _ssd_pass(direction, xbc, dt128, consts, extra):
    b, s, cw_ = xbc.shape
    nc = s // CHUNK
    per8 = CHUNK // 8
    cm = (lambda c: c) if direction == 0 else (lambda c: nc - 1 - c)
    cur = lambda n: pl.BlockSpec((1, CHUNK, n), lambda bi, c: (bi, cm(c), 0))
    prev = pl.BlockSpec((1, 8, cw_), lambda bi, c: (bi, jnp.maximum(cm(c) * per8 - 1, 0), 0))
    nxt = pl.BlockSpec((1, 8, cw_), lambda bi, c: (bi, jnp.minimum((cm(c) + 1) * per8, s // 8 - 1), 0))
    full = lambda a: pl.BlockSpec(a.shape, lambda bi, c: (0,) * a.ndim)
    if direction == 0:
        extra_specs = [full(extra[0])]
    else:
        extra_specs = [cur(HEAD_W), cur(HEAD_W), full(extra[2])]
    return pl.pallas_call(
        functools.partial(_ssd_kernel, direction),
        grid=(b, nc),
        in_specs=[cur(cw_), prev, nxt, cur(DT_PAD)] + [full(a) for a in consts] + extra_specs,
        out_specs=cur(HEAD_W),
        out_shape=jax.ShapeDtypeStruct((b, s, HEAD_W), jnp.float32),
        scratch_shapes=[pltpu.VMEM((SSD_STATE, HEAD_W), jnp.float32)],
        compiler_params=pltpu.CompilerParams(dimension_semantics=("arbitrary", "arbitrary"),
                                             vmem_limit_bytes=48 << 20),
        name="ssd_fwd" if direction == 0 else "ssd_bwd",
    )(xbc, xbc, xbc, dt128, *consts, *extra)


def _ssd_pallas(xbc, dt128, z, conv_w, conv_b, dt_bias_fwd, dt_bias_bwd, a_log_fwd, a_log_bwd,
                d_skip, ssd_norm_w):
    f32 = jnp.float32
    pad = lambda v: jnp.pad(v, (0, DT_PAD - v.shape[0])).reshape(1, DT_PAD)
    dtb = pad(jnp.concatenate([dt_bias_fwd, dt_bias_bwd]))
    a_neg = pad(-jnp.exp(jnp.concatenate([a_log_fwd, a_log_bwd])))
    tri = (jnp.arange(CHUNK)[None, :] <= jnp.arange(CHUNK)[:, None]).astype(f32)
    head_of_lane = jnp.arange(HEAD_W) // SSD_HEADDIM
    y = None
    for direction in range(2):
        e = (jnp.arange(DT_PAD)[:, None] == (direction * SSD_HEADS + head_of_lane)[None, :]).astype(f32)
        consts = (conv_w, conv_b.reshape(1, -1), dtb, a_neg, e, tri)
        if direction == 0:
            extra = (jnp.repeat(d_skip, SSD_HEADDIM).reshape(1, HEAD_W),)
        else:
            extra = (y, z, ssd_norm_w.reshape(1, HEAD_W))
        y = _ssd_pass(direction, xbc, dt128, consts, extra)
    return y


MERGE_TB = 256


def _merge_kernel(x_ref, ys_ref, ym_ref, g_ref, nw_ref, pa_ref, pb_ref, wo_ref, wq_ref,
                  h_ref, un_ref, q_ref):
    bf16 = jnp.bfloat16
    f32 = jnp.float32
    g = jax.nn.sigmoid(g_ref[...])
    ma = jnp.dot(ys_ref[...].astype(bf16), pa_ref[...], preferred_element_type=f32)
    mb = jnp.dot(ym_ref[...].astype(bf16), pb_ref[...], preferred_element_type=f32)
    m = g[:, :D_MODEL] * ma + g[:, D_MODEL:] * mb
    h = x_ref[...] + jnp.dot(m.astype(bf16), wo_ref[...], preferred_element_type=f32)
    h_ref[...] = h
    un = _rms_rows(h, nw_ref[...])
    un_ref[...] = un
    q_ref[...] = jnp.dot(un.astype(bf16), wq_ref[...], preferred_element_type=f32)


def _merge(x2, y_ssd, y_mla, gates, norm_ffn_w, w_proj_ssd, w_proj_mla, w_out, peer_w_q):
    t, d = x2.shape
    bf16 = jnp.bfloat16
    tb = min(MERGE_TB, t)
    nq = peer_w_q.shape[1]
    row = lambda n: pl.BlockSpec((tb, n), lambda i: (i, 0))
    full = lambda a: pl.BlockSpec(a.shape, lambda i: (0, 0))
    ws = [w.astype(bf16) for w in (w_proj_ssd, w_proj_mla, w_out, peer_w_q)]
    nw = norm_ffn_w.reshape(1, d)
    return pl.pallas_call(
        _merge_kernel,
        grid=(t // tb,),
        in_specs=[row(d), row(d), row(d), row(GATE_W), full(nw)] + [full(w) for w in ws],
        out_specs=[row(d), row(d), row(nq)],
        out_shape=[jax.ShapeDtypeStruct((t, d), jnp.float32), jax.ShapeDtypeStruct((t, d), jnp.float32),
                   jax.ShapeDtypeStruct((t, nq), jnp.float32)],
        compiler_params=pltpu.CompilerParams(dimension_semantics=("arbitrary",),
                                             vmem_limit_bytes=56 << 20),
        name="merge",
    )(x2, y_ssd, y_mla, gates, nw, *ws)


MLA_TB = 512
ATT_TQ = 256
ATT_HEADS = 4


def _mla_proj_kernel(pos_ref, qa_ref, kva_ref, kr_ref, krs_ref, qnw_ref, kvnw_ref, inv_ref,
                     wq_ref, wqs_ref, wk_ref, wv_ref, q_out, k_out, v_out):
    bf16 = jnp.bfloat16
    ang = pos_ref[...].astype(jnp.float32) * inv_ref[...]
    cos, sin = jnp.cos(ang), jnp.sin(ang)
    qa = _rms_rows(qa_ref[...], qnw_ref[...]).astype(bf16)
    q = jnp.dot(qa, wq_ref[...], preferred_element_type=jnp.float32)
    qs = jnp.dot(qa, wqs_ref[...], preferred_element_type=jnp.float32)
    kva = _rms_rows(kva_ref[...], kvnw_ref[...]).astype(bf16)
    k = jnp.dot(kva, wk_ref[...], preferred_element_type=jnp.float32)
    v_out[...] = jnp.dot(kva, wv_ref[...], preferred_element_type=jnp.float32).astype(bf16)
    kro = kr_ref[...] * cos + krs_ref[...] * sin
    c = (QK_DIM ** -0.5) * LOG2E
    for h in range(MLA_HEADS):
        sl = slice(h * HEAD_PAD, (h + 1) * HEAD_PAD)
        q_out[:, sl] = ((q[:, sl] * cos + qs[:, sl] * sin) * c).astype(bf16)
        k_out[:, sl] = (k[:, sl] + kro).astype(bf16)


def _rope_partner(w1, w2):
    return jnp.concatenate([-w2, w1], axis=-1)


def _pad_heads(nope, rope):
    ref = nope if nope is not None else rope
    lead = ref.shape[:-1]
    z = lambda n: jnp.zeros(lead + (n,), ref.dtype)
    parts = [nope if nope is not None else z(QK_NOPE), rope if rope is not None else z(QK_ROPE),
             z(HEAD_PAD - QK_DIM)]
    out = jnp.concatenate(parts, axis=-1)
    return out.reshape(out.shape[:-2] + (out.shape[-2] * HEAD_PAD,))


def _mla_proj(positions, q_a, kv_a, kr, krs, q_a_norm_w, kv_a_norm_w, w_q_b, w_kv_b):
    t = q_a.shape[0]
    bf16 = jnp.bfloat16
    half = QK_ROPE // 2
    wq = w_q_b.reshape(Q_LORA, MLA_HEADS, QK_DIM)
    wq_n, wq_r = wq[..., :QK_NOPE], wq[..., QK_NOPE:]
    wq_main = _pad_heads(wq_n, wq_r).astype(bf16)
    wq_part = _pad_heads(None, _rope_partner(wq_r[..., :half], wq_r[..., half:])).astype(bf16)
    wkv = w_kv_b.reshape(KV_LORA, MLA_HEADS, QK_NOPE + V_HEAD)
    wk = _pad_heads(wkv[..., :QK_NOPE], None).astype(bf16)
    wv = wkv[..., QK_NOPE:].reshape(KV_LORA, MLA_HEADS * V_HEAD).astype(bf16)
    inv = 1.0 / (ROPE_THETA ** (jnp.arange(0, QK_ROPE, 2, dtype=jnp.float32) / QK_ROPE))
    inv_lane = jnp.concatenate([jnp.zeros((QK_NOPE,), jnp.float32), inv, inv,
                                jnp.zeros((HEAD_PAD - QK_DIM,), jnp.float32)]).reshape(1, HEAD_PAD)
    tb = min(MLA_TB, t)
    row = lambda n: pl.BlockSpec((tb, n), lambda i: (i, 0))
    full = lambda a: pl.BlockSpec(a.shape, lambda i: (0,) * a.ndim)
    qn = q_a_norm_w.reshape(1, Q_LORA)
    kvn = kv_a_norm_w.reshape(1, KV_LORA)
    args = (positions.reshape(t, 1), q_a, kv_a, kr, krs, qn, kvn, inv_lane, wq_main, wq_part, wk, wv)
    in_specs = [row(1), row(Q_LORA), row(KV_LORA), row(HEAD_PAD), row(HEAD_PAD)] + [full(a) for a in args[5:]]
    nq = MLA_HEADS * HEAD_PAD
    return pl.pallas_call(
        _mla_proj_kernel,
        grid=(t // tb,),
        in_specs=in_specs,
        out_specs=[row(nq), row(nq), row(MLA_HEADS * V_HEAD)],
        out_shape=[jax.ShapeDtypeStruct((t, nq), bf16), jax.ShapeDtypeStruct((t, nq), bf16),
                   jax.ShapeDtypeStruct((t, MLA_HEADS * V_HEAD), bf16)],
        compiler_params=pltpu.CompilerParams(dimension_semantics=("arbitrary",),
                                             vmem_limit_bytes=48 << 20),
        name="mla_proj",
    )(*args)


def _attn_kernel(q_ref, k_ref, v_ref, o_ref):
    tq = q_ref.shape[1]
    lane = lax.broadcasted_iota(jnp.int32, (tq, 2 * V_HEAD), 1)
    for pair in range(ATT_HEADS // 2):
        vsl = slice(pair * 2 * V_HEAD, (pair + 1) * 2 * V_HEAD)
        v = v_ref[0, :, vsl]
        outs = []
        for hh in range(2 * pair, 2 * pair + 2):
            sl = slice(hh * HEAD_PAD, (hh + 1) * HEAD_PAD)
            s = lax.dot_general(q_ref[0, :, sl], k_ref[0, :, sl], (((1,), (1,)), ((), ())),
                                preferred_element_type=jnp.float32)
            m = jnp.max(s, axis=-1, keepdims=True)
            p = jnp.exp2(s - m)
            l = jnp.sum(p, axis=-1, keepdims=True)
            pv = jnp.dot(p.astype(jnp.bfloat16), v, preferred_element_type=jnp.float32)
            outs.append(pv / l)
        o_ref[0, :, vsl] = jnp.where(lane < V_HEAD, outs[0], outs[1]).astype(o_ref.dtype)


def _attention(q, k, v):
    b, s, _ = q.shape
    tq = min(ATT_TQ, s)
    qw, vw = ATT_HEADS * HEAD_PAD, ATT_HEADS * V_HEAD
    return pl.pallas_call(
        _attn_kernel,
        grid=(b, MLA_HEADS // ATT_HEADS, s // tq),
        in_specs=[pl.BlockSpec((1, tq, qw), lambda bi, j, i: (bi, i, j)),
                  pl.BlockSpec((1, s, qw), lambda bi, j, i: (bi, 0, j)),
                  pl.BlockSpec((1, s, vw), lambda bi, j, i: (bi, 0, j))],
        out_specs=pl.BlockSpec((1, tq, vw), lambda bi, j, i: (bi, i, j)),
        out_shape=jax.ShapeDtypeStruct((b, s, MLA_HEADS * V_HEAD), jnp.bfloat16),
        compiler_params=pltpu.CompilerParams(dimension_semantics=("arbitrary",) * 3,
                                             vmem_limit_bytes=56 << 20),
        name="mla_attention",
    )(q, k, v)


def _mla_pallas(positions, q_a, kv_a, kr, krs, q_a_norm_w, kv_a_norm_w, w_q_b, w_kv_b):
    b, s = positions.shape
    q, k, v = _mla_proj(positions, q_a, kv_a, kr, krs, q_a_norm_w, kv_a_norm_w, w_q_b, w_kv_b)
    y = _attention(q.reshape(b, s, -1), k.reshape(b, s, -1), v.reshape(b, s, -1))
    return y.reshape(b * s, -1)


PEER_TB = 128
PEER_K = PEER_HEADS * PEER_TOPK
PEER_GROUP = 32
GATHER_TB = 256
HALF_D = D_MODEL // 2
NEG_INF = float("-inf")


def _top16_rows(xs, extras=None):
    n = len(xs)
    rows = xs[0].shape[0]
    iota = lax.broadcasted_iota(jnp.int32, xs[0].shape, 0).astype(jnp.float32)
    xs = list(xs)
    vals, poss, exts = [[] for _ in xs], [[] for _ in xs], [[] for _ in xs]
    for _ in range(PEER_TOPK):
        for j in range(n):
            m = jnp.max(xs[j], axis=0, keepdims=True)
            p = jnp.min(jnp.where(xs[j] == m, iota, float(rows)), axis=0, keepdims=True)
            hit = iota == p
            if extras is not None:
                exts[j].append(jnp.sum(jnp.where(hit, extras[j], 0.0), axis=0, keepdims=True))
            xs[j] = jnp.where(hit, NEG_INF, xs[j])
            vals[j].append(m)
            poss[j].append(p)
    cat = lambda parts: jnp.concatenate(parts, axis=0)
    return [(cat(vals[j]), cat(poss[j]), cat(exts[j]) if extras is not None else None) for j in range(n)]


_CAND_PAIRS = [(a, b) for a in range(PEER_TOPK) for b in range(PEER_TOPK) if (a + 1) * (b + 1) <= PEER_TOPK]
CAND_ROWS = -(-len(_CAND_PAIRS) // 8) * 8
ROW_WORDS = 4
TOPK_LOCKSTEP_HEADS = 4


def _cand_select():
    g0 = [[0.0] * PEER_TOPK for _ in range(CAND_ROWS)]
    g1 = [[0.0] * PEER_TOPK for _ in range(CAND_ROWS)]
    for r, (a, b) in enumerate(_CAND_PAIRS):
        g0[r][a] = 1.0
        g1[r][b] = 1.0
    return jnp.array(g0, jnp.float32), jnp.array(g1, jnp.float32)


def _peer_topk_kernel(q_ref, keys_ref, g0_ref, g1_ref, idx_ref, gate_ref):
    f32 = jnp.float32
    pick = lambda g_ref, v: jnp.dot(g_ref[...], v, precision=lax.Precision.HIGHEST, preferred_element_type=f32)
    pad_row = lax.broadcasted_iota(jnp.int32, (CAND_ROWS, q_ref.shape[0]), 0) >= len(_CAND_PAIRS)
    for h0 in range(0, PEER_HEADS, TOPK_LOCKSTEP_HEADS):
        heads = range(h0, h0 + TOPK_LOCKSTEP_HEADS)
        scores = []
        for h in heads:
            for i in range(2):
                c0 = (h * 2 + i) * KEY_HALF
                scores.append(lax.dot_general(keys_ref[i], q_ref[:, c0:c0 + KEY_HALF], (((1,), (1,)), ((), ())),
                                              preferred_element_type=f32))
        tops = _top16_rows(scores)
        cands, cidxs = [], []
        for j in range(len(heads)):
            (v0, p0, _), (v1, p1, _) = tops[2 * j], tops[2 * j + 1]
            cands.append(jnp.where(pad_row, NEG_INF, pick(g0_ref, v0) + pick(g1_ref, v1)))
            cidxs.append((pick(g0_ref, p0) * N_KEYS + pick(g1_ref, p1)) * ROW_WORDS)
        for h, (best, _, eidx) in zip(heads, _top16_rows(cands, cidxs)):
            e = jnp.exp(best - best[0:1, :])
            gate = e / jnp.sum(e, axis=0, keepdims=True)
            idx_ref[h * PEER_TOPK:(h + 1) * PEER_TOPK, :] = eidx.astype(jnp.int32)
            gate_ref[h * PEER_TOPK:(h + 1) * PEER_TOPK, :] = gate


def _peer_topk(q, sub_keys):
    t = q.shape[0]
    g0, g1 = _cand_select()
    return pl.pallas_call(
        _peer_topk_kernel,
        grid=(t // PEER_TB,),
        in_specs=[pl.BlockSpec((PEER_TB, q.shape[1]), lambda i: (i, 0)),
                  pl.BlockSpec(sub_keys.shape, lambda i: (0, 0, 0)),
                  pl.BlockSpec(g0.shape, lambda i: (0, 0)),
                  pl.BlockSpec(g1.shape, lambda i: (0, 0))],
        out_specs=[pl.BlockSpec((PEER_K, PEER_TB), lambda i: (0, i)),
                   pl.BlockSpec((PEER_K, PEER_TB), lambda i: (0, i))],
        out_shape=[jax.ShapeDtypeStruct((PEER_K, t), jnp.int32),
                   jax.ShapeDtypeStruct((PEER_K, t), jnp.float32)],
        name="peer_topk",
    )(q, sub_keys, g0, g1)


def _pack_table(tab):
    e = tab.shape[0]
    tb = tab.astype(jnp.bfloat16)
    lo = lax.bitcast_convert_type(tb[:, :HALF_D], jnp.uint16).astype(jnp.uint32)
    hi = lax.bitcast_convert_type(tb[:, HALF_D:], jnp.uint16).astype(jnp.uint32)
    return (lo | (hi << 16)).reshape(e * ROW_WORDS, 128)


def _token_groups(eight_tokens):
    def trip(g, carry):
        for sub in range(PEER_GROUP // 8):
            eight_tokens(pl.multiple_of(g * PEER_GROUP + sub * 8, 8))
        return carry

    lax.fori_loop(0, GATHER_TB // PEER_GROUP, trip, 0)


def _gather_rows(idx_ref, tok, tab_ref, scr):
    nchunk = HALF_D // 128
    for k in range(PEER_K):
        e = idx_ref[tok * PEER_K + k]
        scr[k // 8, pl.ds(k % 8, nchunk, stride=8), :] = tab_ref[pl.ds(pl.multiple_of(e, ROW_WORDS), ROW_WORDS), :]
    gm = jnp.concatenate(
        [scr[:, j * 8:(j + 1) * 8, :].reshape(PEER_K, 128) for j in range(nchunk)], axis=1)
    return pltpu.bitcast(gm, jnp.bfloat16)


def _peer_u_kernel(idx_ref, ut_ref, tab_ref, out_ref, scr0, scr1):
    shp = (8, 2 * PEER_K)
    row = lax.broadcasted_iota(jnp.int32, shp, 0)
    even = (lax.broadcasted_iota(jnp.int32, shp, 1) & 1) == 0

    def eight_tokens(base):
        ut8 = ut_ref[pl.ds(base, 8), :]
        lhs = jnp.concatenate([ut8[:, :HALF_D], ut8[:, HALF_D:]], axis=0).astype(jnp.bfloat16)
        acc = jnp.zeros(shp, jnp.float32)
        for i in range(8):
            b = _gather_rows(idx_ref, base + i, tab_ref, (scr0, scr1)[i % 2])
            r = lax.dot_general(lhs, b, (((1,), (1,)), ((), ())),
                                preferred_element_type=jnp.float32)
            acc = jnp.where(row == i, jnp.where(even, r[0:8], r[8:16]), acc)
        out_ref[pl.ds(base, 8), :] = acc

    _token_groups(eight_tokens)


def _peer_v_kernel(idx_ref, a_ref, gate_ref, tab_ref, out_ref, scr0, scr1):
    shp = (8, 2 * PEER_K)
    row = lax.broadcasted_iota(jnp.int32, (8, HALF_D), 0)
    even = (lax.broadcasted_iota(jnp.int32, shp, 1) & 1) == 0

    def eight_tokens(base):
        a8 = a_ref[pl.ds(base, 8), :]
        s = a8 + pltpu.roll(a8, 2 * PEER_K - 1, axis=1)
        act = jnp.where(even, s, pltpu.roll(s, 1, axis=1))
        gelu = 0.5 * act * (1.0 + lax.erf(act * (2.0 ** -0.5)))
        w = gelu * gate_ref[pl.ds(base, 8), :]
        lhs = jnp.concatenate([jnp.where(even, w, 0.0), jnp.where(even, 0.0, w)],
                              axis=0).astype(jnp.bfloat16)
        acc_lo = jnp.zeros((8, HALF_D), jnp.float32)
        acc_hi = jnp.zeros((8, HALF_D), jnp.float32)
        for i in range(8):
            b = _gather_rows(idx_ref, base + i, tab_ref, (scr0, scr1)[i % 2])
            r = jnp.dot(lhs, b, preferred_element_type=jnp.float32)
            acc_lo = jnp.where(row == i, r[0:8], acc_lo)
            acc_hi = jnp.where(row == i, r[8:16], acc_hi)
        out_ref[pl.ds(base, 8), 0:HALF_D] = acc_lo
        out_ref[pl.ds(base, 8), HALF_D:] = acc_hi

    _token_groups(eight_tokens)


def _peer_scratch():
    return [pltpu.VMEM((PEER_K // 8, 8 * (HALF_D // 128), 128), jnp.uint32) for _ in range(2)]


def _peer_params(tab):
    table_bytes = tab.size * 4
    return pltpu.CompilerParams(dimension_semantics=("arbitrary",),
                                vmem_limit_bytes=table_bytes + (16 << 20))


def _peer_u_pass(idx_flat, un, tab):
    t = un.shape[0]
    return pl.pallas_call(
        _peer_u_kernel,
        grid=(t // GATHER_TB,),
        in_specs=[pl.BlockSpec((GATHER_TB * PEER_K,), lambda i: (i,), memory_space=pltpu.SMEM),
                  pl.BlockSpec((GATHER_TB, D_MODEL), lambda i: (i, 0)),
                  pl.BlockSpec(memory_space=pltpu.VMEM)],
        out_specs=pl.BlockSpec((GATHER_TB, 2 * PEER_K), lambda i: (i, 0)),
        out_shape=jax.ShapeDtypeStruct((t, 2 * PEER_K), jnp.float32),
        scratch_shapes=_peer_scratch(),
        compiler_params=_peer_params(tab),
        name="peer_u_pass",
    )(idx_flat, un, tab)


def _peer_v_pass(idx_flat, a, gate2, tab):
    t = a.shape[0]
    return pl.pallas_call(
        _peer_v_kernel,
        grid=(t // GATHER_TB,),
        in_specs=[pl.BlockSpec((GATHER_TB * PEER_K,), lambda i: (i,), memory_space=pltpu.SMEM),
                  pl.BlockSpec((GATHER_TB, 2 * PEER_K), lambda i: (i, 0)),
                  pl.BlockSpec((GATHER_TB, 2 * PEER_K), lambda i: (i, 0)),
                  pl.BlockSpec(memory_space=pltpu.VMEM)],
        out_specs=pl.BlockSpec((GATHER_TB, D_MODEL), lambda i: (i, 0)),
        out_shape=jax.ShapeDtypeStruct((t, D_MODEL), jnp.float32),
        scratch_shapes=_peer_scratch(),
        compiler_params=_peer_params(tab),
        name="peer_v_pass",
    )(idx_flat, a, gate2, tab)


def _peer_pallas(un, q, sub_keys, peer_u, peer_v):
    idx_t, gate_t = _peer_topk(q, sub_keys)
    idx_flat = idx_t.T.reshape(-1)
    gate2 = jnp.repeat(gate_t.T, 2, axis=1)
    a = _peer_u_pass(idx_flat, un, _pack_table(peer_u))
    return _peer_v_pass(idx_flat, a, gate2, _pack_table(peer_v))


FINAL_TB = 512


def _final_norm_kernel(h_ref, p_ref, w_ref, o_ref):
    o_ref[...] = _rms_rows(h_ref[...] + p_ref[...], w_ref[...])


def _final_norm(h, p, w):
    t, d = h.shape
    tb = min(FINAL_TB, t)
    row = pl.BlockSpec((tb, d), lambda i: (i, 0))
    return pl.pallas_call(
        _final_norm_kernel,
        grid=(t // tb,),
        in_specs=[row, row, pl.BlockSpec((1, d), lambda i: (0, 0))],
        out_specs=row,
        out_shape=jax.ShapeDtypeStruct((t, d), jnp.float32),
        name="final_norm",
    )(h, p, w.reshape(1, d))


def kernel(x, positions, norm_mix_w, w_in, conv_w, conv_b, dt_bias_fwd, dt_bias_bwd, a_log_fwd, a_log_bwd, d_skip, ssd_norm_w, q_a_norm_w, w_q_b, kv_a_norm_w, w_kv_b, w_proj_ssd, w_proj_mla, w_out, norm_ffn_w, peer_w_q, peer_sub_keys, peer_u, peer_v, norm_final_w):
    b, s, _ = x.shape
    x2 = x.reshape(b * s, D_MODEL)
    z, xbc, dt128, q_a, kv_a, kr, krs, gate_logits = _in_proj(x2, norm_mix_w, w_in)
    y_ssd = _ssd_pallas(xbc.reshape(b, s, -1), dt128.reshape(b, s, -1), z.reshape(b, s, -1), conv_w, conv_b,
                        dt_bias_fwd, dt_bias_bwd, a_log_fwd, a_log_bwd, d_skip, ssd_norm_w)

    y_mla = _mla_pallas(positions, q_a, kv_a, kr, krs, q_a_norm_w, kv_a_norm_w, w_q_b, w_kv_b)

    h2, un, q_peer = _merge(x2, y_ssd.reshape(b * s, -1), y_mla, gate_logits, norm_ffn_w,
                            w_proj_ssd, w_proj_mla, w_out, peer_w_q)
    pf = _peer_pallas(un, q_peer, peer_sub_keys, peer_u, peer_v)
    out = _final_norm(h2, pf, norm_final_w)
    return out.reshape(b, s, D_MODEL)
```

```python
import functools

import jax
import jax.numpy as jnp
from jax import lax
from jax.experimental import pallas as pl
from jax.experimental.pallas import tpu as pltpu

D_MODEL = 1024
SSD_D_INNER = 1024
SSD_HEADDIM = 64
SSD_HEADS = 16
SSD_GROUPS = 2
SSD_HPG = 8
SSD_STATE = 128
CHUNK = 128
SSD_CONV_DIM = SSD_D_INNER + 2 * SSD_GROUPS * SSD_STATE
MLA_HEADS = 16
QK_NOPE = 64
QK_ROPE = 32
V_HEAD = 64
Q_LORA = 384
KV_LORA = 256
ROPE_THETA = 10000.0
PEER_HEADS = 8
N_KEYS = 128
KEY_HALF = 128
PEER_TOPK = 16
EPS = 1e-6
HEAD_PAD = 128
QK_DIM = QK_NOPE + QK_ROPE
LOG2E = 1.4426950408889634
IN_SIZES = (SSD_D_INNER, SSD_CONV_DIM, 2 * SSD_HEADS, Q_LORA, KV_LORA, QK_ROPE, 2 * D_MODEL)


def _split_points(sizes):
    pts, acc = [], 0
    for s in sizes[:-1]:
        acc += s
        pts.append(acc)
    return pts


def _rms_rows(x, w):
    return x * lax.rsqrt(jnp.mean(x * x, axis=-1, keepdims=True) + EPS) * w


IN_TB = 256
DT_PAD = 128
GATE_W = 2 * D_MODEL
IN_OUT_WIDTHS = (SSD_D_INNER, SSD_CONV_DIM, DT_PAD, Q_LORA, KV_LORA, HEAD_PAD, HEAD_PAD, GATE_W)


def _in_proj_kernel(x_ref, nw_ref, w_ref, *out_refs):
    u = _rms_rows(x_ref[...], nw_ref[...]).astype(jnp.bfloat16)
    col = 0
    for o_ref, width in zip(out_refs, IN_OUT_WIDTHS):
        o_ref[...] = jnp.dot(u, w_ref[:, col:col + width], preferred_element_type=jnp.float32)
        col += width


def _arrange_w_in(w_in):
    z, xbc, dt, qa, kva, kr, gates = jnp.split(w_in, _split_points(IN_SIZES), axis=1)
    d = w_in.shape[0]
    half = QK_ROPE // 2
    dt_p = jnp.pad(dt, ((0, 0), (0, DT_PAD - dt.shape[1])))
    kr3 = kr.reshape(d, 1, QK_ROPE)
    kr_p = _pad_heads(None, kr3)
    krs_p = _pad_heads(None, _rope_partner(kr3[..., :half], kr3[..., half:]))
    return jnp.concatenate([z, xbc, dt_p, qa, kva, kr_p, krs_p, gates], axis=1).astype(jnp.bfloat16)


def _in_proj(x2, norm_w, w_in):
    t, d = x2.shape
    w = _arrange_w_in(w_in)
    tb = min(IN_TB, t)
    return pl.pallas_call(
        _in_proj_kernel,
        grid=(t // tb,),
        in_specs=[pl.BlockSpec((tb, d), lambda i: (i, 0)),
                  pl.BlockSpec((1, d), lambda i: (0, 0)),
                  pl.BlockSpec(w.shape, lambda i: (0, 0))],
        out_specs=[pl.BlockSpec((tb, n), lambda i: (i, 0)) for n in IN_OUT_WIDTHS],
        out_shape=[jax.ShapeDtypeStruct((t, n), jnp.float32) for n in IN_OUT_WIDTHS],
        compiler_params=pltpu.CompilerParams(dimension_semantics=("arbitrary",),
                                             vmem_limit_bytes=56 << 20),
        name="in_proj",
    )(x2, norm_w.reshape(1, d), w)


HEAD_W = SSD_HEADS * SSD_HEADDIM
GROUP_W = HEAD_W // SSD_GROUPS
BC_W = SSD_GROUPS * SSD_STATE


def _split3(x):
    f32, bf16 = jnp.float32, jnp.bfloat16
    hi = x.astype(bf16)
    r = x - hi.astype(f32)
    mid = r.astype(bf16)
    lo = (r - mid.astype(f32)).astype(bf16)
    return hi, mid, lo


def _ssd_kernel(direction, xc_ref, xp_ref, xn_ref, dt_ref, cw_ref, cb_ref, dtb_ref, a_ref, e_ref,
                tri_ref, *rest):
    f32, bf16 = jnp.float32, jnp.bfloat16
    fwd = direction == 0
    if fwd:
        dsk_ref, y_ref, h_scr = rest
    else:
        y1_ref, z_ref, nw_ref, y_ref, h_scr = rest
    c = pl.program_id(1)
    nc = pl.num_programs(1)
    cm = c if fwd else nc - 1 - c

    @pl.when(c == 0)
    def _():
        h_scr[...] = jnp.zeros_like(h_scr)

    xc = xc_ref[0]
    row = lax.broadcasted_iota(jnp.int32, xc.shape, 0)
    prev_row = jnp.where(cm > 0, xp_ref[0, 7:8, :], 0.0)
    next_row = jnp.where(cm < nc - 1, xn_ref[0, 0:1, :], 0.0)
    x_m1 = jnp.where(row == 0, prev_row, pltpu.roll(xc, 1, axis=0))
    x_p1 = jnp.where(row == CHUNK - 1, next_row, pltpu.roll(xc, CHUNK - 1, axis=0))
    conv = cw_ref[0:1, :] * x_m1 + cw_ref[1:2, :] * xc + cw_ref[2:3, :] * x_p1 + cb_ref[...]
    act = conv * jax.nn.sigmoid(conv)
    xs = act[:, :HEAD_W]

    dtr = dt_ref[0] + dtb_ref[...]
    dt = jnp.maximum(dtr, 0.0) + jnp.log(1.0 + jnp.exp(-jnp.abs(dtr)))
    a = dt * a_ref[...]
    cs = sum(jnp.dot(tri_ref[...], part, preferred_element_type=f32) for part in _split3(a))
    expand = lambda v: sum(jnp.dot(part, e_ref[...], preferred_element_type=f32) for part in _split3(v))
    cs_e = expand(cs)
    tot_e = cs_e[CHUNK - 1:CHUNK, :]
    if fwd:
        pos = cs
        w_state = jnp.exp(tot_e - cs_e)
        out_scale = jnp.exp(cs_e)
    else:
        pos = cs - a
        e_e = cs_e - expand(a)
        w_state = jnp.exp(e_e)
        out_scale = jnp.exp(tot_e - e_e)
    pos_t = pos.T
    xd = xs * expand(dt)
    xdb = xd.astype(bf16)
    xw = (xd * w_state).astype(bf16)
    chunk_decay = jnp.exp(tot_e)

    li = lax.broadcasted_iota(jnp.int32, (CHUNK, CHUNK), 0)
    si = lax.broadcasted_iota(jnp.int32, (CHUNK, CHUNK), 1)
    mask = (si <= li) if fwd else (si >= li)
    for g in range(SSD_GROUPS):
        gsl = slice(g * GROUP_W, (g + 1) * GROUP_W)
        bg = act[:, HEAD_W + g * SSD_STATE:HEAD_W + (g + 1) * SSD_STATE]
        cg = act[:, HEAD_W + BC_W + g * SSD_STATE:HEAD_W + BC_W + (g + 1) * SSD_STATE].astype(bf16)
        cb = lax.dot_general(cg, bg.astype(bf16), (((1,), (1,)), ((), ())), preferred_element_type=f32)
        h_in = h_scr[:, gsl]
        y_off = jnp.dot(cg, h_in.astype(bf16), preferred_element_type=f32) * out_scale[:, gsl]
        s_new = jnp.dot(bg.T.astype(bf16), xw[:, gsl], preferred_element_type=f32)
        h_scr[:, gsl] = h_in * chunk_decay[:, gsl] + s_new
        for r in range(SSD_HPG):
            hd = g * SSD_HPG + r
            lane = direction * SSD_HEADS + hd
            col = pos[:, lane:lane + 1]
            rw = pos_t[lane:lane + 1, :]
            diff = (col - rw) if fwd else (rw - col)
            m = (cb * jnp.exp(jnp.where(mask, diff, NEG_INF))).astype(bf16)
            hsl = slice(hd * SSD_HEADDIM, (hd + 1) * SSD_HEADDIM)
            y_d = jnp.dot(m, xdb[:, hsl], preferred_element_type=f32)
            y_ref[0, :, hsl] = y_d + y_off[:, r * SSD_HEADDIM:(r + 1) * SSD_HEADDIM]

    if fwd:
        y_ref[0] = y_ref[0] + xs * dsk_ref[...]
    else:
        yg = (y_ref[0] + y1_ref[0]) * (z_ref[0] * jax.nn.sigmoid(z_ref[0]))
        for g in range(SSD_GROUPS):
            gsl = slice(g * GROUP_W, (g + 1) * GROUP_W)
            y_ref[0, :, gsl] = _rms_rows(yg[:, gsl], nw_ref[:, gsl])


def _ssd_pass(direction, xbc, dt128, consts, extra):
    b, s, cw_ = xbc.shape
    nc = s // CHUNK
    per8 = CHUNK // 8
    cm = (lambda c: c) if direction == 0 else (lambda c: nc - 1 - c)
    cur = lambda n: pl.BlockSpec((1, CHUNK, n), lambda bi, c: (bi, cm(c), 0))
    prev = pl.BlockSpec((1, 8, cw_), lambda bi, c: (bi, jnp.maximum(cm(c) * per8 - 1, 0), 0))
    nxt = pl.BlockSpec((1, 8, cw_), lambda bi, c: (bi, jnp.minimum((cm(c) + 1) * per8, s // 8 - 1), 0))
    full = lambda a: pl.BlockSpec(a.shape, lambda bi, c: (0,) * a.ndim)
    if direction == 0:
        extra_specs = [full(extra[0])]
    else:
        extra_specs = [cur(HEAD_W), cur(HEAD_W), full(extra[2])]
    return pl.pallas_call(
        functools.partial(_ssd_kernel, direction),
        grid=(b, nc),
        in_specs=[cur(cw_), prev, nxt, cur(DT_PAD)] + [full(a) for a in consts] + extra_specs,
        out_specs=cur(HEAD_W),
        out_shape=jax.ShapeDtypeStruct((b, s, HEAD_W), jnp.float32),
        scratch_shapes=[pltpu.VMEM((SSD_STATE, HEAD_W), jnp.float32)],
        compiler_params=pltpu.CompilerParams(dimension_semantics=("arbitrary", "arbitrary"),
                                             vmem_limit_bytes=48 << 20),
        name="ssd_fwd" if direction == 0 else "ssd_bwd",
    )(xbc, xbc, xbc, dt128, *consts, *extra)


def _ssd_pallas(xbc, dt128, z, conv_w, conv_b, dt_bias_fwd, dt_bias_bwd, a_log_fwd, a_log_bwd,
                d_skip, ssd_norm_w):
    bf16 = jnp.bfloat16
    pad = lambda v: jnp.pad(v, (0, DT_PAD - v.shape[0])).reshape(1, DT_PAD)
    dtb = pad(jnp.concatenate([dt_bias_fwd, dt_bias_bwd]))
    a_neg = pad(-jnp.exp(jnp.concatenate([a_log_fwd, a_log_bwd])))
    tri = (jnp.arange(CHUNK)[None, :] <= jnp.arange(CHUNK)[:, None]).astype(bf16)
    head_of_lane = jnp.arange(HEAD_W) // SSD_HEADDIM
    y = None
    for direction in range(2):
        e = (jnp.arange(DT_PAD)[:, None] == (direction * SSD_HEADS + head_of_lane)[None, :]).astype(bf16)
        consts = (conv_w, conv_b.reshape(1, -1), dtb, a_neg, e, tri)
        if direction == 0:
            extra = (jnp.repeat(d_skip, SSD_HEADDIM).reshape(1, HEAD_W),)
        else:
            extra = (y, z, ssd_norm_w.reshape(1, HEAD_W))
        y = _ssd_pass(direction, xbc, dt128, consts, extra)
    return y


MERGE_TB = 256


def _merge_kernel(x_ref, ys_ref, ym_ref, g_ref, nw_ref, pa_ref, pb_ref, wo_ref, wq_ref,
                  h_ref, un_ref, q_ref):
    bf16 = jnp.bfloat16
    f32 = jnp.float32
    g = jax.nn.sigmoid(g_ref[...])
    ma = jnp.dot(ys_ref[...].astype(bf16), pa_ref[...], preferred_element_type=f32)
    mb = jnp.dot(ym_ref[...].astype(bf16), pb_ref[...], preferred_element_type=f32)
    m = g[:, :D_MODEL] * ma + g[:, D_MODEL:] * mb
    h = x_ref[...] + jnp.dot(m.astype(bf16), wo_ref[...], preferred_element_type=f32)
    h_ref[...] = h
    un = _rms_rows(h, nw_ref[...])
    un_ref[...] = un
    q_ref[...] = jnp.dot(un.astype(bf16), wq_ref[...], preferred_element_type=f32)


def _merge(x2, y_ssd, y_mla, gates, norm_ffn_w, w_proj_ssd, w_proj_mla, w_out, peer_w_q):
    t, d = x2.shape
    bf16 = jnp.bfloat16
    tb = min(MERGE_TB, t)
    nq = peer_w_q.shape[1]
    row = lambda n: pl.BlockSpec((tb, n), lambda i: (i, 0))
    full = lambda a: pl.BlockSpec(a.shape, lambda i: (0, 0))
    ws = [w.astype(bf16) for w in (w_proj_ssd, w_proj_mla, w_out, peer_w_q)]
    nw = norm_ffn_w.reshape(1, d)
    return pl.pallas_call(
        _merge_kernel,
        grid=(t // tb,),
        in_specs=[row(d), row(d), row(d), row(GATE_W), full(nw)] + [full(w) for w in ws],
        out_specs=[row(d), row(d), row(nq)],
        out_shape=[jax.ShapeDtypeStruct((t, d), jnp.float32), jax.ShapeDtypeStruct((t, d), jnp.float32),
                   jax.ShapeDtypeStruct((t, nq), jnp.float32)],
        compiler_params=pltpu.CompilerParams(dimension_semantics=("arbitrary",),
                                             vmem_limit_bytes=56 << 20),
        name="merge",
    )(x2, y_ssd, y_mla, gates, nw, *ws)


MLA_TB = 512
ATT_TQ = 256
ATT_HEADS = 4


def _mla_proj_kernel(pos_ref, qa_ref, kva_ref, kr_ref, krs_ref, qnw_ref, kvnw_ref, inv_ref,
                     wq_ref, wqs_ref, wk_ref, wv_ref, q_out, k_out, v_out):
    bf16 = jnp.bfloat16
    ang = pos_ref[...].astype(jnp.float32) * inv_ref[...]
    cos, sin = jnp.cos(ang), jnp.sin(ang)
    qa = _rms_rows(qa_ref[...], qnw_ref[...]).astype(bf16)
    q = jnp.dot(qa, wq_ref[...], preferred_element_type=jnp.float32)
    qs = jnp.dot(qa, wqs_ref[...], preferred_element_type=jnp.float32)
    kva = _rms_rows(kva_ref[...], kvnw_ref[...]).astype(bf16)
    k = jnp.dot(kva, wk_ref[...], preferred_element_type=jnp.float32)
    v_out[...] = jnp.dot(kva, wv_ref[...], preferred_element_type=jnp.float32).astype(bf16)
    kro = kr_ref[...] * cos + krs_ref[...] * sin
    c = (QK_DIM ** -0.5) * LOG2E
    for h in range(MLA_HEADS):
        sl = slice(h * HEAD_PAD, (h + 1) * HEAD_PAD)
        q_out[:, sl] = ((q[:, sl] * cos + qs[:, sl] * sin) * c).astype(bf16)
        k_out[:, sl] = (k[:, sl] + kro).astype(bf16)


def _rope_partner(w1, w2):
    return jnp.concatenate([-w2, w1], axis=-1)


def _pad_heads(nope, rope):
    ref = nope if nope is not None else rope
    lead = ref.shape[:-1]
    z = lambda n: jnp.zeros(lead + (n,), ref.dtype)
    parts = [nope if nope is not None else z(QK_NOPE), rope if rope is not None else z(QK_ROPE),
             z(HEAD_PAD - QK_DIM)]
    out = jnp.concatenate(parts, axis=-1)
    return out.reshape(out.shape[:-2] + (out.shape[-2] * HEAD_PAD,))


def _mla_proj(positions, q_a, kv_a, kr, krs, q_a_norm_w, kv_a_norm_w, w_q_b, w_kv_b):
    t = q_a.shape[0]
    bf16 = jnp.bfloat16
    half = QK_ROPE // 2
    wq = w_q_b.reshape(Q_LORA, MLA_HEADS, QK_DIM)
    wq_n, wq_r = wq[..., :QK_NOPE], wq[..., QK_NOPE:]
    wq_main = _pad_heads(wq_n, wq_r).astype(bf16)
    wq_part = _pad_heads(None, _rope_partner(wq_r[..., :half], wq_r[..., half:])).astype(bf16)
    wkv = w_kv_b.reshape(KV_LORA, MLA_HEADS, QK_NOPE + V_HEAD)
    wk = _pad_heads(wkv[..., :QK_NOPE], None).astype(bf16)
    wv = wkv[..., QK_NOPE:].reshape(KV_LORA, MLA_HEADS * V_HEAD).astype(bf16)
    inv = 1.0 / (ROPE_THETA ** (jnp.arange(0, QK_ROPE, 2, dtype=jnp.float32) / QK_ROPE))
    inv_lane = jnp.concatenate([jnp.zeros((QK_NOPE,), jnp.float32), inv, inv,
                                jnp.zeros((HEAD_PAD - QK_DIM,), jnp.float32)]).reshape(1, HEAD_PAD)
    tb = min(MLA_TB, t)
    row = lambda n: pl.BlockSpec((tb, n), lambda i: (i, 0))
    full = lambda a: pl.BlockSpec(a.shape, lambda i: (0,) * a.ndim)
    qn = q_a_norm_w.reshape(1, Q_LORA)
    kvn = kv_a_norm_w.reshape(1, KV_LORA)
    args = (positions.reshape(t, 1), q_a, kv_a, kr, krs, qn, kvn, inv_lane, wq_main, wq_part, wk, wv)
    in_specs = [row(1), row(Q_LORA), row(KV_LORA), row(HEAD_PAD), row(HEAD_PAD)] + [full(a) for a in args[5:]]
    nq = MLA_HEADS * HEAD_PAD
    return pl.pallas_call(
        _mla_proj_kernel,
        grid=(t // tb,),
        in_specs=in_specs,
        out_specs=[row(nq), row(nq), row(MLA_HEADS * V_HEAD)],
        out_shape=[jax.ShapeDtypeStruct((t, nq), bf16), jax.ShapeDtypeStruct((t, nq), bf16),
                   jax.ShapeDtypeStruct((t, MLA_HEADS * V_HEAD), bf16)],
        compiler_params=pltpu.CompilerParams(dimension_semantics=("arbitrary",),
                                             vmem_limit_bytes=48 << 20),
        name="mla_proj",
    )(*args)


def _attn_kernel(q_ref, k_ref, v_ref, o_ref):
    tq = q_ref.shape[1]
    lane = lax.broadcasted_iota(jnp.int32, (tq, 2 * V_HEAD), 1)
    for pair in range(ATT_HEADS // 2):
        vsl = slice(pair * 2 * V_HEAD, (pair + 1) * 2 * V_HEAD)
        v = v_ref[0, :, vsl]
        outs = []
        for hh in range(2 * pair, 2 * pair + 2):
            sl = slice(hh * HEAD_PAD, (hh + 1) * HEAD_PAD)
            s = lax.dot_general(q_ref[0, :, sl], k_ref[0, :, sl], (((1,), (1,)), ((), ())),
                                preferred_element_type=jnp.float32)
            m = jnp.max(s, axis=-1, keepdims=True)
            p = jnp.exp2(s - m)
            l = jnp.sum(p, axis=-1, keepdims=True)
            pv = jnp.dot(p.astype(jnp.bfloat16), v, preferred_element_type=jnp.float32)
            outs.append(pv / l)
        o_ref[0, :, vsl] = jnp.where(lane < V_HEAD, outs[0], outs[1]).astype(o_ref.dtype)


def _attention(q, k, v):
    b, s, _ = q.shape
    tq = min(ATT_TQ, s)
    qw, vw = ATT_HEADS * HEAD_PAD, ATT_HEADS * V_HEAD
    return pl.pallas_call(
        _attn_kernel,
        grid=(b, MLA_HEADS // ATT_HEADS, s // tq),
        in_specs=[pl.BlockSpec((1, tq, qw), lambda bi, j, i: (bi, i, j)),
                  pl.BlockSpec((1, s, qw), lambda bi, j, i: (bi, 0, j)),
                  pl.BlockSpec((1, s, vw), lambda bi, j, i: (bi, 0, j))],
        out_specs=pl.BlockSpec((1, tq, vw), lambda bi, j, i: (bi, i, j)),
        out_shape=jax.ShapeDtypeStruct((b, s, MLA_HEADS * V_HEAD), jnp.bfloat16),
        compiler_params=pltpu.CompilerParams(dimension_semantics=("arbitrary",) * 3,
                                             vmem_limit_bytes=56 << 20),
        name="mla_attention",
    )(q, k, v)


def _mla_pallas(positions, q_a, kv_a, kr, krs, q_a_norm_w, kv_a_norm_w, w_q_b, w_kv_b):
    b, s = positions.shape
    q, k, v = _mla_proj(positions, q_a, kv_a, kr, krs, q_a_norm_w, kv_a_norm_w, w_q_b, w_kv_b)
    y = _attention(q.reshape(b, s, -1), k.reshape(b, s, -1), v.reshape(b, s, -1))
    return y.reshape(b * s, -1)


PEER_TB = 128
PEER_K = PEER_HEADS * PEER_TOPK
PEER_GROUP = 32
GATHER_TB = 256
HALF_D = D_MODEL // 2
NEG_INF = float("-inf")


def _top16_rows(xs, extras=None):
    n = len(xs)
    rows = xs[0].shape[0]
    iota = lax.broadcasted_iota(jnp.int32, xs[0].shape, 0).astype(jnp.float32)
    xs = list(xs)
    vals, poss, exts = [[] for _ in xs], [[] for _ in xs], [[] for _ in xs]
    for _ in range(PEER_TOPK):
        for j in range(n):
            m = jnp.max(xs[j], axis=0, keepdims=True)
            p = jnp.min(jnp.where(xs[j] == m, iota, float(rows)), axis=0, keepdims=True)
            hit = iota == p
            if extras is not None:
                exts[j].append(jnp.sum(jnp.where(hit, extras[j], 0.0), axis=0, keepdims=True))
            xs[j] = jnp.where(hit, NEG_INF, xs[j])
            vals[j].append(m)
            poss[j].append(p)
    cat = lambda parts: jnp.concatenate(parts, axis=0)
    return [(cat(vals[j]), cat(poss[j]), cat(exts[j]) if extras is not None else None) for j in range(n)]


_CAND_PAIRS = [(a, b) for a in range(PEER_TOPK) for b in range(PEER_TOPK) if (a + 1) * (b + 1) <= PEER_TOPK]
CAND_ROWS = -(-len(_CAND_PAIRS) // 8) * 8
ROW_WORDS = 4
TOPK_LOCKSTEP_HEADS = 4


def _cand_select():
    g0 = [[0.0] * PEER_TOPK for _ in range(CAND_ROWS)]
    g1 = [[0.0] * PEER_TOPK for _ in range(CAND_ROWS)]
    for r, (a, b) in enumerate(_CAND_PAIRS):
        g0[r][a] = 1.0
        g1[r][b] = 1.0
    return jnp.array(g0, jnp.float32), jnp.array(g1, jnp.float32)


def _peer_topk_kernel(q_ref, keys_ref, g0_ref, g1_ref, idx_ref, gate_ref):
    f32 = jnp.float32
    pick = lambda g_ref, v: jnp.dot(g_ref[...], v, precision=lax.Precision.HIGHEST, preferred_element_type=f32)
    pad_row = lax.broadcasted_iota(jnp.int32, (CAND_ROWS, q_ref.shape[0]), 0) >= len(_CAND_PAIRS)
    for h0 in range(0, PEER_HEADS, TOPK_LOCKSTEP_HEADS):
        heads = range(h0, h0 + TOPK_LOCKSTEP_HEADS)
        scores = []
        for h in heads:
            for i in range(2):
                c0 = (h * 2 + i) * KEY_HALF
                scores.append(lax.dot_general(keys_ref[i], q_ref[:, c0:c0 + KEY_HALF], (((1,), (1,)), ((), ())),
                                              preferred_element_type=f32))
        tops = _top16_rows(scores)
        cands, cidxs = [], []
        for j in range(len(heads)):
            (v0, p0, _), (v1, p1, _) = tops[2 * j], tops[2 * j + 1]
            cands.append(jnp.where(pad_row, NEG_INF, pick(g0_ref, v0) + pick(g1_ref, v1)))
            cidxs.append((pick(g0_ref, p0) * N_KEYS + pick(g1_ref, p1)) * ROW_WORDS)
        for h, (best, _, eidx) in zip(heads, _top16_rows(cands, cidxs)):
            e = jnp.exp(best - best[0:1, :])
            gate = e / jnp.sum(e, axis=0, keepdims=True)
            idx_ref[h * PEER_TOPK:(h + 1) * PEER_TOPK, :] = eidx.astype(jnp.int32)
            gate_ref[h * PEER_TOPK:(h + 1) * PEER_TOPK, :] = gate


def _peer_topk(q, sub_keys):
    t = q.shape[0]
    g0, g1 = _cand_select()
    return pl.pallas_call(
        _peer_topk_kernel,
        grid=(t // PEER_TB,),
        in_specs=[pl.BlockSpec((PEER_TB, q.shape[1]), lambda i: (i, 0)),
                  pl.BlockSpec(sub_keys.shape, lambda i: (0, 0, 0)),
                  pl.BlockSpec(g0.shape, lambda i: (0, 0)),
                  pl.BlockSpec(g1.shape, lambda i: (0, 0))],
        out_specs=[pl.BlockSpec((PEER_K, PEER_TB), lambda i: (0, i)),
                   pl.BlockSpec((PEER_K, PEER_TB), lambda i: (0, i))],
        out_shape=[jax.ShapeDtypeStruct((PEER_K, t), jnp.int32),
                   jax.ShapeDtypeStruct((PEER_K, t), jnp.float32)],
        name="peer_topk",
    )(q, sub_keys, g0, g1)


def _pack_table(tab):
    e = tab.shape[0]
    tb = tab.astype(jnp.bfloat16)
    lo = lax.bitcast_convert_type(tb[:, :HALF_D], jnp.uint16).astype(jnp.uint32)
    hi = lax.bitcast_convert_type(tb[:, HALF_D:], jnp.uint16).astype(jnp.uint32)
    return (lo | (hi << 16)).reshape(e * ROW_WORDS, 128)


def _token_groups(eight_tokens):
    def trip(g, carry):
        for sub in range(PEER_GROUP // 8):
            eight_tokens(pl.multiple_of(g * PEER_GROUP + sub * 8, 8))
        return carry

    lax.fori_loop(0, GATHER_TB // PEER_GROUP, trip, 0)


def _gather_rows(idx_ref, tok, tab_ref, scr):
    nchunk = HALF_D // 128
    for k in range(PEER_K):
        e = idx_ref[tok * PEER_K + k]
        scr[k // 8, pl.ds(k % 8, nchunk, stride=8), :] = tab_ref[pl.ds(pl.multiple_of(e, ROW_WORDS), ROW_WORDS), :]
    gm = jnp.concatenate(
        [scr[:, j * 8:(j + 1) * 8, :].reshape(PEER_K, 128) for j in range(nchunk)], axis=1)
    return pltpu.bitcast(gm, jnp.bfloat16)


def _peer_u_kernel(idx_ref, ut_ref, tab_ref, out_ref, scr0, scr1):
    shp = (8, 2 * PEER_K)
    row = lax.broadcasted_iota(jnp.int32, shp, 0)
    even = (lax.broadcasted_iota(jnp.int32, shp, 1) & 1) == 0

    def eight_tokens(base):
        ut8 = ut_ref[pl.ds(base, 8), :]
        lhs = jnp.concatenate([ut8[:, :HALF_D], ut8[:, HALF_D:]], axis=0).astype(jnp.bfloat16)
        acc = jnp.zeros(shp, jnp.float32)
        for i in range(8):
            b = _gather_rows(idx_ref, base + i, tab_ref, (scr0, scr1)[i % 2])
            r = lax.dot_general(lhs, b, (((1,), (1,)), ((), ())),
                                preferred_element_type=jnp.float32)
            acc = jnp.where(row == i, jnp.where(even, r[0:8], r[8:16]), acc)
        out_ref[pl.ds(base, 8), :] = acc

    _token_groups(eight_tokens)


def _peer_v_kernel(idx_ref, a_ref, gate_ref, h_ref, nw_ref, tab_ref, out_ref, scr0, scr1):
    shp = (8, 2 * PEER_K)
    row = lax.broadcasted_iota(jnp.int32, (8, HALF_D), 0)
    even = (lax.broadcasted_iota(jnp.int32, shp, 1) & 1) == 0

    def eight_tokens(base):
        a8 = a_ref[pl.ds(base, 8), :]
        s = a8 + pltpu.roll(a8, 2 * PEER_K - 1, axis=1)
        act = jnp.where(even, s, pltpu.roll(s, 1, axis=1))
        gelu = 0.5 * act * (1.0 + lax.erf(act * (2.0 ** -0.5)))
        w = gelu * gate_ref[pl.ds(base, 8), :]
        lhs = jnp.concatenate([jnp.where(even, w, 0.0), jnp.where(even, 0.0, w)],
                              axis=0).astype(jnp.bfloat16)
        acc_lo = jnp.zeros((8, HALF_D), jnp.float32)
        acc_hi = jnp.zeros((8, HALF_D), jnp.float32)
        for i in range(8):
            b = _gather_rows(idx_ref, base + i, tab_ref, (scr0, scr1)[i % 2])
            r = jnp.dot(lhs, b, preferred_element_type=jnp.float32)
            acc_lo = jnp.where(row == i, r[0:8], acc_lo)
            acc_hi = jnp.where(row == i, r[8:16], acc_hi)
        y_lo = h_ref[pl.ds(base, 8), 0:HALF_D] + acc_lo
        y_hi = h_ref[pl.ds(base, 8), HALF_D:] + acc_hi
        ms = (jnp.sum(y_lo * y_lo, axis=-1, keepdims=True)
              + jnp.sum(y_hi * y_hi, axis=-1, keepdims=True)) * (1.0 / D_MODEL)
        inv = lax.rsqrt(ms + EPS)
        out_ref[pl.ds(base, 8), 0:HALF_D] = y_lo * inv * nw_ref[:, 0:HALF_D]
        out_ref[pl.ds(base, 8), HALF_D:] = y_hi * inv * nw_ref[:, HALF_D:]

    _token_groups(eight_tokens)


def _peer_scratch():
    return [pltpu.VMEM((PEER_K // 8, 8 * (HALF_D // 128), 128), jnp.uint32) for _ in range(2)]


def _peer_params(tab):
    table_bytes = tab.size * 4
    return pltpu.CompilerParams(dimension_semantics=("arbitrary",),
                                vmem_limit_bytes=table_bytes + (16 << 20))


def _peer_u_pass(idx_flat, un, tab):
    t = un.shape[0]
    return pl.pallas_call(
        _peer_u_kernel,
        grid=(t // GATHER_TB,),
        in_specs=[pl.BlockSpec((GATHER_TB * PEER_K,), lambda i: (i,), memory_space=pltpu.SMEM),
                  pl.BlockSpec((GATHER_TB, D_MODEL), lambda i: (i, 0)),
                  pl.BlockSpec(memory_space=pltpu.VMEM)],
        out_specs=pl.BlockSpec((GATHER_TB, 2 * PEER_K), lambda i: (i, 0)),
        out_shape=jax.ShapeDtypeStruct((t, 2 * PEER_K), jnp.float32),
        scratch_shapes=_peer_scratch(),
        compiler_params=_peer_params(tab),
        name="peer_u_pass",
    )(idx_flat, un, tab)


def _peer_v_pass(idx_flat, a, gate2, h, norm_w, tab):
    t = a.shape[0]
    row = lambda n: pl.BlockSpec((GATHER_TB, n), lambda i: (i, 0))
    return pl.pallas_call(
        _peer_v_kernel,
        grid=(t // GATHER_TB,),
        in_specs=[pl.BlockSpec((GATHER_TB * PEER_K,), lambda i: (i,), memory_space=pltpu.SMEM),
                  row(2 * PEER_K), row(2 * PEER_K), row(D_MODEL),
                  pl.BlockSpec((1, D_MODEL), lambda i: (0, 0)),
                  pl.BlockSpec(memory_space=pltpu.VMEM)],
        out_specs=row(D_MODEL),
        out_shape=jax.ShapeDtypeStruct((t, D_MODEL), jnp.float32),
        scratch_shapes=_peer_scratch(),
        compiler_params=_peer_params(tab),
        name="peer_v_pass",
    )(idx_flat, a, gate2, h, norm_w.reshape(1, D_MODEL), tab)


def _peer_and_final_norm(h, un, q, sub_keys, peer_u, peer_v, norm_final_w):
    idx_t, gate_t = _peer_topk(q, sub_keys)
    idx_flat = idx_t.T.reshape(-1)
    gate2 = jnp.repeat(gate_t.T, 2, axis=1)
    a = _peer_u_pass(idx_flat, un, _pack_table(peer_u))
    return _peer_v_pass(idx_flat, a, gate2, h, norm_final_w, _pack_table(peer_v))


def kernel(x, positions, norm_mix_w, w_in, conv_w, conv_b, dt_bias_fwd, dt_bias_bwd, a_log_fwd, a_log_bwd, d_skip, ssd_norm_w, q_a_norm_w, w_q_b, kv_a_norm_w, w_kv_b, w_proj_ssd, w_proj_mla, w_out, norm_ffn_w, peer_w_q, peer_sub_keys, peer_u, peer_v, norm_final_w):
    b, s, _ = x.shape
    x2 = x.reshape(b * s, D_MODEL)
    z, xbc, dt128, q_a, kv_a, kr, krs, gate_logits = _in_proj(x2, norm_mix_w, w_in)
    y_ssd = _ssd_pallas(xbc.reshape(b, s, -1), dt128.reshape(b, s, -1), z.reshape(b, s, -1), conv_w, conv_b,
                        dt_bias_fwd, dt_bias_bwd, a_log_fwd, a_log_bwd, d_skip, ssd_norm_w)

    y_mla = _mla_pallas(positions, q_a, kv_a, kr, krs, q_a_norm_w, kv_a_norm_w, w_q_b, w_kv_b)

    h2, un, q_peer = _merge(x2, y_ssd.reshape(b * s, -1), y_mla, gate_logits, norm_ffn_w,
                            w_proj_ssd, w_proj_mla, w_out, peer_w_q)
    out = _peer_and_final_norm(h2, un, q_peer, peer_sub_keys, peer_u, peer_v, norm_final_w)
    return out.reshape(b, s, D_MODEL)
```

```python
import functools

import jax
import jax.numpy as jnp
from jax import lax
from jax.experimental import pallas as pl
from jax.experimental.pallas import tpu as pltpu

D_MODEL = 1024
SSD_D_INNER = 1024
SSD_HEADDIM = 64
SSD_HEADS = 16
SSD_GROUPS = 2
SSD_HPG = 8
SSD_STATE = 128
CHUNK = 128
SSD_CONV_DIM = SSD_D_INNER + 2 * SSD_GROUPS * SSD_STATE
MLA_HEADS = 16
QK_NOPE = 64
QK_ROPE = 32
V_HEAD = 64
Q_LORA = 384
KV_LORA = 256
ROPE_THETA = 10000.0
PEER_HEADS = 8
N_KEYS = 128
KEY_HALF = 128
PEER_TOPK = 16
EPS = 1e-6
HEAD_PAD = 128
QK_DIM = QK_NOPE + QK_ROPE
LOG2E = 1.4426950408889634
IN_SIZES = (SSD_D_INNER, SSD_CONV_DIM, 2 * SSD_HEADS, Q_LORA, KV_LORA, QK_ROPE, 2 * D_MODEL)


def _split_points(sizes):
    pts, acc = [], 0
    for s in sizes[:-1]:
        acc += s
        pts.append(acc)
    return pts


VMEM_BYTES_V7X = 64 << 20
VMEM_CAP = VMEM_BYTES_V7X - (8 << 20)
BLOCK_ALLOWANCE = 24 << 20


def _params(grid_rank, resident_bytes=0):
    return pltpu.CompilerParams(dimension_semantics=("arbitrary",) * grid_rank,
                                vmem_limit_bytes=min(resident_bytes + BLOCK_ALLOWANCE, VMEM_CAP))


def _nbytes(*arrays):
    return sum(a.size * a.dtype.itemsize for a in arrays)


def _rms_rows(x, w):
    return x * lax.rsqrt(jnp.mean(x * x, axis=-1, keepdims=True) + EPS) * w


IN_TB = 256
DT_PAD = 128
GATE_W = 2 * D_MODEL
IN_OUT_WIDTHS = (SSD_D_INNER, SSD_CONV_DIM, DT_PAD, Q_LORA, KV_LORA, HEAD_PAD, HEAD_PAD, GATE_W)


def _in_proj_kernel(x_ref, nw_ref, w_ref, *out_refs):
    u = _rms_rows(x_ref[...], nw_ref[...]).astype(jnp.bfloat16)
    col = 0
    for o_ref, width in zip(out_refs, IN_OUT_WIDTHS):
        o_ref[...] = jnp.dot(u, w_ref[:, col:col + width], preferred_element_type=jnp.float32)
        col += width


def _arrange_w_in(w_in):
    z, xbc, dt, qa, kva, kr, gates = jnp.split(w_in, _split_points(IN_SIZES), axis=1)
    d = w_in.shape[0]
    half = QK_ROPE // 2
    dt_p = jnp.pad(dt, ((0, 0), (0, DT_PAD - dt.shape[1])))
    kr3 = kr.reshape(d, 1, QK_ROPE)
    kr_p = _pad_heads(None, kr3)
    krs_p = _pad_heads(None, _rope_partner(kr3[..., :half], kr3[..., half:]))
    return jnp.concatenate([z, xbc, dt_p, qa, kva, kr_p, krs_p, gates], axis=1).astype(jnp.bfloat16)


def _in_proj(x2, norm_w, w_in):
    t, d = x2.shape
    w = _arrange_w_in(w_in)
    tb = min(IN_TB, t)
    return pl.pallas_call(
        _in_proj_kernel,
        grid=(t // tb,),
        in_specs=[pl.BlockSpec((tb, d), lambda i: (i, 0)),
                  pl.BlockSpec((1, d), lambda i: (0, 0)),
                  pl.BlockSpec(w.shape, lambda i: (0, 0))],
        out_specs=[pl.BlockSpec((tb, n), lambda i: (i, 0)) for n in IN_OUT_WIDTHS],
        out_shape=[jax.ShapeDtypeStruct((t, n), jnp.float32) for n in IN_OUT_WIDTHS],
        compiler_params=_params(1, 2 * _nbytes(w)),
        name="in_proj",
    )(x2, norm_w.reshape(1, d), w)


HEAD_W = SSD_HEADS * SSD_HEADDIM
GROUP_W = HEAD_W // SSD_GROUPS
BC_W = SSD_GROUPS * SSD_STATE


def _split3(x):
    f32, bf16 = jnp.float32, jnp.bfloat16
    hi = x.astype(bf16)
    r = x - hi.astype(f32)
    mid = r.astype(bf16)
    lo = (r - mid.astype(f32)).astype(bf16)
    return hi, mid, lo


def _ssd_kernel(direction, xc_ref, xp_ref, xn_ref, dt_ref, cw_ref, cb_ref, dtb_ref, a_ref, e_ref,
                tri_ref, *rest):
    f32, bf16 = jnp.float32, jnp.bfloat16
    fwd = direction == 0
    if fwd:
        dsk_ref, y_ref, h_scr = rest
    else:
        y1_ref, z_ref, nw_ref, y_ref, h_scr = rest
    c = pl.program_id(1)
    nc = pl.num_programs(1)
    cm = c if fwd else nc - 1 - c

    @pl.when(c == 0)
    def _():
        h_scr[...] = jnp.zeros_like(h_scr)

    xc = xc_ref[0]
    row = lax.broadcasted_iota(jnp.int32, xc.shape, 0)
    prev_row = jnp.where(cm > 0, xp_ref[0, 7:8, :], 0.0)
    next_row = jnp.where(cm < nc - 1, xn_ref[0, 0:1, :], 0.0)
    x_m1 = jnp.where(row == 0, prev_row, pltpu.roll(xc, 1, axis=0))
    x_p1 = jnp.where(row == CHUNK - 1, next_row, pltpu.roll(xc, CHUNK - 1, axis=0))
    conv = cw_ref[0:1, :] * x_m1 + cw_ref[1:2, :] * xc + cw_ref[2:3, :] * x_p1 + cb_ref[...]
    act = conv * jax.nn.sigmoid(conv)
    xs = act[:, :HEAD_W]

    dtr = dt_ref[0] + dtb_ref[...]
    dt = jnp.maximum(dtr, 0.0) + jnp.log(1.0 + jnp.exp(-jnp.abs(dtr)))
    a = dt * a_ref[...]
    cs = sum(jnp.dot(tri_ref[...], part, preferred_element_type=f32) for part in _split3(a))
    expand = lambda v: sum(jnp.dot(part, e_ref[...], preferred_element_type=f32) for part in _split3(v))
    cs_e = expand(cs)
    tot_e = cs_e[CHUNK - 1:CHUNK, :]
    if fwd:
        pos = cs
        w_state = jnp.exp(tot_e - cs_e)
        out_scale = jnp.exp(cs_e)
    else:
        pos = cs - a
        e_e = cs_e - expand(a)
        w_state = jnp.exp(e_e)
        out_scale = jnp.exp(tot_e - e_e)
    pos_t = pos.T
    xd = xs * expand(dt)
    xdb = xd.astype(bf16)
    xw = (xd * w_state).astype(bf16)
    chunk_decay = jnp.exp(tot_e)

    li = lax.broadcasted_iota(jnp.int32, (CHUNK, CHUNK), 0)
    si = lax.broadcasted_iota(jnp.int32, (CHUNK, CHUNK), 1)
    mask = (si <= li) if fwd else (si >= li)
    for g in range(SSD_GROUPS):
        gsl = slice(g * GROUP_W, (g + 1) * GROUP_W)
        bg = act[:, HEAD_W + g * SSD_STATE:HEAD_W + (g + 1) * SSD_STATE]
        cg = act[:, HEAD_W + BC_W + g * SSD_STATE:HEAD_W + BC_W + (g + 1) * SSD_STATE].astype(bf16)
        cb = lax.dot_general(cg, bg.astype(bf16), (((1,), (1,)), ((), ())), preferred_element_type=f32)
        h_in = h_scr[:, gsl]
        y_off = jnp.dot(cg, h_in.astype(bf16), preferred_element_type=f32) * out_scale[:, gsl]
        s_new = jnp.dot(bg.T.astype(bf16), xw[:, gsl], preferred_element_type=f32)
        h_scr[:, gsl] = h_in * chunk_decay[:, gsl] + s_new
        for r in range(SSD_HPG):
            hd = g * SSD_HPG + r
            lane = direction * SSD_HEADS + hd
            col = pos[:, lane:lane + 1]
            rw = pos_t[lane:lane + 1, :]
            diff = (col - rw) if fwd else (rw - col)
            m = (cb * jnp.exp(jnp.where(mask, diff, NEG_INF))).astype(bf16)
            hsl = slice(hd * SSD_HEADDIM, (hd + 1) * SSD_HEADDIM)
            y_d = jnp.dot(m, xdb[:, hsl], preferred_element_type=f32)
            y_ref[0, :, hsl] = y_d + y_off[:, r * SSD_HEADDIM:(r + 1) * SSD_HEADDIM]

    if fwd:
        y_ref[0] = y_ref[0] + xs * dsk_ref[...]
    else:
        yg = (y_ref[0] + y1_ref[0]) * (z_ref[0] * jax.nn.sigmoid(z_ref[0]))
        for g in range(SSD_GROUPS):
            gsl = slice(g * GROUP_W, (g + 1) * GROUP_W)
            y_ref[0, :, gsl] = _rms_rows(yg[:, gsl], nw_ref[:, gsl])


def _ssd_pass(direction, xbc, dt128, consts, extra):
    b, s, cw_ = xbc.shape
    nc = s // CHUNK
    per8 = CHUNK // 8
    cm = (lambda c: c) if direction == 0 else (lambda c: nc - 1 - c)
    cur = lambda n: pl.BlockSpec((1, CHUNK, n), lambda bi, c: (bi, cm(c), 0))
    prev = pl.BlockSpec((1, 8, cw_), lambda bi, c: (bi, jnp.maximum(cm(c) * per8 - 1, 0), 0))
    nxt = pl.BlockSpec((1, 8, cw_), lambda bi, c: (bi, jnp.minimum((cm(c) + 1) * per8, s // 8 - 1), 0))
    full = lambda a: pl.BlockSpec(a.shape, lambda bi, c: (0,) * a.ndim)
    if direction == 0:
        extra_specs = [full(extra[0])]
    else:
        extra_specs = [cur(HEAD_W), cur(HEAD_W), full(extra[2])]
    return pl.pallas_call(
        functools.partial(_ssd_kernel, direction),
        grid=(b, nc),
        in_specs=[cur(cw_), prev, nxt, cur(DT_PAD)] + [full(a) for a in consts] + extra_specs,
        out_specs=cur(HEAD_W),
        out_shape=jax.ShapeDtypeStruct((b, s, HEAD_W), jnp.float32),
        scratch_shapes=[pltpu.VMEM((SSD_STATE, HEAD_W), jnp.float32)],
        compiler_params=_params(2),
        name="ssd_fwd" if direction == 0 else "ssd_bwd",
    )(xbc, xbc, xbc, dt128, *consts, *extra)


def _ssd_pallas(xbc, dt128, z, conv_w, conv_b, dt_bias_fwd, dt_bias_bwd, a_log_fwd, a_log_bwd,
                d_skip, ssd_norm_w):
    bf16 = jnp.bfloat16
    pad = lambda v: jnp.pad(v, (0, DT_PAD - v.shape[0])).reshape(1, DT_PAD)
    dtb = pad(jnp.concatenate([dt_bias_fwd, dt_bias_bwd]))
    a_neg = pad(-jnp.exp(jnp.concatenate([a_log_fwd, a_log_bwd])))
    tri = (jnp.arange(CHUNK)[None, :] <= jnp.arange(CHUNK)[:, None]).astype(bf16)
    head_of_lane = jnp.arange(HEAD_W) // SSD_HEADDIM
    y = None
    for direction in range(2):
        e = (jnp.arange(DT_PAD)[:, None] == (direction * SSD_HEADS + head_of_lane)[None, :]).astype(bf16)
        consts = (conv_w, conv_b.reshape(1, -1), dtb, a_neg, e, tri)
        if direction == 0:
            extra = (jnp.repeat(d_skip, SSD_HEADDIM).reshape(1, HEAD_W),)
        else:
            extra = (y, z, ssd_norm_w.reshape(1, HEAD_W))
        y = _ssd_pass(direction, xbc, dt128, consts, extra)
    return y


MERGE_TB = 256


def _merge_kernel(x_ref, ys_ref, ym_ref, g_ref, nw_ref, pa_ref, pb_ref, wo_ref, wq_ref,
                  h_ref, un_ref, q_ref):
    bf16 = jnp.bfloat16
    f32 = jnp.float32
    g = jax.nn.sigmoid(g_ref[...])
    ma = jnp.dot(ys_ref[...].astype(bf16), pa_ref[...], preferred_element_type=f32)
    mb = jnp.dot(ym_ref[...].astype(bf16), pb_ref[...], preferred_element_type=f32)
    m = g[:, :D_MODEL] * ma + g[:, D_MODEL:] * mb
    h = x_ref[...] + jnp.dot(m.astype(bf16), wo_ref[...], preferred_element_type=f32)
    h_ref[...] = h
    un = _rms_rows(h, nw_ref[...])
    un_ref[...] = un
    q_ref[...] = jnp.dot(un.astype(bf16), wq_ref[...], preferred_element_type=f32)


def _merge(x2, y_ssd, y_mla, gates, norm_ffn_w, w_proj_ssd, w_proj_mla, w_out, peer_w_q):
    t, d = x2.shape
    bf16 = jnp.bfloat16
    tb = min(MERGE_TB, t)
    nq = peer_w_q.shape[1]
    row = lambda n: pl.BlockSpec((tb, n), lambda i: (i, 0))
    full = lambda a: pl.BlockSpec(a.shape, lambda i: (0, 0))
    ws = [w.astype(bf16) for w in (w_proj_ssd, w_proj_mla, w_out, peer_w_q)]
    nw = norm_ffn_w.reshape(1, d)
    return pl.pallas_call(
        _merge_kernel,
        grid=(t // tb,),
        in_specs=[row(d), row(d), row(d), row(GATE_W), full(nw)] + [full(w) for w in ws],
        out_specs=[row(d), row(d), row(nq)],
        out_shape=[jax.ShapeDtypeStruct((t, d), jnp.float32), jax.ShapeDtypeStruct((t, d), jnp.float32),
                   jax.ShapeDtypeStruct((t, nq), jnp.float32)],
        compiler_params=_params(1, 2 * _nbytes(*ws)),
        name="merge",
    )(x2, y_ssd, y_mla, gates, nw, *ws)


MLA_TB = 512
ATT_TQ = 256
ATT_HEADS = 8


def _mla_proj_kernel(pos_ref, qa_ref, kva_ref, kr_ref, krs_ref, qnw_ref, kvnw_ref, inv_ref,
                     wq_ref, wqs_ref, wk_ref, wv_ref, q_out, k_out, v_out):
    bf16 = jnp.bfloat16
    ang = pos_ref[...].astype(jnp.float32) * inv_ref[...]
    cos, sin = jnp.cos(ang), jnp.sin(ang)
    qa = _rms_rows(qa_ref[...], qnw_ref[...]).astype(bf16)
    q = jnp.dot(qa, wq_ref[...], preferred_element_type=jnp.float32)
    qs = jnp.dot(qa, wqs_ref[...], preferred_element_type=jnp.float32)
    kva = _rms_rows(kva_ref[...], kvnw_ref[...]).astype(bf16)
    k = jnp.dot(kva, wk_ref[...], preferred_element_type=jnp.float32)
    v_out[...] = jnp.dot(kva, wv_ref[...], preferred_element_type=jnp.float32).astype(bf16)
    kro = kr_ref[...] * cos + krs_ref[...] * sin
    c = (QK_DIM ** -0.5) * LOG2E
    for h in range(MLA_HEADS):
        sl = slice(h * HEAD_PAD, (h + 1) * HEAD_PAD)
        q_out[:, sl] = ((q[:, sl] * cos + qs[:, sl] * sin) * c).astype(bf16)
        k_out[:, sl] = (k[:, sl] + kro).astype(bf16)


def _rope_partner(w1, w2):
    return jnp.concatenate([-w2, w1], axis=-1)


def _pad_heads(nope, rope):
    ref = nope if nope is not None else rope
    lead = ref.shape[:-1]
    z = lambda n: jnp.zeros(lead + (n,), ref.dtype)
    parts = [nope if nope is not None else z(QK_NOPE), rope if rope is not None else z(QK_ROPE),
             z(HEAD_PAD - QK_DIM)]
    out = jnp.concatenate(parts, axis=-1)
    return out.reshape(out.shape[:-2] + (out.shape[-2] * HEAD_PAD,))


def _mla_proj(positions, q_a, kv_a, kr, krs, q_a_norm_w, kv_a_norm_w, w_q_b, w_kv_b):
    t = q_a.shape[0]
    bf16 = jnp.bfloat16
    half = QK_ROPE // 2
    wq = w_q_b.reshape(Q_LORA, MLA_HEADS, QK_DIM)
    wq_n, wq_r = wq[..., :QK_NOPE], wq[..., QK_NOPE:]
    wq_main = _pad_heads(wq_n, wq_r).astype(bf16)
    wq_part = _pad_heads(None, _rope_partner(wq_r[..., :half], wq_r[..., half:])).astype(bf16)
    wkv = w_kv_b.reshape(KV_LORA, MLA_HEADS, QK_NOPE + V_HEAD)
    wk = _pad_heads(wkv[..., :QK_NOPE], None).astype(bf16)
    wv = wkv[..., QK_NOPE:].reshape(KV_LORA, MLA_HEADS * V_HEAD).astype(bf16)
    inv = 1.0 / (ROPE_THETA ** (jnp.arange(0, QK_ROPE, 2, dtype=jnp.float32) / QK_ROPE))
    inv_lane = jnp.concatenate([jnp.zeros((QK_NOPE,), jnp.float32), inv, inv,
                                jnp.zeros((HEAD_PAD - QK_DIM,), jnp.float32)]).reshape(1, HEAD_PAD)
    tb = min(MLA_TB, t)
    row = lambda n: pl.BlockSpec((tb, n), lambda i: (i, 0))
    full = lambda a: pl.BlockSpec(a.shape, lambda i: (0,) * a.ndim)
    qn = q_a_norm_w.reshape(1, Q_LORA)
    kvn = kv_a_norm_w.reshape(1, KV_LORA)
    args = (positions.reshape(t, 1), q_a, kv_a, kr, krs, qn, kvn, inv_lane, wq_main, wq_part, wk, wv)
    in_specs = [row(1), row(Q_LORA), row(KV_LORA), row(HEAD_PAD), row(HEAD_PAD)] + [full(a) for a in args[5:]]
    nq = MLA_HEADS * HEAD_PAD
    return pl.pallas_call(
        _mla_proj_kernel,
        grid=(t // tb,),
        in_specs=in_specs,
        out_specs=[row(nq), row(nq), row(MLA_HEADS * V_HEAD)],
        out_shape=[jax.ShapeDtypeStruct((t, nq), bf16), jax.ShapeDtypeStruct((t, nq), bf16),
                   jax.ShapeDtypeStruct((t, MLA_HEADS * V_HEAD), bf16)],
        compiler_params=_params(1, 2 * _nbytes(wq_main, wq_part, wk, wv)),
        name="mla_proj",
    )(*args)


def _attn_kernel(q_ref, k_ref, v_ref, o_ref):
    tq = q_ref.shape[1]
    lane = lax.broadcasted_iota(jnp.int32, (tq, 2 * V_HEAD), 1)
    for pair in range(ATT_HEADS // 2):
        vsl = slice(pair * 2 * V_HEAD, (pair + 1) * 2 * V_HEAD)
        v = v_ref[0, :, vsl]
        outs = []
        for hh in range(2 * pair, 2 * pair + 2):
            sl = slice(hh * HEAD_PAD, (hh + 1) * HEAD_PAD)
            s = lax.dot_general(q_ref[0, :, sl], k_ref[0, :, sl], (((1,), (1,)), ((), ())),
                                preferred_element_type=jnp.float32)
            m = jnp.max(s, axis=-1, keepdims=True)
            p = jnp.exp2(s - m)
            l = jnp.sum(p, axis=-1, keepdims=True)
            pv = jnp.dot(p.astype(jnp.bfloat16), v, preferred_element_type=jnp.float32)
            outs.append(pv / l)
        o_ref[0, :, vsl] = jnp.where(lane < V_HEAD, outs[0], outs[1]).astype(o_ref.dtype)


def _attention(q, k, v):
    b, s, _ = q.shape
    tq = min(ATT_TQ, s)
    qw, vw = ATT_HEADS * HEAD_PAD, ATT_HEADS * V_HEAD
    return pl.pallas_call(
        _attn_kernel,
        grid=(b, MLA_HEADS // ATT_HEADS, s // tq),
        in_specs=[pl.BlockSpec((1, tq, qw), lambda bi, j, i: (bi, i, j)),
                  pl.BlockSpec((1, s, qw), lambda bi, j, i: (bi, 0, j)),
                  pl.BlockSpec((1, s, vw), lambda bi, j, i: (bi, 0, j))],
        out_specs=pl.BlockSpec((1, tq, vw), lambda bi, j, i: (bi, i, j)),
        out_shape=jax.ShapeDtypeStruct((b, s, MLA_HEADS * V_HEAD), jnp.bfloat16),
        compiler_params=_params(3, 2 * 2 * s * (qw + vw)),
        name="mla_attention",
    )(q, k, v)


def _mla_pallas(positions, q_a, kv_a, kr, krs, q_a_norm_w, kv_a_norm_w, w_q_b, w_kv_b):
    b, s = positions.shape
    q, k, v = _mla_proj(positions, q_a, kv_a, kr, krs, q_a_norm_w, kv_a_norm_w, w_q_b, w_kv_b)
    y = _attention(q.reshape(b, s, -1), k.reshape(b, s, -1), v.reshape(b, s, -1))
    return y.reshape(b * s, -1)


PEER_TB = 128
PEER_K = PEER_HEADS * PEER_TOPK
PEER_GROUP = 32
GATHER_TB = 256
HALF_D = D_MODEL // 2
NEG_INF = float("-inf")


def _top16_rows(xs, extras=None):
    n = len(xs)
    rows = xs[0].shape[0]
    iota = lax.broadcasted_iota(jnp.int32, xs[0].shape, 0).astype(jnp.float32)
    xs = list(xs)
    vals, poss, exts = [[] for _ in xs], [[] for _ in xs], [[] for _ in xs]
    for _ in range(PEER_TOPK):
        for j in range(n):
            m = jnp.max(xs[j], axis=0, keepdims=True)
            p = jnp.min(jnp.where(xs[j] == m, iota, float(rows)), axis=0, keepdims=True)
            hit = iota == p
            if extras is not None:
                exts[j].append(jnp.sum(jnp.where(hit, extras[j], 0.0), axis=0, keepdims=True))
            xs[j] = jnp.where(hit, NEG_INF, xs[j])
            vals[j].append(m)
            poss[j].append(p)
    cat = lambda parts: jnp.concatenate(parts, axis=0)
    return [(cat(vals[j]), cat(poss[j]), cat(exts[j]) if extras is not None else None) for j in range(n)]


_CAND_PAIRS = [(a, b) for a in range(PEER_TOPK) for b in range(PEER_TOPK) if (a + 1) * (b + 1) <= PEER_TOPK]
CAND_ROWS = -(-len(_CAND_PAIRS) // 8) * 8
ROW_WORDS = 4
TOPK_LOCKSTEP_HEADS = 4


def _cand_select():
    g0 = [[0.0] * PEER_TOPK for _ in range(CAND_ROWS)]
    g1 = [[0.0] * PEER_TOPK for _ in range(CAND_ROWS)]
    for r, (a, b) in enumerate(_CAND_PAIRS):
        g0[r][a] = 1.0
        g1[r][b] = 1.0
    return jnp.array(g0, jnp.float32), jnp.array(g1, jnp.float32)


def _peer_topk_kernel(q_ref, keys_ref, g0_ref, g1_ref, idx_ref, gate_ref):
    f32 = jnp.float32
    pick = lambda g_ref, v: jnp.dot(g_ref[...], v, precision=lax.Precision.HIGHEST, preferred_element_type=f32)
    pad_row = lax.broadcasted_iota(jnp.int32, (CAND_ROWS, q_ref.shape[0]), 0) >= len(_CAND_PAIRS)
    for h0 in range(0, PEER_HEADS, TOPK_LOCKSTEP_HEADS):
        heads = range(h0, h0 + TOPK_LOCKSTEP_HEADS)
        scores = []
        for h in heads:
            for i in range(2):
                c0 = (h * 2 + i) * KEY_HALF
                scores.append(lax.dot_general(keys_ref[i], q_ref[:, c0:c0 + KEY_HALF], (((1,), (1,)), ((), ())),
                                              preferred_element_type=f32))
        tops = _top16_rows(scores)
        cands, cidxs = [], []
        for j in range(len(heads)):
            (v0, p0, _), (v1, p1, _) = tops[2 * j], tops[2 * j + 1]
            cands.append(jnp.where(pad_row, NEG_INF, pick(g0_ref, v0) + pick(g1_ref, v1)))
            cidxs.append((pick(g0_ref, p0) * N_KEYS + pick(g1_ref, p1)) * ROW_WORDS)
        for h, (best, _, eidx) in zip(heads, _top16_rows(cands, cidxs)):
            e = jnp.exp(best - best[0:1, :])
            gate = e / jnp.sum(e, axis=0, keepdims=True)
            idx_ref[h * PEER_TOPK:(h + 1) * PEER_TOPK, :] = eidx.astype(jnp.int32)
            gate_ref[h * PEER_TOPK:(h + 1) * PEER_TOPK, :] = gate


def _peer_topk(q, sub_keys):
    t = q.shape[0]
    g0, g1 = _cand_select()
    return pl.pallas_call(
        _peer_topk_kernel,
        grid=(t // PEER_TB,),
        in_specs=[pl.BlockSpec((PEER_TB, q.shape[1]), lambda i: (i, 0)),
                  pl.BlockSpec(sub_keys.shape, lambda i: (0, 0, 0)),
                  pl.BlockSpec(g0.shape, lambda i: (0, 0)),
                  pl.BlockSpec(g1.shape, lambda i: (0, 0))],
        out_specs=[pl.BlockSpec((PEER_K, PEER_TB), lambda i: (0, i)),
                   pl.BlockSpec((PEER_K, PEER_TB), lambda i: (0, i))],
        out_shape=[jax.ShapeDtypeStruct((PEER_K, t), jnp.int32),
                   jax.ShapeDtypeStruct((PEER_K, t), jnp.float32)],
        name="peer_topk",
    )(q, sub_keys, g0, g1)


def _pack_table(tab):
    e = tab.shape[0]
    tb = tab.astype(jnp.bfloat16)
    lo = lax.bitcast_convert_type(tb[:, :HALF_D], jnp.uint16).astype(jnp.uint32)
    hi = lax.bitcast_convert_type(tb[:, HALF_D:], jnp.uint16).astype(jnp.uint32)
    return (lo | (hi << 16)).reshape(e * ROW_WORDS, 128)


def _token_groups(eight_tokens):
    def trip(g, carry):
        for sub in range(PEER_GROUP // 8):
            eight_tokens(pl.multiple_of(g * PEER_GROUP + sub * 8, 8))
        return carry

    lax.fori_loop(0, GATHER_TB // PEER_GROUP, trip, 0)


def _gather_rows(idx_ref, tok, tab_ref, scr):
    nchunk = HALF_D // 128
    for k in range(PEER_K):
        e = idx_ref[tok * PEER_K + k]
        scr[k // 8, pl.ds(k % 8, nchunk, stride=8), :] = tab_ref[pl.ds(pl.multiple_of(e, ROW_WORDS), ROW_WORDS), :]
    gm = jnp.concatenate(
        [scr[:, j * 8:(j + 1) * 8, :].reshape(PEER_K, 128) for j in range(nchunk)], axis=1)
    return pltpu.bitcast(gm, jnp.bfloat16)


def _peer_u_kernel(idx_ref, ut_ref, tab_ref, out_ref, scr0, scr1):
    shp = (8, 2 * PEER_K)
    row = lax.broadcasted_iota(jnp.int32, shp, 0)
    even = (lax.broadcasted_iota(jnp.int32, shp, 1) & 1) == 0

    def eight_tokens(base):
        ut8 = ut_ref[pl.ds(base, 8), :]
        lhs = jnp.concatenate([ut8[:, :HALF_D], ut8[:, HALF_D:]], axis=0).astype(jnp.bfloat16)
        acc = jnp.zeros(shp, jnp.float32)
        for i in range(8):
            b = _gather_rows(idx_ref, base + i, tab_ref, (scr0, scr1)[i % 2])
            r = lax.dot_general(lhs, b, (((1,), (1,)), ((), ())),
                                preferred_element_type=jnp.float32)
            acc = jnp.where(row == i, jnp.where(even, r[0:8], r[8:16]), acc)
        out_ref[pl.ds(base, 8), :] = acc

    _token_groups(eight_tokens)


def _peer_v_kernel(idx_ref, a_ref, gate_ref, h_ref, nw_ref, tab_ref, out_ref, scr0, scr1):
    shp = (8, 2 * PEER_K)
    row = lax.broadcasted_iota(jnp.int32, (8, HALF_D), 0)
    even = (lax.broadcasted_iota(jnp.int32, shp, 1) & 1) == 0

    def eight_tokens(base):
        a8 = a_ref[pl.ds(base, 8), :]
        s = a8 + pltpu.roll(a8, 2 * PEER_K - 1, axis=1)
        act = jnp.where(even, s, pltpu.roll(s, 1, axis=1))
        gelu = 0.5 * act * (1.0 + lax.erf(act * (2.0 ** -0.5)))
        w = gelu * gate_ref[pl.ds(base, 8), :]
        lhs = jnp.concatenate([jnp.where(even, w, 0.0), jnp.where(even, 0.0, w)],
                              axis=0).astype(jnp.bfloat16)
        acc_lo = jnp.zeros((8, HALF_D), jnp.float32)
        acc_hi = jnp.zeros((8, HALF_D), jnp.float32)
        for i in range(8):
            b = _gather_rows(idx_ref, base + i, tab_ref, (scr0, scr1)[i % 2])
            r = jnp.dot(lhs, b, preferred_element_type=jnp.float32)
            acc_lo = jnp.where(row == i, r[0:8], acc_lo)
            acc_hi = jnp.where(row == i, r[8:16], acc_hi)
        y_lo = h_ref[pl.ds(base, 8), 0:HALF_D] + acc_lo
        y_hi = h_ref[pl.ds(base, 8), HALF_D:] + acc_hi
        ms = (jnp.sum(y_lo * y_lo, axis=-1, keepdims=True)
              + jnp.sum(y_hi * y_hi, axis=-1, keepdims=True)) * (1.0 / D_MODEL)
        inv = lax.rsqrt(ms + EPS)
        out_ref[pl.ds(base, 8), 0:HALF_D] = y_lo * inv * nw_ref[:, 0:HALF_D]
        out_ref[pl.ds(base, 8), HALF_D:] = y_hi * inv * nw_ref[:, HALF_D:]

    _token_groups(eight_tokens)


def _peer_scratch():
    return [pltpu.VMEM((PEER_K // 8, 8 * (HALF_D // 128), 128), jnp.uint32) for _ in range(2)]


def _peer_params(tab):
    return _params(1, _nbytes(tab))


def _peer_u_pass(idx_flat, un, tab):
    t = un.shape[0]
    return pl.pallas_call(
        _peer_u_kernel,
        grid=(t // GATHER_TB,),
        in_specs=[pl.BlockSpec((GATHER_TB * PEER_K,), lambda i: (i,), memory_space=pltpu.SMEM),
                  pl.BlockSpec((GATHER_TB, D_MODEL), lambda i: (i, 0)),
                  pl.BlockSpec(memory_space=pltpu.VMEM)],
        out_specs=pl.BlockSpec((GATHER_TB, 2 * PEER_K), lambda i: (i, 0)),
        out_shape=jax.ShapeDtypeStruct((t, 2 * PEER_K), jnp.float32),
        scratch_shapes=_peer_scratch(),
        compiler_params=_peer_params(tab),
        name="peer_u_pass",
    )(idx_flat, un, tab)


def _peer_v_pass(idx_flat, a, gate2, h, norm_w, tab):
    t = a.shape[0]
    row = lambda n: pl.BlockSpec((GATHER_TB, n), lambda i: (i, 0))
    return pl.pallas_call(
        _peer_v_kernel,
        grid=(t // GATHER_TB,),
        in_specs=[pl.BlockSpec((GATHER_TB * PEER_K,), lambda i: (i,), memory_space=pltpu.SMEM),
                  row(2 * PEER_K), row(2 * PEER_K), row(D_MODEL),
                  pl.BlockSpec((1, D_MODEL), lambda i: (0, 0)),
                  pl.BlockSpec(memory_space=pltpu.VMEM)],
        out_specs=row(D_MODEL),
        out_shape=jax.ShapeDtypeStruct((t, D_MODEL), jnp.float32),
        scratch_shapes=_peer_scratch(),
        compiler_params=_peer_params(tab),
        name="peer_v_pass",
    )(idx_flat, a, gate2, h, norm_w.reshape(1, D_MODEL), tab)


def _peer_and_final_norm(h, un, q, sub_keys, peer_u, peer_v, norm_final_w):
    idx_t, gate_t = _peer_topk(q, sub_keys)
    idx_flat = idx_t.T.reshape(-1)
    gate2 = jnp.repeat(gate_t.T, 2, axis=1)
    a = _peer_u_pass(idx_flat, un, _pack_table(peer_u))
    return _peer_v_pass(idx_flat, a, gate2, h, norm_final_w, _pack_table(peer_v))


def kernel(x, positions, norm_mix_w, w_in, conv_w, conv_b, dt_bias_fwd, dt_bias_bwd, a_log_fwd, a_log_bwd, d_skip, ssd_norm_w, q_a_norm_w, w_q_b, kv_a_norm_w, w_kv_b, w_proj_ssd, w_proj_mla, w_out, norm_ffn_w, peer_w_q, peer_sub_keys, peer_u, peer_v, norm_final_w):
    b, s, _ = x.shape
    x2 = x.reshape(b * s, D_MODEL)
    z, xbc, dt128, q_a, kv_a, kr, krs, gate_logits = _in_proj(x2, norm_mix_w, w_in)
    y_ssd = _ssd_pallas(xbc.reshape(b, s, -1), dt128.reshape(b, s, -1), z.reshape(b, s, -1), conv_w, conv_b,
                        dt_bias_fwd, dt_bias_bwd, a_log_fwd, a_log_bwd, d_skip, ssd_norm_w)

    y_mla = _mla_pallas(positions, q_a, kv_a, kr, krs, q_a_norm_w, kv_a_norm_w, w_q_b, w_kv_b)

    h2, un, q_peer = _merge(x2, y_ssd.reshape(b * s, -1), y_mla, gate_logits, norm_ffn_w,
                            w_proj_ssd, w_proj_mla, w_out, peer_w_q)
    out = _peer_and_final_norm(h2, un, q_peer, peer_sub_keys, peer_u, peer_v, norm_final_w)
    return out.reshape(b, s, D_MODEL)
```

```python
import functools

import jax
import jax.numpy as jnp
from jax import lax
from jax.experimental import pallas as pl
from jax.experimental.pallas import tpu as pltpu

D_MODEL = 1024
SSD_D_INNER = 1024
SSD_HEADDIM = 64
SSD_HEADS = 16
SSD_GROUPS = 2
SSD_HPG = 8
SSD_STATE = 128
CHUNK = 128
SSD_CONV_DIM = SSD_D_INNER + 2 * SSD_GROUPS * SSD_STATE
MLA_HEADS = 16
QK_NOPE = 64
QK_ROPE = 32
V_HEAD = 64
Q_LORA = 384
KV_LORA = 256
ROPE_THETA = 10000.0
PEER_HEADS = 8
N_KEYS = 128
KEY_HALF = 128
PEER_TOPK = 16
EPS = 1e-6
HEAD_PAD = 128
QK_DIM = QK_NOPE + QK_ROPE
LOG2E = 1.4426950408889634
IN_SIZES = (SSD_D_INNER, SSD_CONV_DIM, 2 * SSD_HEADS, Q_LORA, KV_LORA, QK_ROPE, 2 * D_MODEL)


def _split_points(sizes):
    pts, acc = [], 0
    for s in sizes[:-1]:
        acc += s
        pts.append(acc)
    return pts


VMEM_BYTES_V7X = 64 << 20
VMEM_CAP = VMEM_BYTES_V7X - (8 << 20)
BLOCK_ALLOWANCE = 24 << 20


def _params(grid_rank, resident_bytes=0):
    return pltpu.CompilerParams(dimension_semantics=("arbitrary",) * grid_rank,
                                vmem_limit_bytes=min(resident_bytes + BLOCK_ALLOWANCE, VMEM_CAP))


def _nbytes(*arrays):
    return sum(a.size * a.dtype.itemsize for a in arrays)


def _rms_rows(x, w):
    return x * lax.rsqrt(jnp.mean(x * x, axis=-1, keepdims=True) + EPS) * w


IN_TB = 256
DT_PAD = 128
GATE_W = 2 * D_MODEL
IN_OUT_WIDTHS = (SSD_D_INNER, SSD_CONV_DIM, DT_PAD, Q_LORA, KV_LORA, HEAD_PAD, HEAD_PAD, GATE_W)


def _in_proj_kernel(x_ref, nw_ref, w_ref, *out_refs):
    u = _rms_rows(x_ref[...], nw_ref[...]).astype(jnp.bfloat16)
    col = 0
    for o_ref, width in zip(out_refs, IN_OUT_WIDTHS):
        o_ref[...] = jnp.dot(u, w_ref[:, col:col + width], preferred_element_type=jnp.float32)
        col += width


def _arrange_w_in(w_in):
    z, xbc, dt, qa, kva, kr, gates = jnp.split(w_in, _split_points(IN_SIZES), axis=1)
    d = w_in.shape[0]
    half = QK_ROPE // 2
    dt_p = jnp.pad(dt, ((0, 0), (0, DT_PAD - dt.shape[1])))
    kr3 = kr.reshape(d, 1, QK_ROPE)
    kr_p = _pad_heads(None, kr3)
    krs_p = _pad_heads(None, _rope_partner(kr3[..., :half], kr3[..., half:]))
    return jnp.concatenate([z, xbc, dt_p, qa, kva, kr_p, krs_p, gates], axis=1).astype(jnp.bfloat16)


def _in_proj(x2, norm_w, w_in):
    t, d = x2.shape
    w = _arrange_w_in(w_in)
    tb = min(IN_TB, t)
    return pl.pallas_call(
        _in_proj_kernel,
        grid=(t // tb,),
        in_specs=[pl.BlockSpec((tb, d), lambda i: (i, 0)),
                  pl.BlockSpec((1, d), lambda i: (0, 0)),
                  pl.BlockSpec(w.shape, lambda i: (0, 0))],
        out_specs=[pl.BlockSpec((tb, n), lambda i: (i, 0)) for n in IN_OUT_WIDTHS],
        out_shape=[jax.ShapeDtypeStruct((t, n), jnp.float32) for n in IN_OUT_WIDTHS],
        compiler_params=_params(1, 2 * _nbytes(w)),
        name="in_proj",
    )(x2, norm_w.reshape(1, d), w)


HEAD_W = SSD_HEADS * SSD_HEADDIM
GROUP_W = HEAD_W // SSD_GROUPS
BC_W = SSD_GROUPS * SSD_STATE


def _split3(x):
    f32, bf16 = jnp.float32, jnp.bfloat16
    hi = x.astype(bf16)
    r = x - hi.astype(f32)
    mid = r.astype(bf16)
    lo = (r - mid.astype(f32)).astype(bf16)
    return hi, mid, lo


def _ssd_kernel(direction, xc_ref, xp_ref, xn_ref, dt_ref, cw_ref, cb_ref, dtb_ref, a_ref, e_ref,
                tri_ref, *rest):
    f32, bf16 = jnp.float32, jnp.bfloat16
    fwd = direction == 0
    if fwd:
        dsk_ref, y_ref, h_scr = rest
    else:
        y1_ref, z_ref, nw_ref, y_ref, h_scr = rest
    c = pl.program_id(1)
    nc = pl.num_programs(1)
    cm = c if fwd else nc - 1 - c

    @pl.when(c == 0)
    def _():
        h_scr[...] = jnp.zeros_like(h_scr)

    xc = xc_ref[0]
    row = lax.broadcasted_iota(jnp.int32, xc.shape, 0)
    prev_row = jnp.where(cm > 0, xp_ref[0, 7:8, :], 0.0)
    next_row = jnp.where(cm < nc - 1, xn_ref[0, 0:1, :], 0.0)
    x_m1 = jnp.where(row == 0, prev_row, pltpu.roll(xc, 1, axis=0))
    x_p1 = jnp.where(row == CHUNK - 1, next_row, pltpu.roll(xc, CHUNK - 1, axis=0))
    conv = cw_ref[0:1, :] * x_m1 + cw_ref[1:2, :] * xc + cw_ref[2:3, :] * x_p1 + cb_ref[...]
    act = conv * jax.nn.sigmoid(conv)
    xs = act[:, :HEAD_W]

    dtr = dt_ref[0] + dtb_ref[...]
    dt = jnp.maximum(dtr, 0.0) + jnp.log(1.0 + jnp.exp(-jnp.abs(dtr)))
    a = dt * a_ref[...]
    cs = sum(jnp.dot(tri_ref[...], part, preferred_element_type=f32) for part in _split3(a))
    expand = lambda v: sum(jnp.dot(part, e_ref[...], preferred_element_type=f32) for part in _split3(v))
    cs_e = expand(cs)
    tot_e = cs_e[CHUNK - 1:CHUNK, :]
    if fwd:
        pos = cs
        w_state = jnp.exp(tot_e - cs_e)
        out_scale = jnp.exp(cs_e)
    else:
        pos = cs - a
        e_e = cs_e - expand(a)
        w_state = jnp.exp(e_e)
        out_scale = jnp.exp(tot_e - e_e)
    pos_t = pos.T
    xd = xs * expand(dt)
    xdb = xd.astype(bf16)
    xw = (xd * w_state).astype(bf16)
    chunk_decay = jnp.exp(tot_e)

    li = lax.broadcasted_iota(jnp.int32, (CHUNK, CHUNK), 0)
    si = lax.broadcasted_iota(jnp.int32, (CHUNK, CHUNK), 1)
    mask = (si <= li) if fwd else (si >= li)
    for g in range(SSD_GROUPS):
        gsl = slice(g * GROUP_W, (g + 1) * GROUP_W)
        bg = act[:, HEAD_W + g * SSD_STATE:HEAD_W + (g + 1) * SSD_STATE]
        cg = act[:, HEAD_W + BC_W + g * SSD_STATE:HEAD_W + BC_W + (g + 1) * SSD_STATE].astype(bf16)
        cb = lax.dot_general(cg, bg.astype(bf16), (((1,), (1,)), ((), ())), preferred_element_type=f32)
        h_in = h_scr[:, gsl]
        y_off = jnp.dot(cg, h_in.astype(bf16), preferred_element_type=f32) * out_scale[:, gsl]
        s_new = jnp.dot(bg.T.astype(bf16), xw[:, gsl], preferred_element_type=f32)
        h_scr[:, gsl] = h_in * chunk_decay[:, gsl] + s_new
        for r in range(SSD_HPG):
            hd = g * SSD_HPG + r
            lane = direction * SSD_HEADS + hd
            col = pos[:, lane:lane + 1]
            rw = pos_t[lane:lane + 1, :]
            diff = (col - rw) if fwd else (rw - col)
            m = (cb * jnp.exp(jnp.where(mask, diff, NEG_INF))).astype(bf16)
            hsl = slice(hd * SSD_HEADDIM, (hd + 1) * SSD_HEADDIM)
            y_d = jnp.dot(m, xdb[:, hsl], preferred_element_type=f32)
            y_ref[0, :, hsl] = y_d + y_off[:, r * SSD_HEADDIM:(r + 1) * SSD_HEADDIM]

    if fwd:
        y_ref[0] = y_ref[0] + xs * dsk_ref[...]
    else:
        yg = (y_ref[0] + y1_ref[0]) * (z_ref[0] * jax.nn.sigmoid(z_ref[0]))
        for g in range(SSD_GROUPS):
            gsl = slice(g * GROUP_W, (g + 1) * GROUP_W)
            y_ref[0, :, gsl] = _rms_rows(yg[:, gsl], nw_ref[:, gsl])


def _ssd_pass(direction, xbc, dt128, consts, extra):
    b, s, cw_ = xbc.shape
    nc = s // CHUNK
    per8 = CHUNK // 8
    cm = (lambda c: c) if direction == 0 else (lambda c: nc - 1 - c)
    cur = lambda n: pl.BlockSpec((1, CHUNK, n), lambda bi, c: (bi, cm(c), 0))
    prev = pl.BlockSpec((1, 8, cw_), lambda bi, c: (bi, jnp.maximum(cm(c) * per8 - 1, 0), 0))
    nxt = pl.BlockSpec((1, 8, cw_), lambda bi, c: (bi, jnp.minimum((cm(c) + 1) * per8, s // 8 - 1), 0))
    full = lambda a: pl.BlockSpec(a.shape, lambda bi, c: (0,) * a.ndim)
    if direction == 0:
        extra_specs = [full(extra[0])]
    else:
        extra_specs = [cur(HEAD_W), cur(HEAD_W), full(extra[2])]
    return pl.pallas_call(
        functools.partial(_ssd_kernel, direction),
        grid=(b, nc),
        in_specs=[cur(cw_), prev, nxt, cur(DT_PAD)] + [full(a) for a in consts] + extra_specs,
        out_specs=cur(HEAD_W),
        out_shape=jax.ShapeDtypeStruct((b, s, HEAD_W), jnp.float32),
        scratch_shapes=[pltpu.VMEM((SSD_STATE, HEAD_W), jnp.float32)],
        compiler_params=_params(2),
        name="ssd_fwd" if direction == 0 else "ssd_bwd",
    )(xbc, xbc, xbc, dt128, *consts, *extra)


def _ssd_pallas(xbc, dt128, z, conv_w, conv_b, dt_bias_fwd, dt_bias_bwd, a_log_fwd, a_log_bwd,
                d_skip, ssd_norm_w):
    bf16 = jnp.bfloat16
    pad = lambda v: jnp.pad(v, (0, DT_PAD - v.shape[0])).reshape(1, DT_PAD)
    dtb = pad(jnp.concatenate([dt_bias_fwd, dt_bias_bwd]))
    a_neg = pad(-jnp.exp(jnp.concatenate([a_log_fwd, a_log_bwd])))
    tri = (jnp.arange(CHUNK)[None, :] <= jnp.arange(CHUNK)[:, None]).astype(bf16)
    head_of_lane = jnp.arange(HEAD_W) // SSD_HEADDIM
    y = None
    for direction in range(2):
        e = (jnp.arange(DT_PAD)[:, None] == (direction * SSD_HEADS + head_of_lane)[None, :]).astype(bf16)
        consts = (conv_w, conv_b.reshape(1, -1), dtb, a_neg, e, tri)
        if direction == 0:
            extra = (jnp.repeat(d_skip, SSD_HEADDIM).reshape(1, HEAD_W),)
        else:
            extra = (y, z, ssd_norm_w.reshape(1, HEAD_W))
        y = _ssd_pass(direction, xbc, dt128, consts, extra)
    return y


MERGE_TB = 256


def _merge_kernel(x_ref, ys_ref, ym_ref, g_ref, nw_ref, pa_ref, pb_ref, wo_ref, wq_ref,
                  h_ref, un_ref, q_ref):
    bf16 = jnp.bfloat16
    f32 = jnp.float32
    g = jax.nn.sigmoid(g_ref[...])
    ma = jnp.dot(ys_ref[...].astype(bf16), pa_ref[...], preferred_element_type=f32)
    mb = jnp.dot(ym_ref[...].astype(bf16), pb_ref[...], preferred_element_type=f32)
    m = g[:, :D_MODEL] * ma + g[:, D_MODEL:] * mb
    h = x_ref[...] + jnp.dot(m.astype(bf16), wo_ref[...], preferred_element_type=f32)
    h_ref[...] = h
    un = _rms_rows(h, nw_ref[...])
    un_ref[...] = un
    q_ref[...] = jnp.dot(un.astype(bf16), wq_ref[...], preferred_element_type=f32)


def _merge(x2, y_ssd, y_mla, gates, norm_ffn_w, w_proj_ssd, w_proj_mla, w_out, peer_w_q):
    t, d = x2.shape
    bf16 = jnp.bfloat16
    tb = min(MERGE_TB, t)
    nq = peer_w_q.shape[1]
    row = lambda n: pl.BlockSpec((tb, n), lambda i: (i, 0))
    full = lambda a: pl.BlockSpec(a.shape, lambda i: (0, 0))
    ws = [w.astype(bf16) for w in (w_proj_ssd, w_proj_mla, w_out, peer_w_q)]
    nw = norm_ffn_w.reshape(1, d)
    return pl.pallas_call(
        _merge_kernel,
        grid=(t // tb,),
        in_specs=[row(d), row(d), row(d), row(GATE_W), full(nw)] + [full(w) for w in ws],
        out_specs=[row(d), row(d), row(nq)],
        out_shape=[jax.ShapeDtypeStruct((t, d), jnp.float32), jax.ShapeDtypeStruct((t, d), jnp.float32),
                   jax.ShapeDtypeStruct((t, nq), jnp.float32)],
        compiler_params=_params(1, 2 * _nbytes(*ws)),
        name="merge",
    )(x2, y_ssd, y_mla, gates, nw, *ws)


MLA_TB = 512
ATT_TQ = 256
ATT_HEADS = 8


def _mla_proj_kernel(pos_ref, qa_ref, kva_ref, kr_ref, krs_ref, qnw_ref, kvnw_ref, inv_ref,
                     wq_ref, wqs_ref, wk_ref, wv_ref, q_out, k_out, v_out):
    bf16 = jnp.bfloat16
    ang = pos_ref[...].astype(jnp.float32) * inv_ref[...]
    cos, sin = jnp.cos(ang), jnp.sin(ang)
    qa = _rms_rows(qa_ref[...], qnw_ref[...]).astype(bf16)
    q = jnp.dot(qa, wq_ref[...], preferred_element_type=jnp.float32)
    qs = jnp.dot(qa, wqs_ref[...], preferred_element_type=jnp.float32)
    kva = _rms_rows(kva_ref[...], kvnw_ref[...]).astype(bf16)
    k = jnp.dot(kva, wk_ref[...], preferred_element_type=jnp.float32)
    v_out[...] = jnp.dot(kva, wv_ref[...], preferred_element_type=jnp.float32).astype(bf16)
    kro = kr_ref[...] * cos + krs_ref[...] * sin
    c = (QK_DIM ** -0.5) * LOG2E
    for h in range(MLA_HEADS):
        sl = slice(h * HEAD_PAD, (h + 1) * HEAD_PAD)
        q_out[:, sl] = ((q[:, sl] * cos + qs[:, sl] * sin) * c).astype(bf16)
        k_out[:, sl] = (k[:, sl] + kro).astype(bf16)


def _rope_partner(w1, w2):
    return jnp.concatenate([-w2, w1], axis=-1)


def _pad_heads(nope, rope):
    ref = nope if nope is not None else rope
    lead = ref.shape[:-1]
    z = lambda n: jnp.zeros(lead + (n,), ref.dtype)
    parts = [nope if nope is not None else z(QK_NOPE), rope if rope is not None else z(QK_ROPE),
             z(HEAD_PAD - QK_DIM)]
    out = jnp.concatenate(parts, axis=-1)
    return out.reshape(out.shape[:-2] + (out.shape[-2] * HEAD_PAD,))


def _mla_proj(positions, q_a, kv_a, kr, krs, q_a_norm_w, kv_a_norm_w, w_q_b, w_kv_b):
    t = q_a.shape[0]
    bf16 = jnp.bfloat16
    half = QK_ROPE // 2
    wq = w_q_b.reshape(Q_LORA, MLA_HEADS, QK_DIM)
    wq_n, wq_r = wq[..., :QK_NOPE], wq[..., QK_NOPE:]
    wq_main = _pad_heads(wq_n, wq_r).astype(bf16)
    wq_part = _pad_heads(None, _rope_partner(wq_r[..., :half], wq_r[..., half:])).astype(bf16)
    wkv = w_kv_b.reshape(KV_LORA, MLA_HEADS, QK_NOPE + V_HEAD)
    wk = _pad_heads(wkv[..., :QK_NOPE], None).astype(bf16)
    wv = wkv[..., QK_NOPE:].reshape(KV_LORA, MLA_HEADS * V_HEAD).astype(bf16)
    inv = 1.0 / (ROPE_THETA ** (jnp.arange(0, QK_ROPE, 2, dtype=jnp.float32) / QK_ROPE))
    inv_lane = jnp.concatenate([jnp.zeros((QK_NOPE,), jnp.float32), inv, inv,
                                jnp.zeros((HEAD_PAD - QK_DIM,), jnp.float32)]).reshape(1, HEAD_PAD)
    tb = min(MLA_TB, t)
    row = lambda n: pl.BlockSpec((tb, n), lambda i: (i, 0))
    full = lambda a: pl.BlockSpec(a.shape, lambda i: (0,) * a.ndim)
    qn = q_a_norm_w.reshape(1, Q_LORA)
    kvn = kv_a_norm_w.reshape(1, KV_LORA)
    args = (positions.reshape(t, 1), q_a, kv_a, kr, krs, qn, kvn, inv_lane, wq_main, wq_part, wk, wv)
    in_specs = [row(1), row(Q_LORA), row(KV_LORA), row(HEAD_PAD), row(HEAD_PAD)] + [full(a) for a in args[5:]]
    nq = MLA_HEADS * HEAD_PAD
    return pl.pallas_call(
        _mla_proj_kernel,
        grid=(t // tb,),
        in_specs=in_specs,
        out_specs=[row(nq), row(nq), row(MLA_HEADS * V_HEAD)],
        out_shape=[jax.ShapeDtypeStruct((t, nq), bf16), jax.ShapeDtypeStruct((t, nq), bf16),
                   jax.ShapeDtypeStruct((t, MLA_HEADS * V_HEAD), bf16)],
        compiler_params=_params(1, 2 * _nbytes(wq_main, wq_part, wk, wv)),
        name="mla_proj",
    )(*args)


def _attn_kernel(q_ref, k_ref, v_ref, o_ref):
    tq = q_ref.shape[1]
    lane = lax.broadcasted_iota(jnp.int32, (tq, 2 * V_HEAD), 1)
    for pair in range(ATT_HEADS // 2):
        vsl = slice(pair * 2 * V_HEAD, (pair + 1) * 2 * V_HEAD)
        v = v_ref[0, :, vsl]
        outs = []
        for hh in range(2 * pair, 2 * pair + 2):
            sl = slice(hh * HEAD_PAD, (hh + 1) * HEAD_PAD)
            s = lax.dot_general(q_ref[0, :, sl], k_ref[0, :, sl], (((1,), (1,)), ((), ())),
                                preferred_element_type=jnp.float32)
            m = jnp.max(s, axis=-1, keepdims=True)
            p = jnp.exp2(s - m)
            l = jnp.sum(p, axis=-1, keepdims=True)
            pv = jnp.dot(p.astype(jnp.bfloat16), v, preferred_element_type=jnp.float32)
            outs.append(pv / l)
        o_ref[0, :, vsl] = jnp.where(lane < V_HEAD, outs[0], outs[1]).astype(o_ref.dtype)


def _attention(q, k, v):
    b, s, _ = q.shape
    tq = min(ATT_TQ, s)
    qw, vw = ATT_HEADS * HEAD_PAD, ATT_HEADS * V_HEAD
    return pl.pallas_call(
        _attn_kernel,
        grid=(b, MLA_HEADS // ATT_HEADS, s // tq),
        in_specs=[pl.BlockSpec((1, tq, qw), lambda bi, j, i: (bi, i, j)),
                  pl.BlockSpec((1, s, qw), lambda bi, j, i: (bi, 0, j)),
                  pl.BlockSpec((1, s, vw), lambda bi, j, i: (bi, 0, j))],
        out_specs=pl.BlockSpec((1, tq, vw), lambda bi, j, i: (bi, i, j)),
        out_shape=jax.ShapeDtypeStruct((b, s, MLA_HEADS * V_HEAD), jnp.bfloat16),
        compiler_params=_params(3, 2 * 2 * s * (qw + vw)),
        name="mla_attention",
    )(q, k, v)


def _mla_pallas(positions, q_a, kv_a, kr, krs, q_a_norm_w, kv_a_norm_w, w_q_b, w_kv_b):
    b, s = positions.shape
    q, k, v = _mla_proj(positions, q_a, kv_a, kr, krs, q_a_norm_w, kv_a_norm_w, w_q_b, w_kv_b)
    y = _attention(q.reshape(b, s, -1), k.reshape(b, s, -1), v.reshape(b, s, -1))
    return y.reshape(b * s, -1)


PEER_TB = 128
PEER_K = PEER_HEADS * PEER_TOPK
PEER_GROUP = 128
GATHER_TB = 256
HALF_D = D_MODEL // 2
NEG_INF = float("-inf")


def _top16_rows(xs, extras=None):
    n = len(xs)
    rows = xs[0].shape[0]
    iota = lax.broadcasted_iota(jnp.int32, xs[0].shape, 0).astype(jnp.float32)
    xs = list(xs)
    vals, poss, exts = [[] for _ in xs], [[] for _ in xs], [[] for _ in xs]
    for _ in range(PEER_TOPK):
        for j in range(n):
            m = jnp.max(xs[j], axis=0, keepdims=True)
            p = jnp.min(jnp.where(xs[j] == m, iota, float(rows)), axis=0, keepdims=True)
            hit = iota == p
            if extras is not None:
                exts[j].append(jnp.sum(jnp.where(hit, extras[j], 0.0), axis=0, keepdims=True))
            xs[j] = jnp.where(hit, NEG_INF, xs[j])
            vals[j].append(m)
            poss[j].append(p)
    cat = lambda parts: jnp.concatenate(parts, axis=0)
    return [(cat(vals[j]), cat(poss[j]), cat(exts[j]) if extras is not None else None) for j in range(n)]


_CAND_PAIRS = [(a, b) for a in range(PEER_TOPK) for b in range(PEER_TOPK) if (a + 1) * (b + 1) <= PEER_TOPK]
CAND_ROWS = -(-len(_CAND_PAIRS) // 8) * 8
ROW_WORDS = 4
TOPK_LOCKSTEP_HEADS = 4


def _cand_select():
    g0 = [[0.0] * PEER_TOPK for _ in range(CAND_ROWS)]
    g1 = [[0.0] * PEER_TOPK for _ in range(CAND_ROWS)]
    for r, (a, b) in enumerate(_CAND_PAIRS):
        g0[r][a] = 1.0
        g1[r][b] = 1.0
    return jnp.array(g0, jnp.float32), jnp.array(g1, jnp.float32)


def _peer_topk_kernel(q_ref, keys_ref, g0_ref, g1_ref, dup_ref, idx_ref, gate_ref):
    f32 = jnp.float32
    pick = lambda g_ref, v: jnp.dot(g_ref[...], v, precision=lax.Precision.HIGHEST, preferred_element_type=f32)
    pad_row = lax.broadcasted_iota(jnp.int32, (CAND_ROWS, q_ref.shape[0]), 0) >= len(_CAND_PAIRS)
    idx_rows, gate_rows = [], []
    for h0 in range(0, PEER_HEADS, TOPK_LOCKSTEP_HEADS):
        heads = range(h0, h0 + TOPK_LOCKSTEP_HEADS)
        scores = []
        for h in heads:
            for i in range(2):
                c0 = (h * 2 + i) * KEY_HALF
                scores.append(lax.dot_general(keys_ref[i], q_ref[:, c0:c0 + KEY_HALF], (((1,), (1,)), ((), ())),
                                              preferred_element_type=f32))
        tops = _top16_rows(scores)
        cands, cidxs = [], []
        for j in range(len(heads)):
            (v0, p0, _), (v1, p1, _) = tops[2 * j], tops[2 * j + 1]
            cands.append(jnp.where(pad_row, NEG_INF, pick(g0_ref, v0) + pick(g1_ref, v1)))
            cidxs.append((pick(g0_ref, p0) * N_KEYS + pick(g1_ref, p1)) * ROW_WORDS)
        for best, _, eidx in _top16_rows(cands, cidxs):
            e = jnp.exp(best - best[0:1, :])
            idx_rows.append(eidx)
            gate_rows.append(e / jnp.sum(e, axis=0, keepdims=True))
    idx_ref[...] = jnp.concatenate(idx_rows, axis=0).T.astype(jnp.int32)
    gate_t = jnp.concatenate(gate_rows, axis=0).T
    gate_ref[...] = sum(jnp.dot(part, dup_ref[...], preferred_element_type=f32) for part in _split3(gate_t))


def _peer_topk(q, sub_keys):
    t = q.shape[0]
    g0, g1 = _cand_select()
    dup = (jnp.arange(PEER_K)[:, None] == jnp.arange(2 * PEER_K)[None, :] // 2).astype(jnp.bfloat16)
    full = lambda a: pl.BlockSpec(a.shape, lambda i: (0,) * a.ndim)
    return pl.pallas_call(
        _peer_topk_kernel,
        grid=(t // PEER_TB,),
        in_specs=[pl.BlockSpec((PEER_TB, q.shape[1]), lambda i: (i, 0)),
                  full(sub_keys), full(g0), full(g1), full(dup)],
        out_specs=[pl.BlockSpec((PEER_TB, PEER_K), lambda i: (i, 0)),
                   pl.BlockSpec((PEER_TB, 2 * PEER_K), lambda i: (i, 0))],
        out_shape=[jax.ShapeDtypeStruct((t, PEER_K), jnp.int32),
                   jax.ShapeDtypeStruct((t, 2 * PEER_K), jnp.float32)],
        name="peer_topk",
    )(q, sub_keys, g0, g1, dup)


def _pack_table(tab):
    e = tab.shape[0]
    tb = tab.astype(jnp.bfloat16)
    lo = lax.bitcast_convert_type(tb[:, :HALF_D], jnp.uint16).astype(jnp.uint32)
    hi = lax.bitcast_convert_type(tb[:, HALF_D:], jnp.uint16).astype(jnp.uint32)
    return (lo | (hi << 16)).reshape(e * ROW_WORDS, 128)


def _token_groups(eight_tokens):
    def trip(g, carry):
        for sub in range(PEER_GROUP // 8):
            eight_tokens(pl.multiple_of(g * PEER_GROUP + sub * 8, 8))
        return carry

    lax.fori_loop(0, GATHER_TB // PEER_GROUP, trip, 0)


def _gather_rows(idx_ref, tok, tab_ref, scr):
    nchunk = HALF_D // 128
    for k in range(PEER_K):
        e = idx_ref[tok * PEER_K + k]
        scr[k // 8, pl.ds(k % 8, nchunk, stride=8), :] = tab_ref[pl.ds(pl.multiple_of(e, ROW_WORDS), ROW_WORDS), :]
    gm = jnp.concatenate(
        [scr[:, j * 8:(j + 1) * 8, :].reshape(PEER_K, 128) for j in range(nchunk)], axis=1)
    return pltpu.bitcast(gm, jnp.bfloat16)


def _peer_u_kernel(idx_ref, ut_ref, tab_ref, out_ref, scr0, scr1):
    shp = (8, 2 * PEER_K)
    row = lax.broadcasted_iota(jnp.int32, shp, 0)
    even = (lax.broadcasted_iota(jnp.int32, shp, 1) & 1) == 0

    def eight_tokens(base):
        ut8 = ut_ref[pl.ds(base, 8), :]
        lhs = jnp.concatenate([ut8[:, :HALF_D], ut8[:, HALF_D:]], axis=0).astype(jnp.bfloat16)
        acc = jnp.zeros(shp, jnp.float32)
        for i in range(8):
            b = _gather_rows(idx_ref, base + i, tab_ref, (scr0, scr1)[i % 2])
            r = lax.dot_general(lhs, b, (((1,), (1,)), ((), ())),
                                preferred_element_type=jnp.float32)
            acc = jnp.where(row == i, jnp.where(even, r[0:8], r[8:16]), acc)
        out_ref[pl.ds(base, 8), :] = acc

    _token_groups(eight_tokens)


def _peer_v_kernel(idx_ref, a_ref, gate_ref, h_ref, nw_ref, tab_ref, out_ref, scr0, scr1):
    shp = (8, 2 * PEER_K)
    row = lax.broadcasted_iota(jnp.int32, (8, HALF_D), 0)
    even = (lax.broadcasted_iota(jnp.int32, shp, 1) & 1) == 0

    def eight_tokens(base):
        a8 = a_ref[pl.ds(base, 8), :]
        s = a8 + pltpu.roll(a8, 2 * PEER_K - 1, axis=1)
        act = jnp.where(even, s, pltpu.roll(s, 1, axis=1))
        gelu = 0.5 * act * (1.0 + lax.erf(act * (2.0 ** -0.5)))
        w = gelu * gate_ref[pl.ds(base, 8), :]
        lhs = jnp.concatenate([jnp.where(even, w, 0.0), jnp.where(even, 0.0, w)],
                              axis=0).astype(jnp.bfloat16)
        acc_lo = jnp.zeros((8, HALF_D), jnp.float32)
        acc_hi = jnp.zeros((8, HALF_D), jnp.float32)
        for i in range(8):
            b = _gather_rows(idx_ref, base + i, tab_ref, (scr0, scr1)[i % 2])
            r = jnp.dot(lhs, b, preferred_element_type=jnp.float32)
            acc_lo = jnp.where(row == i, r[0:8], acc_lo)
            acc_hi = jnp.where(row == i, r[8:16], acc_hi)
        y_lo = h_ref[pl.ds(base, 8), 0:HALF_D] + acc_lo
        y_hi = h_ref[pl.ds(base, 8), HALF_D:] + acc_hi
        ms = (jnp.sum(y_lo * y_lo, axis=-1, keepdims=True)
              + jnp.sum(y_hi * y_hi, axis=-1, keepdims=True)) * (1.0 / D_MODEL)
        inv = lax.rsqrt(ms + EPS)
        out_ref[pl.ds(base, 8), 0:HALF_D] = y_lo * inv * nw_ref[:, 0:HALF_D]
        out_ref[pl.ds(base, 8), HALF_D:] = y_hi * inv * nw_ref[:, HALF_D:]

    _token_groups(eight_tokens)


def _peer_scratch():
    return [pltpu.VMEM((PEER_K // 8, 8 * (HALF_D // 128), 128), jnp.uint32) for _ in range(2)]


def _peer_params(tab):
    return _params(1, _nbytes(tab))


def _peer_u_pass(idx_flat, un, tab):
    t = un.shape[0]
    return pl.pallas_call(
        _peer_u_kernel,
        grid=(t // GATHER_TB,),
        in_specs=[pl.BlockSpec((GATHER_TB * PEER_K,), lambda i: (i,), memory_space=pltpu.SMEM),
                  pl.BlockSpec((GATHER_TB, D_MODEL), lambda i: (i, 0)),
                  pl.BlockSpec(memory_space=pltpu.VMEM)],
        out_specs=pl.BlockSpec((GATHER_TB, 2 * PEER_K), lambda i: (i, 0)),
        out_shape=jax.ShapeDtypeStruct((t, 2 * PEER_K), jnp.float32),
        scratch_shapes=_peer_scratch(),
        compiler_params=_peer_params(tab),
        name="peer_u_pass",
    )(idx_flat, un, tab)


def _peer_v_pass(idx_flat, a, gate2, h, norm_w, tab):
    t = a.shape[0]
    row = lambda n: pl.BlockSpec((GATHER_TB, n), lambda i: (i, 0))
    return pl.pallas_call(
        _peer_v_kernel,
        grid=(t // GATHER_TB,),
        in_specs=[pl.BlockSpec((GATHER_TB * PEER_K,), lambda i: (i,), memory_space=pltpu.SMEM),
                  row(2 * PEER_K), row(2 * PEER_K), row(D_MODEL),
                  pl.BlockSpec((1, D_MODEL), lambda i: (0, 0)),
                  pl.BlockSpec(memory_space=pltpu.VMEM)],
        out_specs=row(D_MODEL),
        out_shape=jax.ShapeDtypeStruct((t, D_MODEL), jnp.float32),
        scratch_shapes=_peer_scratch(),
        compiler_params=_peer_params(tab),
        name="peer_v_pass",
    )(idx_flat, a, gate2, h, norm_w.reshape(1, D_MODEL), tab)


def _peer_and_final_norm(h, un, q, sub_keys, peer_u, peer_v, norm_final_w):
    idx, gate2 = _peer_topk(q, sub_keys)
    idx_flat = idx.reshape(-1)
    a = _peer_u_pass(idx_flat, un, _pack_table(peer_u))
    return _peer_v_pass(idx_flat, a, gate2, h, norm_final_w, _pack_table(peer_v))


def kernel(x, positions, norm_mix_w, w_in, conv_w, conv_b, dt_bias_fwd, dt_bias_bwd, a_log_fwd, a_log_bwd, d_skip, ssd_norm_w, q_a_norm_w, w_q_b, kv_a_norm_w, w_kv_b, w_proj_ssd, w_proj_mla, w_out, norm_ffn_w, peer_w_q, peer_sub_keys, peer_u, peer_v, norm_final_w):
    b, s, _ = x.shape
    x2 = x.reshape(b * s, D_MODEL)
    z, xbc, dt128, q_a, kv_a, kr, krs, gate_logits = _in_proj(x2, norm_mix_w, w_in)
    y_ssd = _ssd_pallas(xbc.reshape(b, s, -1), dt128.reshape(b, s, -1), z.reshape(b, s, -1), conv_w, conv_b,
                        dt_bias_fwd, dt_bias_bwd, a_log_fwd, a_log_bwd, d_skip, ssd_norm_w)

    y_mla = _mla_pallas(positions, q_a, kv_a, kr, krs, q_a_norm_w, kv_a_norm_w, w_q_b, w_kv_b)

    h2, un, q_peer = _merge(x2, y_ssd.reshape(b * s, -1), y_mla, gate_logits, norm_ffn_w,
                            w_proj_ssd, w_proj_mla, w_out, peer_w_q)
    out = _peer_and_final_norm(h2, un, q_peer, peer_sub_keys, peer_u, peer_v, norm_final_w)
    return out.reshape(b, s, D_MODEL)
```

```python
import functools

import jax
import jax.numpy as jnp
from jax import lax
from jax.experimental import pallas as pl
from jax.experimental.pallas import tpu as pltpu

D_MODEL = 1024
SSD_D_INNER = 1024
SSD_HEADDIM = 64
SSD_HEADS = 16
SSD_GROUPS = 2
SSD_HPG = 8
SSD_STATE = 128
CHUNK = 128
SSD_CONV_DIM = SSD_D_INNER + 2 * SSD_GROUPS * SSD_STATE
MLA_HEADS = 16
QK_NOPE = 64
QK_ROPE = 32
V_HEAD = 64
Q_LORA = 384
KV_LORA = 256
ROPE_THETA = 10000.0
PEER_HEADS = 8
N_KEYS = 128
KEY_HALF = 128
PEER_TOPK = 16
EPS = 1e-6
HEAD_PAD = 128
QK_DIM = QK_NOPE + QK_ROPE
LOG2E = 1.4426950408889634
IN_SIZES = (SSD_D_INNER, SSD_CONV_DIM, 2 * SSD_HEADS, Q_LORA, KV_LORA, QK_ROPE, 2 * D_MODEL)


def _split_points(sizes):
    pts, acc = [], 0
    for s in sizes[:-1]:
        acc += s
        pts.append(acc)
    return pts


VMEM_BYTES_V7X = 64 << 20
VMEM_CAP = VMEM_BYTES_V7X - (8 << 20)
BLOCK_ALLOWANCE = 24 << 20


def _params(grid_rank, resident_bytes=0):
    return pltpu.CompilerParams(dimension_semantics=("arbitrary",) * grid_rank,
                                vmem_limit_bytes=min(resident_bytes + BLOCK_ALLOWANCE, VMEM_CAP))


def _nbytes(*arrays):
    return sum(a.size * a.dtype.itemsize for a in arrays)


def _rms_rows(x, w):
    return x * lax.rsqrt(jnp.mean(x * x, axis=-1, keepdims=True) + EPS) * w


IN_TB = 512
DT_PAD = 128
GATE_W = 2 * D_MODEL
IN_OUT_WIDTHS = (SSD_D_INNER, SSD_CONV_DIM, DT_PAD, Q_LORA, KV_LORA, HEAD_PAD, HEAD_PAD, GATE_W)


def _in_proj_kernel(x_ref, nw_ref, w_ref, *out_refs):
    u = _rms_rows(x_ref[...], nw_ref[...]).astype(jnp.bfloat16)
    col = 0
    for o_ref, width in zip(out_refs, IN_OUT_WIDTHS):
        o_ref[...] = jnp.dot(u, w_ref[:, col:col + width], preferred_element_type=jnp.float32)
        col += width


def _arrange_w_in(w_in):
    z, xbc, dt, qa, kva, kr, gates = jnp.split(w_in, _split_points(IN_SIZES), axis=1)
    d = w_in.shape[0]
    half = QK_ROPE // 2
    dt_p = jnp.pad(dt, ((0, 0), (0, DT_PAD - dt.shape[1])))
    kr3 = kr.reshape(d, 1, QK_ROPE)
    kr_p = _pad_heads(None, kr3)
    krs_p = _pad_heads(None, _rope_partner(kr3[..., :half], kr3[..., half:]))
    return jnp.concatenate([z, xbc, dt_p, qa, kva, kr_p, krs_p, gates], axis=1).astype(jnp.bfloat16)


def _in_proj(x2, norm_w, w_in):
    t, d = x2.shape
    w = _arrange_w_in(w_in)
    tb = min(IN_TB, t)
    return pl.pallas_call(
        _in_proj_kernel,
        grid=(t // tb,),
        in_specs=[pl.BlockSpec((tb, d), lambda i: (i, 0)),
                  pl.BlockSpec((1, d), lambda i: (0, 0)),
                  pl.BlockSpec(w.shape, lambda i: (0, 0), pipeline_mode=pl.Buffered(1))],
        out_specs=[pl.BlockSpec((tb, n), lambda i: (i, 0)) for n in IN_OUT_WIDTHS],
        out_shape=[jax.ShapeDtypeStruct((t, n), jnp.float32) for n in IN_OUT_WIDTHS],
        compiler_params=_params(1, _nbytes(w) + (16 << 20)),
        name="in_proj",
    )(x2, norm_w.reshape(1, d), w)


HEAD_W = SSD_HEADS * SSD_HEADDIM
GROUP_W = HEAD_W // SSD_GROUPS
BC_W = SSD_GROUPS * SSD_STATE


def _split3(x):
    f32, bf16 = jnp.float32, jnp.bfloat16
    hi = x.astype(bf16)
    r = x - hi.astype(f32)
    mid = r.astype(bf16)
    lo = (r - mid.astype(f32)).astype(bf16)
    return hi, mid, lo


def _ssd_kernel(direction, xc_ref, xp_ref, xn_ref, dt_ref, cw_ref, cb_ref, dtb_ref, a_ref, e_ref,
                tri_ref, *rest):
    f32, bf16 = jnp.float32, jnp.bfloat16
    fwd = direction == 0
    if fwd:
        dsk_ref, y_ref, h_scr = rest
    else:
        y1_ref, z_ref, nw_ref, y_ref, h_scr = rest
    c = pl.program_id(1)
    nc = pl.num_programs(1)
    cm = c if fwd else nc - 1 - c

    @pl.when(c == 0)
    def _():
        h_scr[...] = jnp.zeros_like(h_scr)

    xc = xc_ref[0]
    row = lax.broadcasted_iota(jnp.int32, xc.shape, 0)
    prev_row = jnp.where(cm > 0, xp_ref[0, 7:8, :], 0.0)
    next_row = jnp.where(cm < nc - 1, xn_ref[0, 0:1, :], 0.0)
    x_m1 = jnp.where(row == 0, prev_row, pltpu.roll(xc, 1, axis=0))
    x_p1 = jnp.where(row == CHUNK - 1, next_row, pltpu.roll(xc, CHUNK - 1, axis=0))
    conv = cw_ref[0:1, :] * x_m1 + cw_ref[1:2, :] * xc + cw_ref[2:3, :] * x_p1 + cb_ref[...]
    act = conv * jax.nn.sigmoid(conv)
    xs = act[:, :HEAD_W]

    dtr = dt_ref[0] + dtb_ref[...]
    dt = jnp.maximum(dtr, 0.0) + jnp.log(1.0 + jnp.exp(-jnp.abs(dtr)))
    a = dt * a_ref[...]
    cs = sum(jnp.dot(tri_ref[...], part, preferred_element_type=f32) for part in _split3(a))
    expand = lambda v: sum(jnp.dot(part, e_ref[...], preferred_element_type=f32) for part in _split3(v))
    cs_e = expand(cs)
    tot_e = cs_e[CHUNK - 1:CHUNK, :]
    if fwd:
        pos = cs
        w_state = jnp.exp(tot_e - cs_e)
        out_scale = jnp.exp(cs_e)
    else:
        pos = cs - a
        e_e = cs_e - expand(a)
        w_state = jnp.exp(e_e)
        out_scale = jnp.exp(tot_e - e_e)
    pos_t = pos.T
    xd = xs * expand(dt)
    xdb = xd.astype(bf16)
    xw = (xd * w_state).astype(bf16)
    chunk_decay = jnp.exp(tot_e)

    li = lax.broadcasted_iota(jnp.int32, (CHUNK, CHUNK), 0)
    si = lax.broadcasted_iota(jnp.int32, (CHUNK, CHUNK), 1)
    mask = (si <= li) if fwd else (si >= li)
    for g in range(SSD_GROUPS):
        gsl = slice(g * GROUP_W, (g + 1) * GROUP_W)
        bg = act[:, HEAD_W + g * SSD_STATE:HEAD_W + (g + 1) * SSD_STATE]
        cg = act[:, HEAD_W + BC_W + g * SSD_STATE:HEAD_W + BC_W + (g + 1) * SSD_STATE].astype(bf16)
        cb = lax.dot_general(cg, bg.astype(bf16), (((1,), (1,)), ((), ())), preferred_element_type=f32)
        h_in = h_scr[:, gsl]
        y_off = jnp.dot(cg, h_in.astype(bf16), preferred_element_type=f32) * out_scale[:, gsl]
        s_new = jnp.dot(bg.T.astype(bf16), xw[:, gsl], preferred_element_type=f32)
        h_scr[:, gsl] = h_in * chunk_decay[:, gsl] + s_new
        for r in range(SSD_HPG):
            hd = g * SSD_HPG + r
            lane = direction * SSD_HEADS + hd
            col = pos[:, lane:lane + 1]
            rw = pos_t[lane:lane + 1, :]
            diff = (col - rw) if fwd else (rw - col)
            m = (cb * jnp.exp(jnp.where(mask, diff, NEG_INF))).astype(bf16)
            hsl = slice(hd * SSD_HEADDIM, (hd + 1) * SSD_HEADDIM)
            y_d = jnp.dot(m, xdb[:, hsl], preferred_element_type=f32)
            y_ref[0, :, hsl] = y_d + y_off[:, r * SSD_HEADDIM:(r + 1) * SSD_HEADDIM]

    if fwd:
        y_ref[0] = y_ref[0] + xs * dsk_ref[...]
    else:
        yg = (y_ref[0] + y1_ref[0]) * (z_ref[0] * jax.nn.sigmoid(z_ref[0]))
        for g in range(SSD_GROUPS):
            gsl = slice(g * GROUP_W, (g + 1) * GROUP_W)
            y_ref[0, :, gsl] = _rms_rows(yg[:, gsl], nw_ref[:, gsl])


def _ssd_pass(direction, xbc, dt128, consts, extra):
    b, s, cw_ = xbc.shape
    nc = s // CHUNK
    per8 = CHUNK // 8
    cm = (lambda c: c) if direction == 0 else (lambda c: nc - 1 - c)
    cur = lambda n: pl.BlockSpec((1, CHUNK, n), lambda bi, c: (bi, cm(c), 0))
    prev = pl.BlockSpec((1, 8, cw_), lambda bi, c: (bi, jnp.maximum(cm(c) * per8 - 1, 0), 0))
    nxt = pl.BlockSpec((1, 8, cw_), lambda bi, c: (bi, jnp.minimum((cm(c) + 1) * per8, s // 8 - 1), 0))
    full = lambda a: pl.BlockSpec(a.shape, lambda bi, c: (0,) * a.ndim)
    if direction == 0:
        extra_specs = [full(extra[0])]
    else:
        extra_specs = [cur(HEAD_W), cur(HEAD_W), full(extra[2])]
    return pl.pallas_call(
        functools.partial(_ssd_kernel, direction),
        grid=(b, nc),
        in_specs=[cur(cw_), prev, nxt, cur(DT_PAD)] + [full(a) for a in consts] + extra_specs,
        out_specs=cur(HEAD_W),
        out_shape=jax.ShapeDtypeStruct((b, s, HEAD_W), jnp.float32),
        scratch_shapes=[pltpu.VMEM((SSD_STATE, HEAD_W), jnp.float32)],
        compiler_params=_params(2),
        name="ssd_fwd" if direction == 0 else "ssd_bwd",
    )(xbc, xbc, xbc, dt128, *consts, *extra)


def _ssd_pallas(xbc, dt128, z, conv_w, conv_b, dt_bias_fwd, dt_bias_bwd, a_log_fwd, a_log_bwd,
                d_skip, ssd_norm_w):
    bf16 = jnp.bfloat16
    pad = lambda v: jnp.pad(v, (0, DT_PAD - v.shape[0])).reshape(1, DT_PAD)
    dtb = pad(jnp.concatenate([dt_bias_fwd, dt_bias_bwd]))
    a_neg = pad(-jnp.exp(jnp.concatenate([a_log_fwd, a_log_bwd])))
    tri = (jnp.arange(CHUNK)[None, :] <= jnp.arange(CHUNK)[:, None]).astype(bf16)
    head_of_lane = jnp.arange(HEAD_W) // SSD_HEADDIM
    y = None
    for direction in range(2):
        e = (jnp.arange(DT_PAD)[:, None] == (direction * SSD_HEADS + head_of_lane)[None, :]).astype(bf16)
        consts = (conv_w, conv_b.reshape(1, -1), dtb, a_neg, e, tri)
        if direction == 0:
            extra = (jnp.repeat(d_skip, SSD_HEADDIM).reshape(1, HEAD_W),)
        else:
            extra = (y, z, ssd_norm_w.reshape(1, HEAD_W))
        y = _ssd_pass(direction, xbc, dt128, consts, extra)
    return y


MERGE_TB = 512


def _merge_kernel(x_ref, ys_ref, ym_ref, g_ref, nw_ref, pa_ref, pb_ref, wo_ref, wq_ref,
                  h_ref, un_ref, q_ref):
    bf16 = jnp.bfloat16
    f32 = jnp.float32
    g = jax.nn.sigmoid(g_ref[...])
    ma = jnp.dot(ys_ref[...].astype(bf16), pa_ref[...], preferred_element_type=f32)
    mb = jnp.dot(ym_ref[...].astype(bf16), pb_ref[...], preferred_element_type=f32)
    m = g[:, :D_MODEL] * ma + g[:, D_MODEL:] * mb
    h = x_ref[...] + jnp.dot(m.astype(bf16), wo_ref[...], preferred_element_type=f32)
    h_ref[...] = h
    un = _rms_rows(h, nw_ref[...])
    un_ref[...] = un
    q_ref[...] = jnp.dot(un.astype(bf16), wq_ref[...], preferred_element_type=f32)


def _merge(x2, y_ssd, y_mla, gates, norm_ffn_w, w_proj_ssd, w_proj_mla, w_out, peer_w_q):
    t, d = x2.shape
    bf16 = jnp.bfloat16
    tb = min(MERGE_TB, t)
    nq = peer_w_q.shape[1]
    row = lambda n: pl.BlockSpec((tb, n), lambda i: (i, 0))
    full = lambda a: pl.BlockSpec(a.shape, lambda i: (0, 0), pipeline_mode=pl.Buffered(1))
    ws = [w.astype(bf16) for w in (w_proj_ssd, w_proj_mla, w_out, peer_w_q)]
    nw = norm_ffn_w.reshape(1, d)
    return pl.pallas_call(
        _merge_kernel,
        grid=(t // tb,),
        in_specs=[row(d), row(d), row(d), row(GATE_W), full(nw)] + [full(w) for w in ws],
        out_specs=[row(d), row(d), row(nq)],
        out_shape=[jax.ShapeDtypeStruct((t, d), jnp.float32), jax.ShapeDtypeStruct((t, d), jnp.float32),
                   jax.ShapeDtypeStruct((t, nq), jnp.float32)],
        compiler_params=_params(1, _nbytes(*ws) + (16 << 20)),
        name="merge",
    )(x2, y_ssd, y_mla, gates, nw, *ws)


MLA_TB = 512
ATT_TQ = 256
ATT_HEADS = 8


def _mla_proj_kernel(pos_ref, qa_ref, kva_ref, kr_ref, krs_ref, qnw_ref, kvnw_ref, inv_ref,
                     wq_ref, wqs_ref, wk_ref, wv_ref, q_out, k_out, v_out):
    bf16 = jnp.bfloat16
    ang = pos_ref[...].astype(jnp.float32) * inv_ref[...]
    cos, sin = jnp.cos(ang), jnp.sin(ang)
    qa = _rms_rows(qa_ref[...], qnw_ref[...]).astype(bf16)
    q = jnp.dot(qa, wq_ref[...], preferred_element_type=jnp.float32)
    qs = jnp.dot(qa, wqs_ref[...], preferred_element_type=jnp.float32)
    kva = _rms_rows(kva_ref[...], kvnw_ref[...]).astype(bf16)
    k = jnp.dot(kva, wk_ref[...], preferred_element_type=jnp.float32)
    v_out[...] = jnp.dot(kva, wv_ref[...], preferred_element_type=jnp.float32).astype(bf16)
    kro = kr_ref[...] * cos + krs_ref[...] * sin
    c = (QK_DIM ** -0.5) * LOG2E
    for h in range(MLA_HEADS):
        sl = slice(h * HEAD_PAD, (h + 1) * HEAD_PAD)
        q_out[:, sl] = ((q[:, sl] * cos + qs[:, sl] * sin) * c).astype(bf16)
        k_out[:, sl] = (k[:, sl] + kro).astype(bf16)


def _rope_partner(w1, w2):
    return jnp.concatenate([-w2, w1], axis=-1)


def _pad_heads(nope, rope):
    ref = nope if nope is not None else rope
    lead = ref.shape[:-1]
    z = lambda n: jnp.zeros(lead + (n,), ref.dtype)
    parts = [nope if nope is not None else z(QK_NOPE), rope if rope is not None else z(QK_ROPE),
             z(HEAD_PAD - QK_DIM)]
    out = jnp.concatenate(parts, axis=-1)
    return out.reshape(out.shape[:-2] + (out.shape[-2] * HEAD_PAD,))


def _mla_proj(positions, q_a, kv_a, kr, krs, q_a_norm_w, kv_a_norm_w, w_q_b, w_kv_b):
    t = q_a.shape[0]
    bf16 = jnp.bfloat16
    half = QK_ROPE // 2
    wq = w_q_b.reshape(Q_LORA, MLA_HEADS, QK_DIM)
    wq_n, wq_r = wq[..., :QK_NOPE], wq[..., QK_NOPE:]
    wq_main = _pad_heads(wq_n, wq_r).astype(bf16)
    wq_part = _pad_heads(None, _rope_partner(wq_r[..., :half], wq_r[..., half:])).astype(bf16)
    wkv = w_kv_b.reshape(KV_LORA, MLA_HEADS, QK_NOPE + V_HEAD)
    wk = _pad_heads(wkv[..., :QK_NOPE], None).astype(bf16)
    wv = wkv[..., QK_NOPE:].reshape(KV_LORA, MLA_HEADS * V_HEAD).astype(bf16)
    inv = 1.0 / (ROPE_THETA ** (jnp.arange(0, QK_ROPE, 2, dtype=jnp.float32) / QK_ROPE))
    inv_lane = jnp.concatenate([jnp.zeros((QK_NOPE,), jnp.float32), inv, inv,
                                jnp.zeros((HEAD_PAD - QK_DIM,), jnp.float32)]).reshape(1, HEAD_PAD)
    tb = min(MLA_TB, t)
    row = lambda n: pl.BlockSpec((tb, n), lambda i: (i, 0))
    full = lambda a: pl.BlockSpec(a.shape, lambda i: (0,) * a.ndim)
    qn = q_a_norm_w.reshape(1, Q_LORA)
    kvn = kv_a_norm_w.reshape(1, KV_LORA)
    args = (positions.reshape(t, 1), q_a, kv_a, kr, krs, qn, kvn, inv_lane, wq_main, wq_part, wk, wv)
    in_specs = [row(1), row(Q_LORA), row(KV_LORA), row(HEAD_PAD), row(HEAD_PAD)] + [full(a) for a in args[5:]]
    nq = MLA_HEADS * HEAD_PAD
    return pl.pallas_call(
        _mla_proj_kernel,
        grid=(t // tb,),
        in_specs=in_specs,
        out_specs=[row(nq), row(nq), row(MLA_HEADS * V_HEAD)],
        out_shape=[jax.ShapeDtypeStruct((t, nq), bf16), jax.ShapeDtypeStruct((t, nq), bf16),
                   jax.ShapeDtypeStruct((t, MLA_HEADS * V_HEAD), bf16)],
        compiler_params=_params(1, 2 * _nbytes(wq_main, wq_part, wk, wv)),
        name="mla_proj",
    )(*args)


def _attn_kernel(q_ref, k_ref, v_ref, o_ref):
    tq = q_ref.shape[1]
    lane = lax.broadcasted_iota(jnp.int32, (tq, 2 * V_HEAD), 1)
    for pair in range(ATT_HEADS // 2):
        vsl = slice(pair * 2 * V_HEAD, (pair + 1) * 2 * V_HEAD)
        v = v_ref[0, :, vsl]
        outs = []
        for hh in range(2 * pair, 2 * pair + 2):
            sl = slice(hh * HEAD_PAD, (hh + 1) * HEAD_PAD)
            s = lax.dot_general(q_ref[0, :, sl], k_ref[0, :, sl], (((1,), (1,)), ((), ())),
                                preferred_element_type=jnp.float32)
            m = jnp.max(s, axis=-1, keepdims=True)
            p = jnp.exp2(s - m)
            l = jnp.sum(p, axis=-1, keepdims=True)
            pv = jnp.dot(p.astype(jnp.bfloat16), v, preferred_element_type=jnp.float32)
            outs.append(pv / l)
        o_ref[0, :, vsl] = jnp.where(lane < V_HEAD, outs[0], outs[1]).astype(o_ref.dtype)


def _attention(q, k, v):
    b, s, _ = q.shape
    tq = min(ATT_TQ, s)
    qw, vw = ATT_HEADS * HEAD_PAD, ATT_HEADS * V_HEAD
    return pl.pallas_call(
        _attn_kernel,
        grid=(b, MLA_HEADS // ATT_HEADS, s // tq),
        in_specs=[pl.BlockSpec((1, tq, qw), lambda bi, j, i: (bi, i, j)),
                  pl.BlockSpec((1, s, qw), lambda bi, j, i: (bi, 0, j)),
                  pl.BlockSpec((1, s, vw), lambda bi, j, i: (bi, 0, j))],
        out_specs=pl.BlockSpec((1, tq, vw), lambda bi, j, i: (bi, i, j)),
        out_shape=jax.ShapeDtypeStruct((b, s, MLA_HEADS * V_HEAD), jnp.bfloat16),
        compiler_params=_params(3, 2 * 2 * s * (qw + vw)),
        name="mla_attention",
    )(q, k, v)


def _mla_pallas(positions, q_a, kv_a, kr, krs, q_a_norm_w, kv_a_norm_w, w_q_b, w_kv_b):
    b, s = positions.shape
    q, k, v = _mla_proj(positions, q_a, kv_a, kr, krs, q_a_norm_w, kv_a_norm_w, w_q_b, w_kv_b)
    y = _attention(q.reshape(b, s, -1), k.reshape(b, s, -1), v.reshape(b, s, -1))
    return y.reshape(b * s, -1)


PEER_TB = 128
PEER_K = PEER_HEADS * PEER_TOPK
PEER_GROUP = 128
GATHER_TB = 256
HALF_D = D_MODEL // 2
NEG_INF = float("-inf")


def _top16_rows(xs, extras=None):
    n = len(xs)
    rows = xs[0].shape[0]
    iota = lax.broadcasted_iota(jnp.int32, xs[0].shape, 0).astype(jnp.float32)
    xs = list(xs)
    vals, poss, exts = [[] for _ in xs], [[] for _ in xs], [[] for _ in xs]
    for _ in range(PEER_TOPK):
        for j in range(n):
            m = jnp.max(xs[j], axis=0, keepdims=True)
            p = jnp.min(jnp.where(xs[j] == m, iota, float(rows)), axis=0, keepdims=True)
            hit = iota == p
            if extras is not None:
                exts[j].append(jnp.sum(jnp.where(hit, extras[j], 0.0), axis=0, keepdims=True))
            xs[j] = jnp.where(hit, NEG_INF, xs[j])
            vals[j].append(m)
            poss[j].append(p)
    cat = lambda parts: jnp.concatenate(parts, axis=0)
    return [(cat(vals[j]), cat(poss[j]), cat(exts[j]) if extras is not None else None) for j in range(n)]


_CAND_PAIRS = [(a, b) for a in range(PEER_TOPK) for b in range(PEER_TOPK) if (a + 1) * (b + 1) <= PEER_TOPK]
CAND_ROWS = -(-len(_CAND_PAIRS) // 8) * 8
ROW_WORDS = 4
TOPK_LOCKSTEP_HEADS = 4


def _cand_select():
    g0 = [[0.0] * PEER_TOPK for _ in range(CAND_ROWS)]
    g1 = [[0.0] * PEER_TOPK for _ in range(CAND_ROWS)]
    for r, (a, b) in enumerate(_CAND_PAIRS):
        g0[r][a] = 1.0
        g1[r][b] = 1.0
    return jnp.array(g0, jnp.float32), jnp.array(g1, jnp.float32)


def _peer_topk_kernel(q_ref, keys_ref, g0_ref, g1_ref, dup_ref, idx_ref, gate_ref):
    f32 = jnp.float32
    pick = lambda g_ref, v: jnp.dot(g_ref[...], v, precision=lax.Precision.HIGHEST, preferred_element_type=f32)
    pad_row = lax.broadcasted_iota(jnp.int32, (CAND_ROWS, q_ref.shape[0]), 0) >= len(_CAND_PAIRS)
    idx_rows, gate_rows = [], []
    for h0 in range(0, PEER_HEADS, TOPK_LOCKSTEP_HEADS):
        heads = range(h0, h0 + TOPK_LOCKSTEP_HEADS)
        scores = []
        for h in heads:
            for i in range(2):
                c0 = (h * 2 + i) * KEY_HALF
                scores.append(lax.dot_general(keys_ref[i], q_ref[:, c0:c0 + KEY_HALF], (((1,), (1,)), ((), ())),
                                              preferred_element_type=f32))
        tops = _top16_rows(scores)
        cands, cidxs = [], []
        for j in range(len(heads)):
            (v0, p0, _), (v1, p1, _) = tops[2 * j], tops[2 * j + 1]
            cands.append(jnp.where(pad_row, NEG_INF, pick(g0_ref, v0) + pick(g1_ref, v1)))
            cidxs.append((pick(g0_ref, p0) * N_KEYS + pick(g1_ref, p1)) * ROW_WORDS)
        for best, _, eidx in _top16_rows(cands, cidxs):
            e = jnp.exp(best - best[0:1, :])
            idx_rows.append(eidx)
            gate_rows.append(e / jnp.sum(e, axis=0, keepdims=True))
    idx_ref[...] = jnp.concatenate(idx_rows, axis=0).T.astype(jnp.int32)
    gate_t = jnp.concatenate(gate_rows, axis=0).T
    gate_ref[...] = sum(jnp.dot(part, dup_ref[...], preferred_element_type=f32) for part in _split3(gate_t))


def _peer_topk(q, sub_keys):
    t = q.shape[0]
    g0, g1 = _cand_select()
    dup = (jnp.arange(PEER_K)[:, None] == jnp.arange(2 * PEER_K)[None, :] // 2).astype(jnp.bfloat16)
    full = lambda a: pl.BlockSpec(a.shape, lambda i: (0,) * a.ndim)
    return pl.pallas_call(
        _peer_topk_kernel,
        grid=(t // PEER_TB,),
        in_specs=[pl.BlockSpec((PEER_TB, q.shape[1]), lambda i: (i, 0)),
                  full(sub_keys), full(g0), full(g1), full(dup)],
        out_specs=[pl.BlockSpec((PEER_TB, PEER_K), lambda i: (i, 0)),
                   pl.BlockSpec((PEER_TB, 2 * PEER_K), lambda i: (i, 0))],
        out_shape=[jax.ShapeDtypeStruct((t, PEER_K), jnp.int32),
                   jax.ShapeDtypeStruct((t, 2 * PEER_K), jnp.float32)],
        name="peer_topk",
    )(q, sub_keys, g0, g1, dup)


def _pack_table(tab):
    e = tab.shape[0]
    tb = tab.astype(jnp.bfloat16)
    lo = lax.bitcast_convert_type(tb[:, :HALF_D], jnp.uint16).astype(jnp.uint32)
    hi = lax.bitcast_convert_type(tb[:, HALF_D:], jnp.uint16).astype(jnp.uint32)
    return (lo | (hi << 16)).reshape(e * ROW_WORDS, 128)


def _token_groups(eight_tokens):
    def trip(g, carry):
        for sub in range(PEER_GROUP // 8):
            eight_tokens(pl.multiple_of(g * PEER_GROUP + sub * 8, 8))
        return carry

    lax.fori_loop(0, GATHER_TB // PEER_GROUP, trip, 0)


def _gather_rows(idx_ref, tok, tab_ref, scr):
    nchunk = HALF_D // 128
    for k in range(PEER_K):
        e = idx_ref[tok * PEER_K + k]
        scr[k // 8, pl.ds(k % 8, nchunk, stride=8), :] = tab_ref[pl.ds(pl.multiple_of(e, ROW_WORDS), ROW_WORDS), :]
    gm = jnp.concatenate(
        [scr[:, j * 8:(j + 1) * 8, :].reshape(PEER_K, 128) for j in range(nchunk)], axis=1)
    return pltpu.bitcast(gm, jnp.bfloat16)


def _peer_u_kernel(idx_ref, ut_ref, tab_ref, out_ref, scr0, scr1):
    shp = (8, 2 * PEER_K)
    row = lax.broadcasted_iota(jnp.int32, shp, 0)
    even = (lax.broadcasted_iota(jnp.int32, shp, 1) & 1) == 0

    def eight_tokens(base):
        ut8 = ut_ref[pl.ds(base, 8), :]
        lhs = jnp.concatenate([ut8[:, :HALF_D], ut8[:, HALF_D:]], axis=0).astype(jnp.bfloat16)
        acc = jnp.zeros(shp, jnp.float32)
        for i in range(8):
            b = _gather_rows(idx_ref, base + i, tab_ref, (scr0, scr1)[i % 2])
            r = lax.dot_general(lhs, b, (((1,), (1,)), ((), ())),
                                preferred_element_type=jnp.float32)
            acc = jnp.where(row == i, jnp.where(even, r[0:8], r[8:16]), acc)
        out_ref[pl.ds(base, 8), :] = acc

    _token_groups(eight_tokens)


def _peer_v_kernel(idx_ref, a_ref, gate_ref, h_ref, nw_ref, tab_ref, out_ref, scr0, scr1):
    shp = (8, 2 * PEER_K)
    row = lax.broadcasted_iota(jnp.int32, (8, HALF_D), 0)
    even = (lax.broadcasted_iota(jnp.int32, shp, 1) & 1) == 0

    def eight_tokens(base):
        a8 = a_ref[pl.ds(base, 8), :]
        s = a8 + pltpu.roll(a8, 2 * PEER_K - 1, axis=1)
        act = jnp.where(even, s, pltpu.roll(s, 1, axis=1))
        gelu = 0.5 * act * (1.0 + lax.erf(act * (2.0 ** -0.5)))
        w = gelu * gate_ref[pl.ds(base, 8), :]
        lhs = jnp.concatenate([jnp.where(even, w, 0.0), jnp.where(even, 0.0, w)],
                              axis=0).astype(jnp.bfloat16)
        acc_lo = jnp.zeros((8, HALF_D), jnp.float32)
        acc_hi = jnp.zeros((8, HALF_D), jnp.float32)
        for i in range(8):
            b = _gather_rows(idx_ref, base + i, tab_ref, (scr0, scr1)[i % 2])
            r = jnp.dot(lhs, b, preferred_element_type=jnp.float32)
            acc_lo = jnp.where(row == i, r[0:8], acc_lo)
            acc_hi = jnp.where(row == i, r[8:16], acc_hi)
        y_lo = h_ref[pl.ds(base, 8), 0:HALF_D] + acc_lo
        y_hi = h_ref[pl.ds(base, 8), HALF_D:] + acc_hi
        ms = (jnp.sum(y_lo * y_lo, axis=-1, keepdims=True)
              + jnp.sum(y_hi * y_hi, axis=-1, keepdims=True)) * (1.0 / D_MODEL)
        inv = lax.rsqrt(ms + EPS)
        out_ref[pl.ds(base, 8), 0:HALF_D] = y_lo * inv * nw_ref[:, 0:HALF_D]
        out_ref[pl.ds(base, 8), HALF_D:] = y_hi * inv * nw_ref[:, HALF_D:]

    _token_groups(eight_tokens)


def _peer_scratch():
    return [pltpu.VMEM((PEER_K // 8, 8 * (HALF_D // 128), 128), jnp.uint32) for _ in range(2)]


def _peer_params(tab):
    return _params(1, _nbytes(tab))


def _peer_u_pass(idx_flat, un, tab):
    t = un.shape[0]
    return pl.pallas_call(
        _peer_u_kernel,
        grid=(t // GATHER_TB,),
        in_specs=[pl.BlockSpec((GATHER_TB * PEER_K,), lambda i: (i,), memory_space=pltpu.SMEM),
                  pl.BlockSpec((GATHER_TB, D_MODEL), lambda i: (i, 0)),
                  pl.BlockSpec(memory_space=pltpu.VMEM)],
        out_specs=pl.BlockSpec((GATHER_TB, 2 * PEER_K), lambda i: (i, 0)),
        out_shape=jax.ShapeDtypeStruct((t, 2 * PEER_K), jnp.float32),
        scratch_shapes=_peer_scratch(),
        compiler_params=_peer_params(tab),
        name="peer_u_pass",
    )(idx_flat, un, tab)


def _peer_v_pass(idx_flat, a, gate2, h, norm_w, tab):
    t = a.shape[0]
    row = lambda n: pl.BlockSpec((GATHER_TB, n), lambda i: (i, 0))
    return pl.pallas_call(
        _peer_v_kernel,
        grid=(t // GATHER_TB,),
        in_specs=[pl.BlockSpec((GATHER_TB * PEER_K,), lambda i: (i,), memory_space=pltpu.SMEM),
                  row(2 * PEER_K), row(2 * PEER_K), row(D_MODEL),
                  pl.BlockSpec((1, D_MODEL), lambda i: (0, 0)),
                  pl.BlockSpec(memory_space=pltpu.VMEM)],
        out_specs=row(D_MODEL),
        out_shape=jax.ShapeDtypeStruct((t, D_MODEL), jnp.float32),
        scratch_shapes=_peer_scratch(),
        compiler_params=_peer_params(tab),
        name="peer_v_pass",
    )(idx_flat, a, gate2, h, norm_w.reshape(1, D_MODEL), tab)


def _peer_and_final_norm(h, un, q, sub_keys, peer_u, peer_v, norm_final_w):
    idx, gate2 = _peer_topk(q, sub_keys)
    idx_flat = idx.reshape(-1)
    a = _peer_u_pass(idx_flat, un, _pack_table(peer_u))
    return _peer_v_pass(idx_flat, a, gate2, h, norm_final_w, _pack_table(peer_v))


def kernel(x, positions, norm_mix_w, w_in, conv_w, conv_b, dt_bias_fwd, dt_bias_bwd, a_log_fwd, a_log_bwd, d_skip, ssd_norm_w, q_a_norm_w, w_q_b, kv_a_norm_w, w_kv_b, w_proj_ssd, w_proj_mla, w_out, norm_ffn_w, peer_w_q, peer_sub_keys, peer_u, peer_v, norm_final_w):
    b, s, _ = x.shape
    x2 = x.reshape(b * s, D_MODEL)
    z, xbc, dt128, q_a, kv_a, kr, krs, gate_logits = _in_proj(x2, norm_mix_w, w_in)
    y_ssd = _ssd_pallas(xbc.reshape(b, s, -1), dt128.reshape(b, s, -1), z.reshape(b, s, -1), conv_w, conv_b,
                        dt_bias_fwd, dt_bias_bwd, a_log_fwd, a_log_bwd, d_skip, ssd_norm_w)

    y_mla = _mla_pallas(positions, q_a, kv_a, kr, krs, q_a_norm_w, kv_a_norm_w, w_q_b, w_kv_b)

    h2, un, q_peer = _merge(x2, y_ssd.reshape(b * s, -1), y_mla, gate_logits, norm_ffn_w,
                            w_proj_ssd, w_proj_mla, w_out, peer_w_q)
    out = _peer_and_final_norm(h2, un, q_peer, peer_sub_keys, peer_u, peer_v, norm_final_w)
    return out.reshape(b, s, D_MODEL)
```
